```python
import jax, jax.numpy as jnp
from jax import lax
import numpy as np

D_MODEL = 1024
BATCH = 4
SEQ = 4096
DEPTH = 2
DEC_BATCH = 128
DEC_SEQ = 1
PAST_LEN = 16384
PAGE_SIZE = 128

HEAD_DIM = 64
ATTN_WIDTH = D_MODEL // 2
N_HEADS = ATTN_WIDTH // HEAD_DIM
N_KV_HEADS = N_HEADS // 4
GQA_GROUP = N_HEADS // N_KV_HEADS
KV_WIDTH = N_KV_HEADS * HEAD_DIM
WINDOW = 128
ATTN_BLOCK = WINDOW
LRU_WIDTH = D_MODEL // 4
LRU_BLOCKS = 4
LRU_BLOCK_W = LRU_WIDTH // LRU_BLOCKS
LRU_C = 8.0
CONV_W = 4
POOL_WINDOWS = (2, 4, 8, 16)
POOL_WIDTH = D_MODEL // 4
POOL_GROUPS = len(POOL_WINDOWS)
POOL_GROUP_W = POOL_WIDTH // POOL_GROUPS
POOL_CTX = max(POOL_WINDOWS) - 1
MIX_WIDTH = ATTN_WIDTH + LRU_WIDTH + POOL_WIDTH
IN_WIDTH = ATTN_WIDTH + 2 * KV_WIDTH + 2 * LRU_WIDTH + POOL_WIDTH
SPLIT_POINTS = (ATTN_WIDTH, ATTN_WIDTH + KV_WIDTH, ATTN_WIDTH + 2 * KV_WIDTH,
                ATTN_WIDTH + 2 * KV_WIDTH + LRU_WIDTH, ATTN_WIDTH + 2 * KV_WIDTH + 2 * LRU_WIDTH)
D_FF = 4 * D_MODEL
LN_EPS = 1e-5
NEG_INF = -1e30

kernel_name = 'hybrid_swa_rglru_pool_deepnorm_step'


def layer_norm(x, g, b):
    xf = x.astype(jnp.float32)
    mu = jnp.mean(xf, axis=-1, keepdims=True)
    xc = xf - mu
    var = jnp.mean(jnp.square(xc), axis=-1, keepdims=True)
    return (xc * lax.rsqrt(var + LN_EPS) * g.astype(jnp.float32) + b.astype(jnp.float32)).astype(x.dtype)


def alibi_slopes():
    return jnp.exp2(-8.0 * (jnp.arange(N_HEADS, dtype=jnp.float32) + 1.0) / N_HEADS)


def window_attention(q, k, v, q_pos, k_pos, sinks):
    scores = jnp.einsum('...qkgd,...skd->...kgqs', q, k).astype(jnp.float32) * (HEAD_DIM ** -0.5)
    delta = q_pos[..., :, None] - k_pos[..., None, :]
    mask = (delta >= 0) & (delta <= WINDOW) & (k_pos[..., None, :] >= 0)
    slopes = alibi_slopes().reshape(N_KV_HEADS, GQA_GROUP, 1, 1)
    scores = scores - slopes * delta[..., None, None, :, :].astype(jnp.float32)
    scores = jnp.where(mask[..., None, None, :, :], scores, NEG_INF)
    sink = jnp.broadcast_to(sinks.astype(jnp.float32).reshape(N_KV_HEADS, GQA_GROUP, 1, 1),
                            scores.shape[:-1] + (1,))
    probs = jax.nn.softmax(jnp.concatenate([scores, sink], axis=-1), axis=-1)[..., :-1]
    return jnp.einsum('...kgqs,...skd->...qkgd', probs.astype(v.dtype), v)


def attention_prompt(q, k, v, sinks):
    B, L = q.shape[:2]
    nblk = L // ATTN_BLOCK
    qb = q.reshape(B, nblk, ATTN_BLOCK, N_KV_HEADS, GQA_GROUP, HEAD_DIM)

    def band(t):
        tp = jnp.concatenate([jnp.zeros((B, WINDOW) + t.shape[2:], t.dtype), t], axis=1)
        tp = tp.reshape(B, nblk + 1, ATTN_BLOCK, N_KV_HEADS, HEAD_DIM)
        return jnp.concatenate([tp[:, :-1], tp[:, 1:]], axis=2)

    q_pos = jnp.arange(L).reshape(nblk, ATTN_BLOCK)
    k_pos = (jnp.arange(nblk) * ATTN_BLOCK - WINDOW)[:, None] + jnp.arange(2 * ATTN_BLOCK)[None, :]
    out = window_attention(qb, band(k), band(v), q_pos, k_pos, sinks)
    return out.reshape(B, L, ATTN_WIDTH), k[:, -WINDOW:], v[:, -WINDOW:]


def attention_sample(q, k, v, k_buf, v_buf, sinks, pos0):
    B, L = q.shape[:2]
    qg = q.reshape(B, L, N_KV_HEADS, GQA_GROUP, HEAD_DIM)
    k_all = jnp.concatenate([k_buf.astype(k.dtype), k], axis=1)
    v_all = jnp.concatenate([v_buf.astype(v.dtype), v], axis=1)
    q_pos = pos0 + jnp.arange(L)
    k_pos = pos0 - WINDOW + jnp.arange(WINDOW + L)
    out = window_attention(qg, k_all, v_all, q_pos, k_pos, sinks)
    return out.reshape(B, L, ATTN_WIDTH), k_all[:, -WINDOW:], v_all[:, -WINDOW:]


def recurrent_branch(xr, gr, conv_ctx, h0, lp):
    B, L, _ = xr.shape
    x_ext = jnp.concatenate([conv_ctx.astype(xr.dtype), xr], axis=1)
    xc = lp['conv_b']
    for tap in range(CONV_W):
        xc = xc + x_ext[:, tap:tap + L] * lp['conv_w'][tap]
    xb = xc.reshape(B, L, LRU_BLOCKS, LRU_BLOCK_W)
    r = jax.nn.sigmoid(jnp.einsum('blnc,ncd->blnd', xb, lp['gate_a_w']).reshape(B, L, LRU_WIDTH) + lp['gate_a_b'])
    i = jax.nn.sigmoid(jnp.einsum('blnc,ncd->blnd', xb, lp['gate_x_w']).reshape(B, L, LRU_WIDTH) + lp['gate_x_b'])
    log_a = (-LRU_C * r.astype(jnp.float32)) * jax.nn.softplus(-lp['lru_lambda'].astype(jnp.float32))
    a = jnp.exp(log_a)
    b = jnp.sqrt(-jnp.expm1(2.0 * log_a)) * (i * xc).astype(jnp.float32)

    def step(h, ab):
        h = ab[0] * h + ab[1]
        return h, h

    h_last, hs = lax.scan(step, h0.astype(jnp.float32), (jnp.swapaxes(a, 0, 1), jnp.swapaxes(b, 0, 1)))
    out = jnp.swapaxes(hs, 0, 1).astype(xr.dtype) * jax.nn.gelu(gr)
    return out, h_last.astype(h0.dtype), x_ext[:, -(CONV_W - 1):]


def pool_branch(zp, pool_ctx, pos0, lp):
    B, L, _ = zp.shape
    z_ext = jnp.concatenate([pool_ctx.astype(zp.dtype), zp], axis=1)
    cs = jnp.cumsum(z_ext.astype(jnp.float32), axis=1)
    cs = jnp.concatenate([jnp.zeros((B, 1, POOL_WIDTH), jnp.float32), cs], axis=1)
    pos = (pos0 + jnp.arange(L)).astype(jnp.float32)
    means = []
    for g, w in enumerate(POOL_WINDOWS):
        ch = slice(g * POOL_GROUP_W, (g + 1) * POOL_GROUP_W)
        win_sum = cs[:, POOL_CTX + 1:POOL_CTX + 1 + L, ch] - cs[:, POOL_CTX + 1 - w:POOL_CTX + 1 - w + L, ch]
        count = jnp.minimum(pos + 1.0, float(w))
        means.append(win_sum / count[None, :, None])
    diff = (jnp.concatenate(means, axis=-1) - zp.astype(jnp.float32)).astype(zp.dtype)
    diff = diff.reshape(B, L, POOL_GROUPS, POOL_GROUP_W)
    out = jnp.einsum('blgc,gcd->blgd', diff, lp['pool_w']).reshape(B, L, POOL_WIDTH) * lp['pool_scale']
    return out, z_ext[:, -POOL_CTX:]


def trunk_layer(x, pos0, state, lp, alpha):
    B, L, _ = x.shape
    u = x @ lp['w_in']
    q, k, v, xr, gr, zp = jnp.split(u, SPLIT_POINTS, axis=-1)
    k = k.reshape(B, L, N_KV_HEADS, HEAD_DIM)
    v = v.reshape(B, L, N_KV_HEADS, HEAD_DIM)
    if state is None:
        attn, k_new, v_new = attention_prompt(q, k, v, lp['attn_sinks'])
        h0 = jnp.zeros((B, LRU_WIDTH), x.dtype)
        conv_ctx = jnp.zeros((B, CONV_W - 1, LRU_WIDTH), x.dtype)
        pool_ctx = jnp.zeros((B, POOL_CTX, POOL_WIDTH), x.dtype)
    else:
        k_buf, v_buf, h0, conv_ctx, pool_ctx = state
        attn, k_new, v_new = attention_sample(q, k, v, k_buf, v_buf, lp['attn_sinks'], pos0)
    rec, h_last, conv_new = recurrent_branch(xr, gr, conv_ctx, h0, lp)
    pool, pool_new = pool_branch(zp, pool_ctx, pos0, lp)
    mix = jnp.concatenate([attn, rec, pool], axis=-1) @ lp['w_out']
    x = layer_norm(alpha * x + mix, lp['ln1_g'], lp['ln1_b'])
    hid = jnp.square(jax.nn.relu(x @ lp['w_ff1']))
    x = layer_norm(alpha * x + hid @ lp['w_ff2'], lp['ln2_g'], lp['ln2_b'])
    return x, (k_new, v_new, h_last, conv_new, pool_new)


def setup_inputs(seed: int = 0) -> dict:
    key = jax.random.key(seed)
    ks = iter(jax.random.split(key, 40))
    f32 = jnp.float32

    def nrm(shape, scale):
        return scale * jax.random.normal(next(ks), shape, f32)

    beta = (8.0 * DEPTH) ** -0.25
    a0 = jax.random.uniform(next(ks), (DEPTH, LRU_WIDTH), f32, minval=0.9, maxval=0.999)
    return {
        'x_prompt': nrm((BATCH, SEQ, D_MODEL), 1.0),
        'x_sample': nrm((DEC_BATCH, DEC_SEQ, D_MODEL), 1.0),
        'cache_k': nrm((DEPTH, DEC_BATCH, WINDOW, N_KV_HEADS, HEAD_DIM), 1.0),
        'cache_v': nrm((DEPTH, DEC_BATCH, WINDOW, N_KV_HEADS, HEAD_DIM), 1.0),
        'state_h': nrm((DEPTH, DEC_BATCH, LRU_WIDTH), 0.5),
        'state_conv': nrm((DEPTH, DEC_BATCH, CONV_W - 1, LRU_WIDTH), 1.0),
        'state_pool': nrm((DEPTH, DEC_BATCH, POOL_CTX, POOL_WIDTH), 1.0),
        'w_in': nrm((DEPTH, D_MODEL, IN_WIDTH), D_MODEL ** -0.5),
        'attn_sinks': nrm((DEPTH, N_HEADS), 1.0),
        'conv_w': nrm((DEPTH, CONV_W, LRU_WIDTH), CONV_W ** -0.5),
        'conv_b': nrm((DEPTH, LRU_WIDTH), 0.02),
        'gate_a_w': nrm((DEPTH, LRU_BLOCKS, LRU_BLOCK_W, LRU_BLOCK_W), LRU_BLOCK_W ** -0.5),
        'gate_a_b': nrm((DEPTH, LRU_WIDTH), 0.02),
        'gate_x_w': nrm((DEPTH, LRU_BLOCKS, LRU_BLOCK_W, LRU_BLOCK_W), LRU_BLOCK_W ** -0.5),
        'gate_x_b': nrm((DEPTH, LRU_WIDTH), 0.02),
        'lru_lambda': jnp.log(a0) - jnp.log1p(-a0),
        'pool_w': nrm((DEPTH, POOL_GROUPS, POOL_GROUP_W, POOL_GROUP_W), POOL_GROUP_W ** -0.5),
        'pool_scale': 1.0 + nrm((DEPTH, POOL_WIDTH), 0.1),
        'w_out': nrm((DEPTH, MIX_WIDTH, D_MODEL), beta * MIX_WIDTH ** -0.5),
        'ln1_g': 1.0 + nrm((DEPTH, D_MODEL), 0.05),
        'ln1_b': nrm((DEPTH, D_MODEL), 0.02),
        'w_ff1': nrm((DEPTH, D_MODEL, D_FF), D_MODEL ** -0.5),
        'w_ff2': nrm((DEPTH, D_FF, D_MODEL), beta * D_FF ** -0.5),
        'ln2_g': 1.0 + nrm((DEPTH, D_MODEL), 0.05),
        'ln2_b': nrm((DEPTH, D_MODEL), 0.02),
    }


def reference(x_prompt, x_sample, cache_k, cache_v, state_h, state_conv, state_pool,
              w_in, attn_sinks, conv_w, conv_b, gate_a_w, gate_a_b, gate_x_w, gate_x_b, lru_lambda,
              pool_w, pool_scale, w_out, ln1_g, ln1_b, w_ff1, w_ff2, ln2_g, ln2_b):
    alpha = (2.0 * DEPTH) ** 0.25
    yp, ys = x_prompt, x_sample
    p_k, p_v, p_h, p_conv, p_pool = [], [], [], [], []
    s_k, s_v, s_h, s_conv, s_pool = [], [], [], [], []
    for l in range(DEPTH):
        lp = {
            'w_in': w_in[l], 'attn_sinks': attn_sinks[l], 'conv_w': conv_w[l], 'conv_b': conv_b[l],
            'gate_a_w': gate_a_w[l], 'gate_a_b': gate_a_b[l], 'gate_x_w': gate_x_w[l], 'gate_x_b': gate_x_b[l],
            'lru_lambda': lru_lambda[l], 'pool_w': pool_w[l], 'pool_scale': pool_scale[l], 'w_out': w_out[l],
            'ln1_g': ln1_g[l], 'ln1_b': ln1_b[l], 'w_ff1': w_ff1[l], 'w_ff2': w_ff2[l],
            'ln2_g': ln2_g[l], 'ln2_b': ln2_b[l],
        }
        yp, (k1, v1, h1, c1, q1) = trunk_layer(yp, 0, None, lp, alpha)
        ys, (k2, v2, h2, c2, q2) = trunk_layer(
            ys, PAST_LEN, (cache_k[l], cache_v[l], state_h[l], state_conv[l], state_pool[l]), lp, alpha)
        p_k.append(k1); p_v.append(v1); p_h.append(h1); p_conv.append(c1); p_pool.append(q1)
        s_k.append(k2); s_v.append(v2); s_h.append(h2); s_conv.append(c2); s_pool.append(q2)
    return (yp, ys,
            jnp.stack(p_k), jnp.stack(p_v), jnp.stack(p_h), jnp.stack(p_conv), jnp.stack(p_pool),
            jnp.stack(s_k), jnp.stack(s_v), jnp.stack(s_h), jnp.stack(s_conv), jnp.stack(s_pool))
```

```python
import functools

import jax
import jax.numpy as jnp
import numpy as np
from jax import lax
from jax.experimental import pallas as pl
from jax.experimental.pallas import tpu as pltpu

D_MODEL = 1024
DEPTH = 2
PAST_LEN = 16384
HEAD_DIM = 64
ATTN_WIDTH = 512
N_HEADS = 8
N_KV_HEADS = 2
GQA_GROUP = 4
KV_WIDTH = 128
WINDOW = 128
LRU_WIDTH = 256
LRU_C = 8.0
CONV_W = 4
POOL_WINDOWS = (2, 4, 8, 16)
POOL_WIDTH = 256
POOL_GROUP_W = 64
POOL_CTX = 15
IN_WIDTH = 1536
D_FF = 4096
LN_EPS = 1e-5
NEG_INF = -1e30
ALPHA = (2.0 * DEPTH) ** 0.25

LANES = 128
SUBLANES = 8
VMEM_LIMIT_BYTES = 56 * 1024 * 1024

PROJ_TM = 512
MIX_T = 256
FFN_TM = 512
FFN_FC = 1024
TAIL = 16
SAMPLE_BT = 16

BF16 = jnp.bfloat16
F32 = jnp.float32

HEAD_ORDER = (0, 4, 1, 5, 2, 6, 3, 7)
ATTN_COL_PERM = np.concatenate([np.arange(n * HEAD_DIM, (n + 1) * HEAD_DIM) for n in HEAD_ORDER])


def _cparams(n_grid):
    return pltpu.CompilerParams(
        dimension_semantics=("arbitrary",) * n_grid,
        vmem_limit_bytes=VMEM_LIMIT_BYTES,
    )


def _const_spec(shape):
    nd = len(shape)
    return pl.BlockSpec(shape, lambda *_: (0,) * nd, pipeline_mode=pl.Buffered(1))


def _layer_norm(x, g, b):
    mu = jnp.mean(x, axis=-1, keepdims=True)
    xc = x - mu
    var = jnp.mean(xc * xc, axis=-1, keepdims=True)
    return xc * lax.rsqrt(var + LN_EPS) * g + b


def _gelu_tanh(x):
    return 0.5 * x * (1.0 + jnp.tanh(np.sqrt(2.0 / np.pi) * (x + 0.044715 * (x * x * x))))


def _sigmoid(x):
    return 1.0 / (1.0 + jnp.exp(-x))


def _softplus(x):
    return jnp.maximum(x, 0.0) + jnp.log(1.0 + jnp.exp(-jnp.abs(x)))


def _in_proj_kernel(x_ref, w_ref, u_ref):
    u_ref[...] = jnp.dot(x_ref[...].astype(BF16), w_ref[...], preferred_element_type=F32)


def in_proj(x2d, w_in_bf16, tm):
    n = x2d.shape[0]
    return pl.pallas_call(
        _in_proj_kernel,
        grid=(n // tm,),
        in_specs=[
            pl.BlockSpec((tm, D_MODEL), lambda i: (i, 0)),
            _const_spec((D_MODEL, IN_WIDTH)),
        ],
        out_specs=pl.BlockSpec((tm, IN_WIDTH), lambda i: (i, 0)),
        out_shape=jax.ShapeDtypeStruct((n, IN_WIDTH), F32),
        compiler_params=_cparams(1),
        name="in_proj",
    )(x2d, w_in_bf16)


def _attn_prompt_kernel(q_ref, kp_ref, kc_ref, vp_ref, vc_ref, bias_ref, sink_ref, o_ref):
    lo = lax.broadcasted_iota(jnp.int32, (WINDOW, LANES), 1) < HEAD_DIM
    q = q_ref[...]
    tiles = [q[:, c * LANES:(c + 1) * LANES] for c in range(GQA_GROUP)]
    qs = jnp.concatenate([jnp.where(lo, t, 0.0) for t in tiles]
                         + [jnp.where(lo, 0.0, t) for t in tiles], axis=0).astype(BF16)
    kk = jnp.concatenate([kp_ref[...], kc_ref[...]], axis=0).astype(BF16)
    vv = jnp.concatenate([vp_ref[...], vc_ref[...]], axis=0).astype(BF16)
    s = lax.dot_general(qs, kk, (((1,), (1,)), ((), ())), preferred_element_type=F32)
    s = s + bias_ref[...]
    sink = sink_ref[...]
    m = jnp.maximum(jnp.max(s, axis=1, keepdims=True), sink)
    p = jnp.exp(s - m)
    denom = jnp.sum(p, axis=1, keepdims=True) + jnp.exp(sink - m)
    o = jnp.dot(p.astype(BF16), vv, preferred_element_type=F32) / denom
    half = GQA_GROUP * WINDOW
    cols = [jnp.where(lo, o[c * WINDOW:(c + 1) * WINDOW], o[half + c * WINDOW:half + (c + 1) * WINDOW])
            for c in range(GQA_GROUP)]
    o_ref[...] = jnp.concatenate(cols, axis=1).astype(o_ref.dtype)


def attn_prompt(u3, bias2, sink_col):
    b, l, _ = u3.shape
    nblk = l // WINDOW
    k_blk, v_blk = ATTN_WIDTH // KV_WIDTH, ATTN_WIDTH // KV_WIDTH + 1
    prev = lambda col: (lambda bi, j: (bi, jnp.maximum(j - 1, 0), col))
    cur = lambda col: (lambda bi, j: (bi, j, col))
    return pl.pallas_call(
        _attn_prompt_kernel,
        grid=(b, nblk),
        in_specs=[
            pl.BlockSpec((None, WINDOW, ATTN_WIDTH), lambda bi, j: (bi, j, 0)),
            pl.BlockSpec((None, WINDOW, KV_WIDTH), prev(k_blk)),
            pl.BlockSpec((None, WINDOW, KV_WIDTH), cur(k_blk)),
            pl.BlockSpec((None, WINDOW, KV_WIDTH), prev(v_blk)),
            pl.BlockSpec((None, WINDOW, KV_WIDTH), cur(v_blk)),
            pl.BlockSpec((None, N_HEADS * WINDOW, 2 * WINDOW), lambda bi, j: (jnp.minimum(j, 1), 0, 0)),
            _const_spec((N_HEADS * WINDOW, 1)),
        ],
        out_specs=pl.BlockSpec((None, WINDOW, ATTN_WIDTH), lambda bi, j: (bi, j, 0)),
        out_shape=jax.ShapeDtypeStruct((b, l, ATTN_WIDTH), BF16),
        compiler_params=_cparams(2),
        name="attn_prompt",
    )(u3, u3, u3, u3, u3, bias2, sink_col)


def _gates(xc, wg_ref, bg_ref, lam_ref):
    g = jnp.dot(xc.astype(BF16), wg_ref[...], preferred_element_type=F32) + bg_ref[...]
    r = _sigmoid(g[:, :LRU_WIDTH])
    i = _sigmoid(g[:, LRU_WIDTH:])
    log_a = (-LRU_C * r) * _softplus(-lam_ref[...])
    a = jnp.exp(log_a)
    b = jnp.sqrt(1.0 - jnp.exp(2.0 * log_a)) * (i * xc)
    return a, b


def _pool_lane_consts(rows):
    lane = lax.broadcasted_iota(jnp.int32, (rows, LANES), 1)
    return lane < POOL_GROUP_W


def _mixer_prompt_kernel(xr_ref, gr_ref, zp_ref, cw_ref, cb_ref, wg_ref, bg_ref, lam_ref,
                         wp_ref, ps_ref, o_ref, h_ref, xext, zext, hcar):
    j = pl.program_id(1)
    t = MIX_T

    @pl.when(j == 0)
    def _():
        xext[0:TAIL, :] = jnp.zeros((TAIL, LRU_WIDTH), F32)
        zext[0:TAIL, :] = jnp.zeros((TAIL, POOL_WIDTH), F32)
        hcar[...] = jnp.zeros((1, LRU_WIDTH), F32)

    xext[TAIL:TAIL + t, :] = xr_ref[...]
    zext[TAIL:TAIL + t, :] = zp_ref[...]

    xc = cb_ref[...]
    for tap in range(CONV_W):
        d = CONV_W - 1 - tap
        xc = xc + xext[TAIL - d:TAIL - d + t, :] * cw_ref[tap:tap + 1, :]

    a, b = _gates(xc, wg_ref, bg_ref, lam_ref)

    row = lax.broadcasted_iota(jnp.int32, (t, LRU_WIDTH), 0)
    s = 1
    while s < t:
        keep = row >= s
        a_sh = jnp.where(keep, pltpu.roll(a, s, 0), 1.0)
        b_sh = jnp.where(keep, pltpu.roll(b, s, 0), 0.0)
        b = a * b_sh + b
        a = a * a_sh
        s *= 2
    hs = a * hcar[...] + b
    hcar[...] = hs[t - 1:t, :]
    h_ref[...] = hs[t - 1:t, :]
    rec = hs * _gelu_tanh(gr_ref[...])

    pos1 = (j * t + lax.broadcasted_iota(jnp.int32, (t, LANES), 0) + 1).astype(F32)
    lo = _pool_lane_consts(t)
    means = []
    for c in range(POOL_WIDTH // LANES):
        w_small, w_big = POOL_WINDOWS[2 * c], POOL_WINDOWS[2 * c + 1]
        cols = slice(c * LANES, (c + 1) * LANES)
        acc = zext[TAIL:TAIL + t, cols]
        z = acc
        small = None
        for d in range(1, w_big):
            acc = acc + zext[TAIL - d:TAIL - d + t, cols]
            if d + 1 == w_small:
                small = acc
        win = jnp.where(lo, small, acc)
        count = jnp.minimum(pos1, jnp.where(lo, float(w_small), float(w_big)))
        means.append(win / count - z)
    diff = jnp.concatenate(means, axis=1).astype(BF16)
    pool = jnp.dot(diff, wp_ref[...], preferred_element_type=F32) * ps_ref[...]

    o_ref[...] = jnp.concatenate([rec, pool], axis=1).astype(o_ref.dtype)

    xext[0:TAIL, :] = xext[t:t + TAIL, :]
    zext[0:TAIL, :] = zext[t:t + TAIL, :]


def mixer_prompt(u3, lw):
    b, l, _ = u3.shape
    t = MIX_T
    col = lambda c: (lambda bi, j: (bi, j, c))
    params = [lw["conv_w"], lw["conv_b"], lw["w_gate"], lw["b_gate"], lw["lam"], lw["w_pool"], lw["pool_scale"]]
    return pl.pallas_call(
        _mixer_prompt_kernel,
        grid=(b, l // t),
        in_specs=[
            pl.BlockSpec((None, t, LRU_WIDTH), col(3)),
            pl.BlockSpec((None, t, LRU_WIDTH), col(4)),
            pl.BlockSpec((None, t, POOL_WIDTH), col(5)),
        ] + [_const_spec(p.shape) for p in params],
        out_specs=[
            pl.BlockSpec((None, t, LRU_WIDTH + POOL_WIDTH), lambda bi, j: (bi, j, 0)),
            pl.BlockSpec((None, 1, LRU_WIDTH), lambda bi, j: (bi, 0, 0)),
        ],
        out_shape=[
            jax.ShapeDtypeStruct((b, l, LRU_WIDTH + POOL_WIDTH), BF16),
            jax.ShapeDtypeStruct((b, 1, LRU_WIDTH), F32),
        ],
        scratch_shapes=[
            pltpu.VMEM((TAIL + t, LRU_WIDTH), F32),
            pltpu.VMEM((TAIL + t, POOL_WIDTH), F32),
            pltpu.VMEM((1, LRU_WIDTH), F32),
        ],
        compiler_params=_cparams(2),
        name="mixer_prompt",
    )(u3, u3, u3, *params)


def _mixer_sample_kernel(q_ref, kn_ref, vn_ref, xr_ref, gr_ref, zp_ref, ck_ref, cv_ref,
                         h0_ref, sc_ref, sp_ref, bias_ref, sink_ref,
                         cw_ref, cb_ref, wg_ref, bg_ref, lam_ref, wp_ref, ps_ref,
                         attn_ref, rp_ref, h_ref):
    bt = SAMPLE_BT
    lo3 = lax.broadcasted_iota(jnp.int32, (bt, GQA_GROUP, LANES), 2) < HEAD_DIM
    q4 = q_ref[...]
    qm = jnp.concatenate([jnp.where(lo3, q4, 0.0), jnp.where(lo3, 0.0, q4)], axis=1)
    s = jnp.einsum("bqd,bkd->bqk", qm.astype(BF16), ck_ref[...].astype(BF16),
                   preferred_element_type=F32) + bias_ref[...]
    s_new = jnp.sum(qm * kn_ref[...][:, None, :], axis=2, keepdims=True)
    sink = sink_ref[...]
    m = jnp.maximum(jnp.maximum(jnp.max(s, axis=2, keepdims=True), s_new), sink)
    p = jnp.exp(s - m)
    p_new = jnp.exp(s_new - m)
    denom = jnp.sum(p, axis=2, keepdims=True) + p_new + jnp.exp(sink - m)
    o = jnp.einsum("bqk,bkd->bqd", p.astype(BF16), cv_ref[...].astype(BF16),
                   preferred_element_type=F32)
    o = (o + p_new * vn_ref[...][:, None, :]) / denom
    attn_ref[...] = jnp.where(lo3, o[:, :GQA_GROUP, :], o[:, GQA_GROUP:, :]).astype(attn_ref.dtype)

    xr = xr_ref[...]
    xc = cb_ref[...] + xr * cw_ref[CONV_W - 1:CONV_W, :]
    for tap in range(CONV_W - 1):
        xc = xc + sc_ref[tap] * cw_ref[tap:tap + 1, :]
    a, b = _gates(xc, wg_ref, bg_ref, lam_ref)
    h = a * h0_ref[...] + b
    h_ref[...] = h
    rec = h * _gelu_tanh(gr_ref[...])

    z = zp_ref[...]
    lo = _pool_lane_consts(bt)
    means = []
    for c in range(POOL_WIDTH // LANES):
        w_small, w_big = POOL_WINDOWS[2 * c], POOL_WINDOWS[2 * c + 1]
        cols = slice(c * LANES, (c + 1) * LANES)
        zc = z[:, cols]
        acc = zc
        small = None
        for d in range(1, w_big):
            acc = acc + sp_ref[POOL_CTX - d][:, cols]
            if d + 1 == w_small:
                small = acc
        win = jnp.where(lo, small, acc)
        count = jnp.where(lo, float(w_small), float(w_big))
        means.append(win / count - zc)
    diff = jnp.concatenate(means, axis=1).astype(BF16)
    pool = jnp.dot(diff, wp_ref[...], preferred_element_type=F32) * ps_ref[...]
    rp_ref[...] = jnp.concatenate([rec, pool], axis=1).astype(rp_ref.dtype)


def mixer_sample(us, q4, ck, cv, h0, sc_t, sp_t, bias_s, sink8, lw):
    n = us.shape[0]
    bt = SAMPLE_BT
    params = [lw["conv_w"], lw["conv_b"], lw["w_gate"], lw["b_gate"], lw["lam"], lw["w_pool"], lw["pool_scale"]]
    ucol = lambda width, c: pl.BlockSpec((bt, width), lambda i: (i, c))
    return pl.pallas_call(
        _mixer_sample_kernel,
        grid=(n // bt,),
        in_specs=[
            pl.BlockSpec((bt, GQA_GROUP, LANES), lambda i: (i, 0, 0)),
            ucol(KV_WIDTH, 4), ucol(KV_WIDTH, 5),
            ucol(LRU_WIDTH, 3), ucol(LRU_WIDTH, 4), ucol(POOL_WIDTH, 5),
            pl.BlockSpec((bt, WINDOW, KV_WIDTH), lambda i: (i, 0, 0)),
            pl.BlockSpec((bt, WINDOW, KV_WIDTH), lambda i: (i, 0, 0)),
            pl.BlockSpec((bt, LRU_WIDTH), lambda i: (i, 0)),
            pl.BlockSpec((CONV_W - 1, bt, LRU_WIDTH), lambda i: (0, i, 0)),
            pl.BlockSpec((POOL_CTX, bt, POOL_WIDTH), lambda i: (0, i, 0)),
            _const_spec((N_HEADS, WINDOW)),
            _const_spec((N_HEADS, 1)),
        ] + [_const_spec(p.shape) for p in params],
        out_specs=[
            pl.BlockSpec((bt, GQA_GROUP, LANES), lambda i: (i, 0, 0)),
            pl.BlockSpec((bt, LRU_WIDTH + POOL_WIDTH), lambda i: (i, 0)),
            pl.BlockSpec((bt, LRU_WIDTH), lambda i: (i, 0)),
        ],
        out_shape=[
            jax.ShapeDtypeStruct((n, GQA_GROUP, LANES), BF16),
            jax.ShapeDtypeStruct((n, LRU_WIDTH + POOL_WIDTH), BF16),
            jax.ShapeDtypeStruct((n, LRU_WIDTH), F32),
        ],
        compiler_params=_cparams(1),
        name="mixer_sample",
    )(q4, us, us, us, us, us, ck, cv, h0, sc_t, sp_t, bias_s, sink8, *params)


def _out_ffn_kernel(x_ref, at_ref, rp_ref, woa_ref, wor_ref, g1_ref, b1_ref,
                    w1_ref, w2_ref, g2_ref, b2_ref, y_ref, acc_ref):
    mix = jnp.dot(at_ref[...], woa_ref[...], preferred_element_type=F32)
    mix = mix + jnp.dot(rp_ref[...], wor_ref[...], preferred_element_type=F32)
    x1 = _layer_norm(ALPHA * x_ref[...] + mix, g1_ref[...], b1_ref[...])
    x1b = x1.astype(BF16)
    for c in range(D_FF // FFN_FC):
        cols = slice(c * FFN_FC, (c + 1) * FFN_FC)
        hid = jnp.dot(x1b, w1_ref[:, cols], preferred_element_type=F32)
        hid = jnp.square(jnp.maximum(hid, 0.0)).astype(BF16)
        part = jnp.dot(hid, w2_ref[cols, :], preferred_element_type=F32)
        if c == 0:
            acc_ref[...] = part
        else:
            acc_ref[...] += part
    y_ref[...] = _layer_norm(ALPHA * x1 + acc_ref[...], g2_ref[...], b2_ref[...])


def out_ffn(x2d, attn2d, rp2d, lw, tm):
    n = x2d.shape[0]
    half = ATTN_WIDTH
    row = lambda width: pl.BlockSpec((tm, width), lambda i: (i, 0))
    vec = _const_spec((1, D_MODEL))
    return pl.pallas_call(
        _out_ffn_kernel,
        grid=(n // tm,),
        in_specs=[
            row(D_MODEL), row(half), row(half),
            _const_spec((half, D_MODEL)), _const_spec((half, D_MODEL)), vec, vec,
            _const_spec((D_MODEL, D_FF)), _const_spec((D_FF, D_MODEL)), vec, vec,
        ],
        out_specs=row(D_MODEL),
        out_shape=jax.ShapeDtypeStruct((n, D_MODEL), F32),
        scratch_shapes=[pltpu.VMEM((tm, D_MODEL), F32)],
        compiler_params=_cparams(1),
        name="out_ffn",
    )(x2d, attn2d, rp2d, lw["w_out_attn"], lw["w_out_rp"], lw["ln1_g"], lw["ln1_b"],
      lw["w_ff1"], lw["w_ff2"], lw["ln2_g"], lw["ln2_b"])


def _block_diag(w):
    g, c, d = w.shape
    eye = jnp.eye(g, dtype=bool)[:, None, :, None]
    return jnp.where(eye, w[:, :, None, :], 0.0).reshape(g * c, g * d)


def _alibi_slopes():
    return jnp.exp2(-8.0 * (jnp.arange(N_HEADS, dtype=F32) + 1.0) / N_HEADS)


def _prompt_bias_tables():
    order = np.array([h * GQA_GROUP + c for h in range(N_KV_HEADS) for c in range(GQA_GROUP)])
    slopes = _alibi_slopes()[order]
    tq = jnp.arange(WINDOW)[:, None]
    jk = jnp.arange(2 * WINDOW)[None, :]
    delta = tq + WINDOW - jk
    visible = (delta >= 0) & (delta <= WINDOW)
    bias = -slopes[:, None, None] * delta.astype(F32)[None]
    full = jnp.where(visible[None], bias, NEG_INF)
    first = jnp.where((visible & (jk >= WINDOW))[None], bias, NEG_INF)
    return jnp.stack([first, full]).reshape(2, N_HEADS * WINDOW, 2 * WINDOW)


def _layer_weights(l, w_in, conv_w, conv_b, gate_a_w, gate_a_b, gate_x_w, gate_x_b, lru_lambda,
                   pool_w, pool_scale, w_out, ln1_g, ln1_b, w_ff1, w_ff2, ln2_g, ln2_b):
    wq = w_in[l][:, :ATTN_WIDTH][:, ATTN_COL_PERM] * (HEAD_DIM ** -0.5)
    w_in_l = jnp.concatenate([wq, w_in[l][:, ATTN_WIDTH:]], axis=1).astype(BF16)
    row = lambda v: v[l].reshape(1, -1).astype(F32)
    return {
        "w_in": w_in_l,
        "conv_w": conv_w[l], "conv_b": row(conv_b),
        "w_gate": jnp.concatenate([_block_diag(gate_a_w[l]), _block_diag(gate_x_w[l])], axis=1).astype(BF16),
        "b_gate": jnp.concatenate([gate_a_b[l], gate_x_b[l]]).reshape(1, -1),
        "lam": row(lru_lambda),
        "w_pool": _block_diag(pool_w[l]).astype(BF16), "pool_scale": row(pool_scale),
        "w_out_attn": w_out[l][:ATTN_WIDTH][ATTN_COL_PERM].astype(BF16),
        "w_out_rp": w_out[l][ATTN_WIDTH:].astype(BF16),
        "ln1_g": row(ln1_g), "ln1_b": row(ln1_b),
        "w_ff1": w_ff1[l].astype(BF16), "w_ff2": w_ff2[l].astype(BF16),
        "ln2_g": row(ln2_g), "ln2_b": row(ln2_b),
    }


def kernel(x_prompt, x_sample, cache_k, cache_v, state_h, state_conv, state_pool, w_in, attn_sinks, conv_w, conv_b, gate_a_w, gate_a_b, gate_x_w, gate_x_b, lru_lambda, pool_w, pool_scale, w_out, ln1_g, ln1_b, w_ff1, w_ff2, ln2_g, ln2_b):
    batch, seq, _ = x_prompt.shape
    dec = x_sample.shape[0]
    stacked_heads = np.array([h * GQA_GROUP + c for h in range(N_KV_HEADS) for c in range(GQA_GROUP)])
    bias_prompt = _prompt_bias_tables()
    slopes_st = _alibi_slopes()[stacked_heads]
    bias_sample = -slopes_st[:, None] * (WINDOW - jnp.arange(WINDOW, dtype=F32))[None, :]

    yp = x_prompt.reshape(batch * seq, D_MODEL)
    ys = x_sample.reshape(dec, D_MODEL)
    outs = {k: [] for k in ("pk", "pv", "ph", "pc", "pp", "sk", "sv", "sh", "sc", "sp")}
    for l in range(DEPTH):
        lw = _layer_weights(l, w_in, conv_w, conv_b, gate_a_w, gate_a_b, gate_x_w, gate_x_b, lru_lambda,
                            pool_w, pool_scale, w_out, ln1_g, ln1_b, w_ff1, w_ff2, ln2_g, ln2_b)
        sinks_st = attn_sinks[l][stacked_heads].astype(F32)
        sink_col = jnp.repeat(sinks_st, WINDOW).reshape(N_HEADS * WINDOW, 1)

        u3 = in_proj(yp, lw["w_in"], PROJ_TM).reshape(batch, seq, IN_WIDTH)
        attn = attn_prompt(u3, bias_prompt, sink_col)
        rp, h_last = mixer_prompt(u3, lw)
        yp = out_ffn(yp, attn.reshape(batch * seq, ATTN_WIDTH), rp.reshape(batch * seq, -1), lw, FFN_TM)
        outs["pk"].append(u3[:, seq - WINDOW:, ATTN_WIDTH:ATTN_WIDTH + KV_WIDTH].reshape(batch, WINDOW, N_KV_HEADS, HEAD_DIM))
        outs["pv"].append(u3[:, seq - WINDOW:, ATTN_WIDTH + KV_WIDTH:ATTN_WIDTH + 2 * KV_WIDTH].reshape(batch, WINDOW, N_KV_HEADS, HEAD_DIM))
        outs["ph"].append(h_last.reshape(batch, LRU_WIDTH))
        outs["pc"].append(u3[:, seq - (CONV_W - 1):, 768:768 + LRU_WIDTH])
        outs["pp"].append(u3[:, seq - POOL_CTX:, 1280:1280 + POOL_WIDTH])

        us = in_proj(ys, lw["w_in"], dec)
        ck = cache_k[l].reshape(dec, WINDOW, KV_WIDTH)
        cv = cache_v[l].reshape(dec, WINDOW, KV_WIDTH)
        attn_s, rp_s, h_s = mixer_sample(
            us, us[:, :ATTN_WIDTH].reshape(dec, GQA_GROUP, LANES), ck, cv, state_h[l],
            jnp.swapaxes(state_conv[l], 0, 1), jnp.swapaxes(state_pool[l], 0, 1),
            bias_sample, sinks_st.reshape(N_HEADS, 1), lw)
        ys = out_ffn(ys, attn_s.reshape(dec, ATTN_WIDTH), rp_s, lw, dec)
        k_new = us[:, ATTN_WIDTH:ATTN_WIDTH + KV_WIDTH]
        v_new = us[:, ATTN_WIDTH + KV_WIDTH:ATTN_WIDTH + 2 * KV_WIDTH]
        outs["sk"].append(jnp.concatenate([ck[:, 1:], k_new[:, None]], axis=1).reshape(dec, WINDOW, N_KV_HEADS, HEAD_DIM))
        outs["sv"].append(jnp.concatenate([cv[:, 1:], v_new[:, None]], axis=1).reshape(dec, WINDOW, N_KV_HEADS, HEAD_DIM))
        outs["sh"].append(h_s)
        outs["sc"].append(jnp.concatenate([state_conv[l][:, 1:], us[:, None, 768:768 + LRU_WIDTH]], axis=1))
        outs["sp"].append(jnp.concatenate([state_pool[l][:, 1:], us[:, None, 1280:1280 + POOL_WIDTH]], axis=1))

    st = {k: jnp.stack(v) for k, v in outs.items()}
    return (yp.reshape(batch, seq, D_MODEL), ys.reshape(dec, 1, D_MODEL),
            st["pk"], st["pv"], st["ph"], st["pc"], st["pp"],
            st["sk"], st["sv"], st["sh"], st["sc"], st["sp"])
```

```python
import functools

import jax
import jax.numpy as jnp
import numpy as np
from jax import lax
from jax.experimental import pallas as pl
from jax.experimental.pallas import tpu as pltpu

D_MODEL = 1024
DEPTH = 2
PAST_LEN = 16384
HEAD_DIM = 64
ATTN_WIDTH = 512
N_HEADS = 8
N_KV_HEADS = 2
GQA_GROUP = 4
KV_WIDTH = 128
WINDOW = 128
LRU_WIDTH = 256
LRU_C = 8.0
CONV_W = 4
POOL_WINDOWS = (2, 4, 8, 16)
POOL_WIDTH = 256
POOL_GROUP_W = 64
POOL_CTX = 15
IN_WIDTH = 1536
D_FF = 4096
LN_EPS = 1e-5
NEG_INF = -1e30
ALPHA = (2.0 * DEPTH) ** 0.25

LANES = 128
SUBLANES = 8
VMEM_LIMIT_BYTES = 56 * 1024 * 1024

PROJ_TM = 512
MIX_T = 256
ATTN_QB = 4
FFN_TM = 512
FFN_FC = 1024
TAIL = 16
SAMPLE_BT = 16
CACHE_BT = 32

BF16 = jnp.bfloat16
F32 = jnp.float32

HEAD_ORDER = (0, 4, 1, 5, 2, 6, 3, 7)
ATTN_COL_PERM = np.concatenate([np.arange(n * HEAD_DIM, (n + 1) * HEAD_DIM) for n in HEAD_ORDER])


def _cparams(n_grid):
    return pltpu.CompilerParams(
        dimension_semantics=("arbitrary",) * n_grid,
        vmem_limit_bytes=VMEM_LIMIT_BYTES,
    )


def _const_spec(shape):
    nd = len(shape)
    return pl.BlockSpec(shape, lambda *_: (0,) * nd, pipeline_mode=pl.Buffered(1))


def _layer_norm(x, g, b):
    mu = jnp.mean(x, axis=-1, keepdims=True)
    xc = x - mu
    var = jnp.mean(xc * xc, axis=-1, keepdims=True)
    return xc * lax.rsqrt(var + LN_EPS) * g + b


def _gelu_tanh(x):
    return 0.5 * x * (1.0 + jnp.tanh(np.sqrt(2.0 / np.pi) * (x + 0.044715 * (x * x * x))))


def _sigmoid(x):
    return 1.0 / (1.0 + jnp.exp(-x))


def _softplus(x):
    return jnp.maximum(x, 0.0) + jnp.log(1.0 + jnp.exp(-jnp.abs(x)))


def _in_proj_kernel(x_ref, w_ref, u_ref):
    u_ref[...] = jnp.dot(x_ref[...].astype(BF16), w_ref[...], preferred_element_type=F32)


def in_proj(x2d, w_in_bf16, tm):
    n = x2d.shape[0]
    return pl.pallas_call(
        _in_proj_kernel,
        grid=(n // tm,),
        in_specs=[
            pl.BlockSpec((tm, D_MODEL), lambda i: (i, 0)),
            _const_spec((D_MODEL, IN_WIDTH)),
        ],
        out_specs=pl.BlockSpec((tm, IN_WIDTH), lambda i: (i, 0)),
        out_shape=jax.ShapeDtypeStruct((n, IN_WIDTH), F32),
        compiler_params=_cparams(1),
        name="in_proj",
    )(x2d, w_in_bf16)


def _attn_block(q, k2, v2, bias_t, sink, lo):
    tiles = [q[:, c * LANES:(c + 1) * LANES] for c in range(GQA_GROUP)]
    qs = jnp.concatenate([jnp.where(lo, t, 0.0) for t in tiles]
                         + [jnp.where(lo, 0.0, t) for t in tiles], axis=0).astype(BF16)
    s = lax.dot_general(k2, qs, (((1,), (1,)), ((), ())), preferred_element_type=F32) + bias_t
    m = jnp.maximum(jnp.max(s, axis=0, keepdims=True), sink)
    p = jnp.exp(s - m)
    denom = jnp.sum(p, axis=0, keepdims=True) + jnp.exp(sink - m)
    o = lax.dot_general(v2, p.astype(BF16), (((0,), (0,)), ((), ())), preferred_element_type=F32)
    o = o * (1.0 / denom)
    half = GQA_GROUP * WINDOW
    cols = []
    for c in range(GQA_GROUP):
        blk = jnp.concatenate([o[:HEAD_DIM, c * WINDOW:(c + 1) * WINDOW],
                               o[HEAD_DIM:, half + c * WINDOW:half + (c + 1) * WINDOW]], axis=0)
        cols.append(blk.T)
    return jnp.concatenate(cols, axis=1)


def _attn_prompt_kernel(q_ref, kp_ref, kc_ref, vp_ref, vc_ref, bias0_ref, bias_ref, sink_ref,
                        o_ref, kt_ref, vt_ref):
    @pl.when(pl.program_id(1) == pl.num_programs(1) - 1)
    def _():
        tail = slice((ATTN_QB - 1) * WINDOW, ATTN_QB * WINDOW)
        kt_ref[...] = kc_ref[tail, :].T
        vt_ref[...] = vc_ref[tail, :].T

    lo = lax.broadcasted_iota(jnp.int32, (WINDOW, LANES), 1) < HEAD_DIM
    sink = sink_ref[...]
    kc = kc_ref[...].astype(BF16)
    vc = vc_ref[...].astype(BF16)
    k_prev = kp_ref[...].astype(BF16)
    v_prev = vp_ref[...].astype(BF16)
    for qb in range(ATTN_QB):
        rows = slice(qb * WINDOW, (qb + 1) * WINDOW)
        k2 = jnp.concatenate([k_prev, kc[rows]], axis=0)
        v2 = jnp.concatenate([v_prev, vc[rows]], axis=0)
        bias_t = bias0_ref[...] if qb == 0 else bias_ref[...]
        o_ref[rows, :] = _attn_block(q_ref[rows, :], k2, v2, bias_t, sink, lo).astype(o_ref.dtype)
        k_prev, v_prev = kc[rows], vc[rows]


def attn_prompt(u3, bias2, sink_row):
    b, l, _ = u3.shape
    tq = ATTN_QB * WINDOW
    k_blk, v_blk = ATTN_WIDTH // KV_WIDTH, ATTN_WIDTH // KV_WIDTH + 1
    prev = lambda col: (lambda bi, j: (bi, jnp.maximum(j * ATTN_QB - 1, 0), col))
    cur = lambda col: (lambda bi, j: (bi, j, col))
    bias_shape = (None, 2 * WINDOW, N_HEADS * WINDOW)
    return pl.pallas_call(
        _attn_prompt_kernel,
        grid=(b, l // tq),
        in_specs=[
            pl.BlockSpec((None, tq, ATTN_WIDTH), lambda bi, j: (bi, j, 0)),
            pl.BlockSpec((None, WINDOW, KV_WIDTH), prev(k_blk)),
            pl.BlockSpec((None, tq, KV_WIDTH), cur(k_blk)),
            pl.BlockSpec((None, WINDOW, KV_WIDTH), prev(v_blk)),
            pl.BlockSpec((None, tq, KV_WIDTH), cur(v_blk)),
            pl.BlockSpec(bias_shape, lambda bi, j: (jnp.minimum(j, 1), 0, 0)),
            pl.BlockSpec(bias_shape, lambda bi, j: (1, 0, 0)),
            _const_spec((1, N_HEADS * WINDOW)),
        ],
        out_specs=[
            pl.BlockSpec((None, tq, ATTN_WIDTH), lambda bi, j: (bi, j, 0)),
            pl.BlockSpec((None, KV_WIDTH, WINDOW), lambda bi, j: (bi, 0, 0)),
            pl.BlockSpec((None, KV_WIDTH, WINDOW), lambda bi, j: (bi, 0, 0)),
        ],
        out_shape=[
            jax.ShapeDtypeStruct((b, l, ATTN_WIDTH), BF16),
            jax.ShapeDtypeStruct((b, KV_WIDTH, WINDOW), F32),
            jax.ShapeDtypeStruct((b, KV_WIDTH, WINDOW), F32),
        ],
        compiler_params=_cparams(2),
        name="attn_prompt",
    )(u3, u3, u3, u3, u3, bias2, bias2, sink_row)


def _gates(xc, wg_ref, bg_ref, lam_ref):
    g = jnp.dot(xc.astype(BF16), wg_ref[...], preferred_element_type=F32) + bg_ref[...]
    r = _sigmoid(g[:, :LRU_WIDTH])
    i = _sigmoid(g[:, LRU_WIDTH:])
    log_a = (-LRU_C * r) * _softplus(-lam_ref[...])
    a = jnp.exp(log_a)
    b = jnp.sqrt(1.0 - jnp.exp(2.0 * log_a)) * (i * xc)
    return a, b


def _pool_lane_consts(rows):
    lane = lax.broadcasted_iota(jnp.int32, (rows, LANES), 1)
    return lane < POOL_GROUP_W


def _mixer_prompt_kernel(xr_ref, gr_ref, zp_ref, cw_ref, cb_ref, wg_ref, bg_ref, lam_ref,
                         wp_ref, ps_ref, o_ref, h_ref, xext, zext, hcar):
    j = pl.program_id(1)
    t = MIX_T

    @pl.when(j == 0)
    def _():
        xext[0:TAIL, :] = jnp.zeros((TAIL, LRU_WIDTH), F32)
        zext[0:TAIL, :] = jnp.zeros((TAIL, POOL_WIDTH), F32)
        hcar[...] = jnp.zeros((1, LRU_WIDTH), F32)

    xext[TAIL:TAIL + t, :] = xr_ref[...]
    zext[TAIL:TAIL + t, :] = zp_ref[...]

    xc = cb_ref[...]
    for tap in range(CONV_W):
        d = CONV_W - 1 - tap
        xc = xc + xext[TAIL - d:TAIL - d + t, :] * cw_ref[tap:tap + 1, :]

    a, b = _gates(xc, wg_ref, bg_ref, lam_ref)

    row = lax.broadcasted_iota(jnp.int32, (t, LRU_WIDTH), 0)
    s = 1
    while s < t:
        keep = row >= s
        a_sh = jnp.where(keep, pltpu.roll(a, s, 0), 1.0)
        b_sh = jnp.where(keep, pltpu.roll(b, s, 0), 0.0)
        b = a * b_sh + b
        a = a * a_sh
        s *= 2
    hs = a * hcar[...] + b
    hcar[...] = hs[t - 1:t, :]
    h_ref[...] = hs[t - 1:t, :]
    rec = hs * _gelu_tanh(gr_ref[...])

    pos1 = (j * t + lax.broadcasted_iota(jnp.int32, (t, LANES), 0) + 1).astype(F32)
    lo = _pool_lane_consts(t)
    means = []
    for c in range(POOL_WIDTH // LANES):
        w_small, w_big = POOL_WINDOWS[2 * c], POOL_WINDOWS[2 * c + 1]
        cols = slice(c * LANES, (c + 1) * LANES)
        acc = zext[TAIL:TAIL + t, cols]
        z = acc
        small = None
        for d in range(1, w_big):
            acc = acc + zext[TAIL - d:TAIL - d + t, cols]
            if d + 1 == w_small:
                small = acc
        win = jnp.where(lo, small, acc)
        count = jnp.minimum(pos1, jnp.where(lo, float(w_small), float(w_big)))
        means.append(win / count - z)
    diff = jnp.concatenate(means, axis=1).astype(BF16)
    pool = jnp.dot(diff, wp_ref[...], preferred_element_type=F32) * ps_ref[...]

    o_ref[...] = jnp.concatenate([rec, pool], axis=1).astype(o_ref.dtype)

    xext[0:TAIL, :] = xext[t:t + TAIL, :]
    zext[0:TAIL, :] = zext[t:t + TAIL, :]


def mixer_prompt(u3, lw):
    b, l, _ = u3.shape
    t = MIX_T
    col = lambda c: (lambda bi, j: (bi, j, c))
    params = [lw["conv_w"], lw["conv_b"], lw["w_gate"], lw["b_gate"], lw["lam"], lw["w_pool"], lw["pool_scale"]]
    return pl.pallas_call(
        _mixer_prompt_kernel,
        grid=(b, l // t),
        in_specs=[
            pl.BlockSpec((None, t, LRU_WIDTH), col(3)),
            pl.BlockSpec((None, t, LRU_WIDTH), col(4)),
            pl.BlockSpec((None, t, POOL_WIDTH), col(5)),
        ] + [_const_spec(p.shape) for p in params],
        out_specs=[
            pl.BlockSpec((None, t, LRU_WIDTH + POOL_WIDTH), lambda bi, j: (bi, j, 0)),
            pl.BlockSpec((None, 1, LRU_WIDTH), lambda bi, j: (bi, 0, 0)),
        ],
        out_shape=[
            jax.ShapeDtypeStruct((b, l, LRU_WIDTH + POOL_WIDTH), BF16),
            jax.ShapeDtypeStruct((b, 1, LRU_WIDTH), F32),
        ],
        scratch_shapes=[
            pltpu.VMEM((TAIL + t, LRU_WIDTH), F32),
            pltpu.VMEM((TAIL + t, POOL_WIDTH), F32),
            pltpu.VMEM((1, LRU_WIDTH), F32),
        ],
        compiler_params=_cparams(2),
        name="mixer_prompt",
    )(u3, u3, u3, *params)


def _mixer_sample_kernel(q_ref, kn_ref, vn_ref, xr_ref, gr_ref, zp_ref, ck_ref, cv_ref,
                         h0_ref, sc_ref, sp_ref, bias_ref, sink_ref,
                         cw_ref, cb_ref, wg_ref, bg_ref, lam_ref, wp_ref, ps_ref,
                         attn_ref, rp_ref, h_ref):
    bt = SAMPLE_BT
    lo3 = lax.broadcasted_iota(jnp.int32, (bt, GQA_GROUP, LANES), 2) < HEAD_DIM
    q4 = q_ref[...]
    qm = jnp.concatenate([jnp.where(lo3, q4, 0.0), jnp.where(lo3, 0.0, q4)], axis=1)
    s = jnp.einsum("bqc,bck->bqk", qm.astype(BF16), ck_ref[...].astype(BF16),
                   preferred_element_type=F32) + bias_ref[...]
    s_new = jnp.sum(qm * kn_ref[...][:, None, :], axis=2, keepdims=True)
    sink = sink_ref[...]
    m = jnp.maximum(jnp.maximum(jnp.max(s, axis=2, keepdims=True), s_new), sink)
    p = jnp.exp(s - m)
    p_new = jnp.exp(s_new - m)
    denom = jnp.sum(p, axis=2, keepdims=True) + p_new + jnp.exp(sink - m)
    o = jnp.einsum("bqk,bck->bqc", p.astype(BF16), cv_ref[...].astype(BF16),
                   preferred_element_type=F32)
    o = (o + p_new * vn_ref[...][:, None, :]) / denom
    attn_ref[...] = jnp.where(lo3, o[:, :GQA_GROUP, :], o[:, GQA_GROUP:, :]).astype(attn_ref.dtype)

    xr = xr_ref[...]
    xc = cb_ref[...] + xr * cw_ref[CONV_W - 1:CONV_W, :]
    for tap in range(CONV_W - 1):
        xc = xc + sc_ref[tap] * cw_ref[tap:tap + 1, :]
    a, b = _gates(xc, wg_ref, bg_ref, lam_ref)
    h = a * h0_ref[...] + b
    h_ref[...] = h
    rec = h * _gelu_tanh(gr_ref[...])

    z = zp_ref[...]
    lo = _pool_lane_consts(bt)
    means = []
    for c in range(POOL_WIDTH // LANES):
        w_small, w_big = POOL_WINDOWS[2 * c], POOL_WINDOWS[2 * c + 1]
        cols = slice(c * LANES, (c + 1) * LANES)
        zc = z[:, cols]
        acc = zc
        small = None
        for d in range(1, w_big):
            acc = acc + sp_ref[POOL_CTX - d][:, cols]
            if d + 1 == w_small:
                small = acc
        win = jnp.where(lo, small, acc)
        count = jnp.where(lo, float(w_small), float(w_big))
        means.append(win / count - zc)
    diff = jnp.concatenate(means, axis=1).astype(BF16)
    pool = jnp.dot(diff, wp_ref[...], preferred_element_type=F32) * ps_ref[...]
    rp_ref[...] = jnp.concatenate([rec, pool], axis=1).astype(rp_ref.dtype)


def mixer_sample(l, us, q4, ck_t, cv_t, h0, sc_t, sp_t, bias_s, sink8, lw):
    n = us.shape[0]
    bt = SAMPLE_BT
    cache_spec = pl.BlockSpec((None, bt, KV_WIDTH, WINDOW), lambda i: (l, i, 0, 0))
    params = [lw["conv_w"], lw["conv_b"], lw["w_gate"], lw["b_gate"], lw["lam"], lw["w_pool"], lw["pool_scale"]]
    ucol = lambda width, c: pl.BlockSpec((bt, width), lambda i: (i, c))
    return pl.pallas_call(
        _mixer_sample_kernel,
        grid=(n // bt,),
        in_specs=[
            pl.BlockSpec((bt, GQA_GROUP, LANES), lambda i: (i, 0, 0)),
            ucol(KV_WIDTH, 4), ucol(KV_WIDTH, 5),
            ucol(LRU_WIDTH, 3), ucol(LRU_WIDTH, 4), ucol(POOL_WIDTH, 5),
            cache_spec, cache_spec,
            pl.BlockSpec((bt, LRU_WIDTH), lambda i: (i, 0)),
            pl.BlockSpec((CONV_W - 1, bt, LRU_WIDTH), lambda i: (0, i, 0)),
            pl.BlockSpec((POOL_CTX, bt, POOL_WIDTH), lambda i: (0, i, 0)),
            _const_spec((N_HEADS, WINDOW)),
            _const_spec((N_HEADS, 1)),
        ] + [_const_spec(p.shape) for p in params],
        out_specs=[
            pl.BlockSpec((bt, GQA_GROUP, LANES), lambda i: (i, 0, 0)),
            pl.BlockSpec((bt, LRU_WIDTH + POOL_WIDTH), lambda i: (i, 0)),
            pl.BlockSpec((bt, LRU_WIDTH), lambda i: (i, 0)),
        ],
        out_shape=[
            jax.ShapeDtypeStruct((n, GQA_GROUP, LANES), BF16),
            jax.ShapeDtypeStruct((n, LRU_WIDTH + POOL_WIDTH), BF16),
            jax.ShapeDtypeStruct((n, LRU_WIDTH), F32),
        ],
        compiler_params=_cparams(1),
        name="mixer_sample",
    )(q4, us, us, us, us, us, ck_t, cv_t, h0, sc_t, sp_t, bias_s, sink8, *params)


def _cache_update_kernel(ck_ref, cv_ref, kn_ref, vn_ref, ok_ref, ov_ref):
    bt = CACHE_BT
    last = lax.broadcasted_iota(jnp.int32, (KV_WIDTH, WINDOW), 1) == WINDOW - 1
    pad = jnp.zeros((LANES - bt, KV_WIDTH), F32)
    for src, new, dst in ((ck_ref, kn_ref, ok_ref), (cv_ref, vn_ref, ov_ref)):
        new_t = jnp.concatenate([new[...], pad], axis=0).T
        for s in range(bt):
            shifted = pltpu.roll(src[s], WINDOW - 1, 1)
            col = pltpu.roll(new_t, WINDOW - 1 - s, 1)
            dst[s] = jnp.where(last, col, shifted)


def cache_update(ck_t, cv_t, k_new, v_new):
    depth, n = k_new.shape[:2]
    bt = CACHE_BT
    cache_spec = pl.BlockSpec((None, bt, KV_WIDTH, WINDOW), lambda l, i: (l, i, 0, 0))
    new_spec = pl.BlockSpec((None, bt, KV_WIDTH), lambda l, i: (l, i, 0))
    shape = jax.ShapeDtypeStruct(ck_t.shape, F32)
    return pl.pallas_call(
        _cache_update_kernel,
        grid=(depth, n // bt),
        in_specs=[cache_spec, cache_spec, new_spec, new_spec],
        out_specs=[cache_spec, cache_spec],
        out_shape=[shape, shape],
        compiler_params=_cparams(2),
        name="cache_update",
    )(ck_t, cv_t, k_new, v_new)


def _out_ffn_kernel(x_ref, at_ref, rp_ref, woa_ref, wor_ref, g1_ref, b1_ref,
                    w1_ref, w2_ref, g2_ref, b2_ref, y_ref, acc_ref):
    mix = jnp.dot(at_ref[...], woa_ref[...], preferred_element_type=F32)
    mix = mix + jnp.dot(rp_ref[...], wor_ref[...], preferred_element_type=F32)
    x1 = _layer_norm(ALPHA * x_ref[...] + mix, g1_ref[...], b1_ref[...])
    x1b = x1.astype(BF16)
    for c in range(D_FF // FFN_FC):
        cols = slice(c * FFN_FC, (c + 1) * FFN_FC)
        hid = jnp.dot(x1b, w1_ref[:, cols], preferred_element_type=F32)
        hid = jnp.square(jnp.maximum(hid, 0.0)).astype(BF16)
        part = jnp.dot(hid, w2_ref[cols, :], preferred_element_type=F32)
        if c == 0:
            acc_ref[...] = part
        else:
            acc_ref[...] += part
    y_ref[...] = _layer_norm(ALPHA * x1 + acc_ref[...], g2_ref[...], b2_ref[...])


def out_ffn(x2d, attn2d, rp2d, lw, tm):
    n = x2d.shape[0]
    half = ATTN_WIDTH
    row = lambda width: pl.BlockSpec((tm, width), lambda i: (i, 0))
    vec = _const_spec((1, D_MODEL))
    return pl.pallas_call(
        _out_ffn_kernel,
        grid=(n // tm,),
        in_specs=[
            row(D_MODEL), row(half), row(half),
            _const_spec((half, D_MODEL)), _const_spec((half, D_MODEL)), vec, vec,
            _const_spec((D_MODEL, D_FF)), _const_spec((D_FF, D_MODEL)), vec, vec,
        ],
        out_specs=row(D_MODEL),
        out_shape=jax.ShapeDtypeStruct((n, D_MODEL), F32),
        scratch_shapes=[pltpu.VMEM((tm, D_MODEL), F32)],
        compiler_params=_cparams(1),
        name="out_ffn",
    )(x2d, attn2d, rp2d, lw["w_out_attn"], lw["w_out_rp"], lw["ln1_g"], lw["ln1_b"],
      lw["w_ff1"], lw["w_ff2"], lw["ln2_g"], lw["ln2_b"])


def _block_diag(w):
    g, c, d = w.shape
    eye = jnp.eye(g, dtype=bool)[:, None, :, None]
    return jnp.where(eye, w[:, :, None, :], 0.0).reshape(g * c, g * d)


def _alibi_slopes():
    return jnp.exp2(-8.0 * (jnp.arange(N_HEADS, dtype=F32) + 1.0) / N_HEADS)


def _prompt_bias_tables():
    slopes = _alibi_slopes()
    jk = jnp.arange(2 * WINDOW)[:, None]
    tq = jnp.arange(WINDOW)[None, :]
    delta = tq + WINDOW - jk
    visible = (delta >= 0) & (delta <= WINDOW)
    bias = -slopes[None, :, None] * delta.astype(F32)[:, None, :]
    full = jnp.where(visible[:, None, :], bias, NEG_INF)
    first = jnp.where((visible & (jk >= WINDOW))[:, None, :], bias, NEG_INF)
    return jnp.stack([first, full]).reshape(2, 2 * WINDOW, N_HEADS * WINDOW)


def _layer_weights(l, w_in, conv_w, conv_b, gate_a_w, gate_a_b, gate_x_w, gate_x_b, lru_lambda,
                   pool_w, pool_scale, w_out, ln1_g, ln1_b, w_ff1, w_ff2, ln2_g, ln2_b):
    wq = w_in[l][:, :ATTN_WIDTH][:, ATTN_COL_PERM] * (HEAD_DIM ** -0.5)
    w_in_l = jnp.concatenate([wq, w_in[l][:, ATTN_WIDTH:]], axis=1).astype(BF16)
    row = lambda v: v[l].reshape(1, -1).astype(F32)
    return {
        "w_in": w_in_l,
        "conv_w": conv_w[l], "conv_b": row(conv_b),
        "w_gate": jnp.concatenate([_block_diag(gate_a_w[l]), _block_diag(gate_x_w[l])], axis=1).astype(BF16),
        "b_gate": jnp.concatenate([gate_a_b[l], gate_x_b[l]]).reshape(1, -1),
        "lam": row(lru_lambda),
        "w_pool": _block_diag(pool_w[l]).astype(BF16), "pool_scale": row(pool_scale),
        "w_out_attn": w_out[l][:ATTN_WIDTH][ATTN_COL_PERM].astype(BF16),
        "w_out_rp": w_out[l][ATTN_WIDTH:].astype(BF16),
        "ln1_g": row(ln1_g), "ln1_b": row(ln1_b),
        "w_ff1": w_ff1[l].astype(BF16), "w_ff2": w_ff2[l].astype(BF16),
        "ln2_g": row(ln2_g), "ln2_b": row(ln2_b),
    }


def kernel(x_prompt, x_sample, cache_k, cache_v, state_h, state_conv, state_pool, w_in, attn_sinks, conv_w, conv_b, gate_a_w, gate_a_b, gate_x_w, gate_x_b, lru_lambda, pool_w, pool_scale, w_out, ln1_g, ln1_b, w_ff1, w_ff2, ln2_g, ln2_b):
    batch, seq, _ = x_prompt.shape
    dec = x_sample.shape[0]
    bias_prompt = _prompt_bias_tables()
    bias_sample = -_alibi_slopes()[:, None] * (WINDOW - jnp.arange(WINDOW, dtype=F32))[None, :]

    def kv_view(c):
        return jnp.transpose(c, (0, 1, 3, 4, 2)).reshape(c.shape[0], c.shape[1], KV_WIDTH, WINDOW)

    def kv_unview(c_t):
        c5 = c_t.reshape(c_t.shape[0], c_t.shape[1], N_KV_HEADS, HEAD_DIM, WINDOW)
        return jnp.transpose(c5, (0, 1, 4, 2, 3))

    ck_t, cv_t = kv_view(cache_k), kv_view(cache_v)

    yp = x_prompt.reshape(batch * seq, D_MODEL)
    ys = x_sample.reshape(dec, D_MODEL)
    outs = {k: [] for k in ("pk", "pv", "ph", "pc", "pp", "kn", "vn", "sh", "sc", "sp")}
    for l in range(DEPTH):
        lw = _layer_weights(l, w_in, conv_w, conv_b, gate_a_w, gate_a_b, gate_x_w, gate_x_b, lru_lambda,
                            pool_w, pool_scale, w_out, ln1_g, ln1_b, w_ff1, w_ff2, ln2_g, ln2_b)
        sinks_l = attn_sinks[l].astype(F32)
        sink_row = jnp.repeat(sinks_l, WINDOW).reshape(1, N_HEADS * WINDOW)

        u3 = in_proj(yp, lw["w_in"], PROJ_TM).reshape(batch, seq, IN_WIDTH)
        attn, k_last_t, v_last_t = attn_prompt(u3, bias_prompt, sink_row)
        rp, h_last = mixer_prompt(u3, lw)
        yp = out_ffn(yp, attn.reshape(batch * seq, ATTN_WIDTH), rp.reshape(batch * seq, -1), lw, FFN_TM)
        outs["pk"].append(k_last_t)
        outs["pv"].append(v_last_t)
        outs["ph"].append(h_last.reshape(batch, LRU_WIDTH))
        outs["pc"].append(u3[:, seq - (CONV_W - 1):, 768:768 + LRU_WIDTH])
        outs["pp"].append(u3[:, seq - POOL_CTX:, 1280:1280 + POOL_WIDTH])

        us = in_proj(ys, lw["w_in"], dec)
        attn_s, rp_s, h_s = mixer_sample(
            l, us, us[:, :ATTN_WIDTH].reshape(dec, GQA_GROUP, LANES), ck_t, cv_t, state_h[l],
            jnp.swapaxes(state_conv[l], 0, 1), jnp.swapaxes(state_pool[l], 0, 1),
            bias_sample, sinks_l.reshape(N_HEADS, 1), lw)
        ys = out_ffn(ys, attn_s.reshape(dec, ATTN_WIDTH), rp_s, lw, dec)
        outs["kn"].append(us[:, ATTN_WIDTH:ATTN_WIDTH + KV_WIDTH])
        outs["vn"].append(us[:, ATTN_WIDTH + KV_WIDTH:ATTN_WIDTH + 2 * KV_WIDTH])
        outs["sh"].append(h_s)
        outs["sc"].append(jnp.concatenate([state_conv[l][:, 1:], us[:, None, 768:768 + LRU_WIDTH]], axis=1))
        outs["sp"].append(jnp.concatenate([state_pool[l][:, 1:], us[:, None, 1280:1280 + POOL_WIDTH]], axis=1))

    st = {k: jnp.stack(v) for k, v in outs.items()}
    sk_t, sv_t = cache_update(ck_t, cv_t, st["kn"], st["vn"])
    return (yp.reshape(batch, seq, D_MODEL), ys.reshape(dec, 1, D_MODEL),
            kv_unview(st["pk"]), kv_unview(st["pv"]), st["ph"], st["pc"], st["pp"],
            kv_unview(sk_t), kv_unview(sv_t), st["sh"], st["sc"], st["sp"])
```

```python
import functools

import jax
import jax.numpy as jnp
import numpy as np
from jax import lax
from jax.experimental import pallas as pl
from jax.experimental.pallas import tpu as pltpu

D_MODEL = 1024
DEPTH = 2
HEAD_DIM = 64
ATTN_WIDTH = 512
N_HEADS = 8
N_KV_HEADS = 2
GQA_GROUP = 4
KV_WIDTH = 128
WINDOW = 128
LRU_WIDTH = 256
LRU_C = 8.0
CONV_W = 4
POOL_WINDOWS = (2, 4, 8, 16)
POOL_WIDTH = 256
POOL_GROUP_W = 64
POOL_CTX = 15
IN_WIDTH = 1536
REST_WIDTH = IN_WIDTH - ATTN_WIDTH
D_FF = 4096
LN_EPS = 1e-5
NEG_INF = -1e30
ALPHA = (2.0 * DEPTH) ** 0.25
Q_SCALE = HEAD_DIM ** -0.5

K_BLK, V_BLK = 0, 1
XR_BLK, GR_BLK, ZP_BLK = 1, 2, 3

LANES = 128
SUBLANES = 8
VMEM_LIMIT_BYTES = 56 * 1024 * 1024

PROJ_TM = 512
MIX_T = 256
ATTN_QB = 4
FFN_TM = 512
FFN_FC = 1024
X_TAIL = SUBLANES
Z_TAIL = 2 * SUBLANES
SAMPLE_BT = 16
CACHE_BT = 32

BF16 = jnp.bfloat16
F32 = jnp.float32

HEAD_ORDER = (0, 4, 1, 5, 2, 6, 3, 7)
HEAD_ORDER_INV = tuple(int(i) for i in np.argsort(HEAD_ORDER))


def _cparams(n_grid):
    return pltpu.CompilerParams(
        dimension_semantics=("arbitrary",) * n_grid,
        vmem_limit_bytes=VMEM_LIMIT_BYTES,
    )


def _whole_spec(shape):
    nd = len(shape)
    return pl.BlockSpec(shape, lambda *_: (0,) * nd, pipeline_mode=pl.Buffered(1))


def _layer_spec(shape, l):
    nd = len(shape) - 1
    return pl.BlockSpec((None,) + tuple(shape[1:]), lambda *_: (l,) + (0,) * nd, pipeline_mode=pl.Buffered(1))


def _layer_norm(x, g, b):
    mu = jnp.mean(x, axis=-1, keepdims=True)
    xc = x - mu
    var = jnp.mean(xc * xc, axis=-1, keepdims=True)
    return xc * lax.rsqrt(var + LN_EPS) * g + b


def _gelu_tanh(x):
    return 0.5 * x * (1.0 + jnp.tanh(np.sqrt(2.0 / np.pi) * (x + 0.044715 * (x * x * x))))


def _sigmoid(x):
    return 1.0 / (1.0 + jnp.exp(-x))


def _softplus(x):
    return jnp.maximum(x, 0.0) + jnp.log(1.0 + jnp.exp(-jnp.abs(x)))


def _in_proj_kernel(x_ref, w_ref, q_ref, r_ref, wb_ref):
    @pl.when(pl.program_id(0) == 0)
    def _():
        lo = lax.broadcasted_iota(jnp.int32, (D_MODEL, LANES), 1) < HEAD_DIM
        src = [w_ref[:, s * LANES:(s + 1) * LANES] * Q_SCALE for s in range(ATTN_WIDTH // LANES)]
        swapped = [pltpu.roll(t, HEAD_DIM, 1) for t in src]
        for c in range(GQA_GROUP):
            s0, s1 = c // 2, GQA_GROUP // 2 + c // 2
            if c % 2 == 0:
                tile = jnp.where(lo, src[s0], swapped[s1])
            else:
                tile = jnp.where(lo, swapped[s0], src[s1])
            wb_ref[:, c * LANES:(c + 1) * LANES] = tile.astype(BF16)
        wb_ref[:, ATTN_WIDTH:] = w_ref[:, ATTN_WIDTH:].astype(BF16)

    u = jnp.dot(x_ref[...].astype(BF16), wb_ref[...], preferred_element_type=F32)
    q_ref[...] = u[:, :ATTN_WIDTH].astype(BF16)
    r_ref[...] = u[:, ATTN_WIDTH:]


def in_proj(l, x2d, w_in, tm):
    n = x2d.shape[0]
    return pl.pallas_call(
        _in_proj_kernel,
        grid=(n // tm,),
        in_specs=[
            pl.BlockSpec((tm, D_MODEL), lambda i: (i, 0)),
            _layer_spec(w_in.shape, l),
        ],
        out_specs=[
            pl.BlockSpec((tm, ATTN_WIDTH), lambda i: (i, 0)),
            pl.BlockSpec((tm, REST_WIDTH), lambda i: (i, 0)),
        ],
        out_shape=[
            jax.ShapeDtypeStruct((n, ATTN_WIDTH), BF16),
            jax.ShapeDtypeStruct((n, REST_WIDTH), F32),
        ],
        scratch_shapes=[pltpu.VMEM((D_MODEL, IN_WIDTH), BF16)],
        compiler_params=_cparams(1),
        name="in_proj",
    )(x2d, w_in)


def _attn_block(q, k2, v2, bias_t, sink, lo):
    zero = jnp.zeros((), q.dtype)
    tiles = [q[:, c * LANES:(c + 1) * LANES] for c in range(GQA_GROUP)]
    qs = jnp.concatenate([jnp.where(lo, t, zero) for t in tiles]
                         + [jnp.where(lo, zero, t) for t in tiles], axis=0)
    s = lax.dot_general(k2, qs, (((1,), (1,)), ((), ())), preferred_element_type=F32) + bias_t
    m = jnp.maximum(jnp.max(s, axis=0, keepdims=True), sink)
    p = jnp.exp(s - m)
    denom = jnp.sum(p, axis=0, keepdims=True) + jnp.exp(sink - m)
    o = lax.dot_general(v2, p.astype(BF16), (((0,), (0,)), ((), ())), preferred_element_type=F32)
    o = o * (1.0 / denom)
    cols = []
    for c in range(ATTN_WIDTH // LANES):
        kv = (2 * c) // GQA_GROUP
        rows = slice(kv * HEAD_DIM, (kv + 1) * HEAD_DIM)
        blk = jnp.concatenate([o[rows, (2 * c) * WINDOW:(2 * c + 1) * WINDOW],
                               o[rows, (2 * c + 1) * WINDOW:(2 * c + 2) * WINDOW]], axis=0)
        cols.append(blk.T)
    return jnp.concatenate(cols, axis=1)


def _attn_prompt_kernel(l, q_ref, kp_ref, kc_ref, vp_ref, vc_ref, bias0_ref, bias_ref, sink_ref,
                        o_ref, kt_ref, vt_ref):
    @pl.when(pl.program_id(1) == pl.num_programs(1) - 1)
    def _():
        tail = slice((ATTN_QB - 1) * WINDOW, ATTN_QB * WINDOW)
        kt_ref[...] = kc_ref[tail, :].T
        vt_ref[...] = vc_ref[tail, :].T

    lo = lax.broadcasted_iota(jnp.int32, (WINDOW, LANES), 1) < HEAD_DIM
    sink = sink_ref[l:l + 1, :]
    kc = kc_ref[...].astype(BF16)
    vc = vc_ref[...].astype(BF16)
    k_prev = kp_ref[...].astype(BF16)
    v_prev = vp_ref[...].astype(BF16)
    for qb in range(ATTN_QB):
        rows = slice(qb * WINDOW, (qb + 1) * WINDOW)
        k2 = jnp.concatenate([k_prev, kc[rows]], axis=0)
        v2 = jnp.concatenate([v_prev, vc[rows]], axis=0)
        bias_t = bias0_ref[...] if qb == 0 else bias_ref[...]
        o_ref[rows, :] = _attn_block(q_ref[rows, :], k2, v2, bias_t, sink, lo).astype(o_ref.dtype)
        k_prev, v_prev = kc[rows], vc[rows]


def attn_prompt(l, q3, r3, bias2, sink_rows):
    b, seq, _ = q3.shape
    tq = ATTN_QB * WINDOW
    prev = lambda col: (lambda bi, j: (bi, jnp.maximum(j * ATTN_QB - 1, 0), col))
    cur = lambda col: (lambda bi, j: (bi, j, col))
    bias_shape = (None, 2 * WINDOW, N_HEADS * WINDOW)
    return pl.pallas_call(
        functools.partial(_attn_prompt_kernel, l),
        grid=(b, seq // tq),
        in_specs=[
            pl.BlockSpec((None, tq, ATTN_WIDTH), lambda bi, j: (bi, j, 0)),
            pl.BlockSpec((None, WINDOW, KV_WIDTH), prev(K_BLK)),
            pl.BlockSpec((None, tq, KV_WIDTH), cur(K_BLK)),
            pl.BlockSpec((None, WINDOW, KV_WIDTH), prev(V_BLK)),
            pl.BlockSpec((None, tq, KV_WIDTH), cur(V_BLK)),
            pl.BlockSpec(bias_shape, lambda bi, j: (jnp.minimum(j, 1), 0, 0)),
            pl.BlockSpec(bias_shape, lambda bi, j: (1, 0, 0)),
            _whole_spec(sink_rows.shape),
        ],
        out_specs=[
            pl.BlockSpec((None, tq, ATTN_WIDTH), lambda bi, j: (bi, j, 0)),
            pl.BlockSpec((None, KV_WIDTH, WINDOW), lambda bi, j: (bi, 0, 0)),
            pl.BlockSpec((None, KV_WIDTH, WINDOW), lambda bi, j: (bi, 0, 0)),
        ],
        out_shape=[
            jax.ShapeDtypeStruct((b, seq, ATTN_WIDTH), BF16),
            jax.ShapeDtypeStruct((b, KV_WIDTH, WINDOW), F32),
            jax.ShapeDtypeStruct((b, KV_WIDTH, WINDOW), F32),
        ],
        compiler_params=_cparams(2),
        name="attn_prompt",
    )(q3, r3, r3, r3, r3, bias2, bias2, sink_rows)


def _gates(l, xc, wg_ref, ba_ref, bx_ref, lam_ref):
    g = jnp.dot(xc.astype(BF16), wg_ref[...], preferred_element_type=F32)
    r = _sigmoid(g[:, :LRU_WIDTH] + ba_ref[l:l + 1, :])
    i = _sigmoid(g[:, LRU_WIDTH:] + bx_ref[l:l + 1, :])
    log_a = (-LRU_C * r) * _softplus(-lam_ref[l:l + 1, :])
    a = jnp.exp(log_a)
    b = jnp.sqrt(1.0 - a * a) * (i * xc)
    return a, b


def _lru_scan(a, b, h_prev):
    t, w = a.shape
    groups = t // SUBLANES
    a3 = a.reshape(groups, SUBLANES, w)
    b3 = b.reshape(groups, SUBLANES, w)
    row = lax.broadcasted_iota(jnp.int32, (groups, SUBLANES, w), 1)
    s = 1
    while s < SUBLANES:
        keep = row >= s
        a_sh = jnp.where(keep, pltpu.roll(a3, s, 1), 1.0)
        b_sh = jnp.where(keep, pltpu.roll(b3, s, 1), 0.0)
        b3 = a3 * b_sh + b3
        a3 = a3 * a_sh
        s *= 2
    hs = []
    h = h_prev
    for g in range(groups):
        hg = a3[g] * h + b3[g]
        hs.append(hg)
        h = hg[SUBLANES - 1:SUBLANES, :]
    return jnp.concatenate(hs, axis=0), h


def _shift_rows(x, d):
    return pltpu.roll(x, d, 0)


def _pool_means(ze, pos1):
    t = ze.shape[0] - Z_TAIL
    lo = lax.broadcasted_iota(jnp.int32, (t, LANES), 1) < POOL_GROUP_W
    s2 = ze + _shift_rows(ze, 1)
    s4 = s2 + _shift_rows(s2, 2)
    hi4 = s4[:, LANES:]
    s8 = hi4 + _shift_rows(hi4, 4)
    s16 = s8 + _shift_rows(s8, 8)
    wins = (jnp.where(lo, s2[Z_TAIL:, :LANES], s4[Z_TAIL:, :LANES]),
            jnp.where(lo, s8[Z_TAIL:], s16[Z_TAIL:]))
    means = []
    for c, win in enumerate(wins):
        w_small, w_big = POOL_WINDOWS[2 * c], POOL_WINDOWS[2 * c + 1]
        count = jnp.where(lo, float(w_small), float(w_big))
        if pos1 is not None:
            count = jnp.minimum(pos1, count)
        means.append(win / count - ze[Z_TAIL:, c * LANES:(c + 1) * LANES])
    return jnp.concatenate(means, axis=1)


def _mixer_prompt_kernel(l, xr_ref, gr_ref, zp_ref, cw_ref, cb_ref, wg_ref, ba_ref, bx_ref, lam_ref,
                         wp_ref, ps_ref, o_ref, h_ref, xtail, ztail, hcar):
    j = pl.program_id(1)
    t = MIX_T

    @pl.when(j == 0)
    def _():
        xtail[...] = jnp.zeros_like(xtail)
        ztail[...] = jnp.zeros_like(ztail)
        hcar[...] = jnp.zeros_like(hcar)

    xr = xr_ref[...]
    zp = zp_ref[...]
    xe = jnp.concatenate([xtail[...], xr], axis=0)
    ze = jnp.concatenate([ztail[...], zp], axis=0)
    xtail[...] = xr[t - X_TAIL:, :]
    ztail[...] = zp[t - Z_TAIL:, :]

    cw = cw_ref[...]
    xc = cb_ref[l:l + 1, :] + xe[X_TAIL:] * cw[CONV_W - 1:CONV_W, :]
    for tap in range(CONV_W - 1):
        d = CONV_W - 1 - tap
        xc = xc + _shift_rows(xe, d)[X_TAIL:] * cw[tap:tap + 1, :]

    a, b = _gates(l, xc, wg_ref, ba_ref, bx_ref, lam_ref)
    hs, h_last = _lru_scan(a, b, hcar[...])
    hcar[...] = h_last
    h_ref[...] = h_last
    rec = hs * _gelu_tanh(gr_ref[...])

    pos1 = (j * t + lax.broadcasted_iota(jnp.int32, (t, LANES), 0) + 1).astype(F32)
    diff = _pool_means(ze, pos1).astype(BF16)
    pool = jnp.dot(diff, wp_ref[...], preferred_element_type=F32) * ps_ref[l:l + 1, :]

    o_ref[...] = jnp.concatenate([rec, pool], axis=1).astype(o_ref.dtype)


def mixer_prompt(l, r3, pw):
    b, seq, _ = r3.shape
    t = MIX_T
    col = lambda c: (lambda bi, j: (bi, j, c))
    return pl.pallas_call(
        functools.partial(_mixer_prompt_kernel, l),
        grid=(b, seq // t),
        in_specs=[
            pl.BlockSpec((None, t, LRU_WIDTH), col(XR_BLK)),
            pl.BlockSpec((None, t, LRU_WIDTH), col(GR_BLK)),
            pl.BlockSpec((None, t, POOL_WIDTH), col(ZP_BLK)),
            _layer_spec(pw["conv_w"].shape, l), _whole_spec(pw["conv_b"].shape),
            _layer_spec(pw["w_gate"].shape, l), _whole_spec(pw["gate_a_b"].shape),
            _whole_spec(pw["gate_x_b"].shape), _whole_spec(pw["lam"].shape),
            _layer_spec(pw["w_pool"].shape, l), _whole_spec(pw["pool_scale"].shape),
        ],
        out_specs=[
            pl.BlockSpec((None, t, LRU_WIDTH + POOL_WIDTH), lambda bi, j: (bi, j, 0)),
            pl.BlockSpec((None, 1, LRU_WIDTH), lambda bi, j: (bi, 0, 0)),
        ],
        out_shape=[
            jax.ShapeDtypeStruct((b, seq, LRU_WIDTH + POOL_WIDTH), BF16),
            jax.ShapeDtypeStruct((b, 1, LRU_WIDTH), F32),
        ],
        scratch_shapes=[
            pltpu.VMEM((X_TAIL, LRU_WIDTH), F32),
            pltpu.VMEM((Z_TAIL, POOL_WIDTH), F32),
            pltpu.VMEM((1, LRU_WIDTH), F32),
        ],
        compiler_params=_cparams(2),
        name="mixer_prompt",
    )(r3, r3, r3, pw["conv_w"], pw["conv_b"], pw["w_gate"], pw["gate_a_b"], pw["gate_x_b"], pw["lam"],
      pw["w_pool"], pw["pool_scale"])


def _mixer_sample_kernel(l, q_ref, kn_ref, vn_ref, xr_ref, gr_ref, zp_ref, ck_ref, cv_ref,
                         h0_ref, sc_ref, sp_ref, bias_ref, sink_ref,
                         cw_ref, cb_ref, wg_ref, ba_ref, bx_ref, lam_ref, wp_ref, ps_ref,
                         attn_ref, rp_ref, h_ref):
    bt = SAMPLE_BT
    lo3 = lax.broadcasted_iota(jnp.int32, (bt, GQA_GROUP, LANES), 2) < HEAD_DIM
    q4 = q_ref[...].astype(F32)
    qm = jnp.concatenate([jnp.where(lo3, q4, 0.0), jnp.where(lo3, 0.0, q4)], axis=1)
    s = jnp.einsum("bqc,bck->bqk", qm.astype(BF16), ck_ref[...].astype(BF16),
                   preferred_element_type=F32) + bias_ref[...]
    s_new = jnp.sum(qm * kn_ref[...][:, None, :], axis=2, keepdims=True)
    sink = sink_ref[:, l:l + 1]
    m = jnp.maximum(jnp.maximum(jnp.max(s, axis=2, keepdims=True), s_new), sink)
    p = jnp.exp(s - m)
    p_new = jnp.exp(s_new - m)
    denom = jnp.sum(p, axis=2, keepdims=True) + p_new + jnp.exp(sink - m)
    o = jnp.einsum("bqk,bck->bqc", p.astype(BF16), cv_ref[...].astype(BF16),
                   preferred_element_type=F32)
    o = (o + p_new * vn_ref[...][:, None, :]) / denom
    attn_ref[...] = jnp.where(lo3, o[:, :GQA_GROUP, :], o[:, GQA_GROUP:, :]).astype(attn_ref.dtype)

    xr = xr_ref[...]
    cw = cw_ref[...]
    xc = cb_ref[l:l + 1, :] + xr * cw[CONV_W - 1:CONV_W, :]
    for tap in range(CONV_W - 1):
        xc = xc + sc_ref[tap] * cw[tap:tap + 1, :]
    a, b = _gates(l, xc, wg_ref, ba_ref, bx_ref, lam_ref)
    h = a * h0_ref[...] + b
    h_ref[...] = h
    rec = h * _gelu_tanh(gr_ref[...])

    z = zp_ref[...]
    lo = lax.broadcasted_iota(jnp.int32, (bt, LANES), 1) < POOL_GROUP_W
    means = []
    for c in range(POOL_WIDTH // LANES):
        w_small, w_big = POOL_WINDOWS[2 * c], POOL_WINDOWS[2 * c + 1]
        cols = slice(c * LANES, (c + 1) * LANES)
        zc = z[:, cols]
        acc = zc
        small = None
        for d in range(1, w_big):
            acc = acc + sp_ref[POOL_CTX - d][:, cols]
            if d + 1 == w_small:
                small = acc
        win = jnp.where(lo, small, acc)
        count = jnp.where(lo, float(w_small), float(w_big))
        means.append(win / count - zc)
    diff = jnp.concatenate(means, axis=1).astype(BF16)
    pool = jnp.dot(diff, wp_ref[...], preferred_element_type=F32) * ps_ref[l:l + 1, :]
    rp_ref[...] = jnp.concatenate([rec, pool], axis=1).astype(rp_ref.dtype)


def mixer_sample(l, q4, rs, ck_t, cv_t, state_h, state_conv_t, state_pool_t, bias_s, sinks_t, pw):
    n = rs.shape[0]
    bt = SAMPLE_BT
    cache_spec = pl.BlockSpec((None, bt, KV_WIDTH, WINDOW), lambda i: (l, i, 0, 0))
    rcol = lambda width, c: pl.BlockSpec((bt, width), lambda i: (i, c))
    return pl.pallas_call(
        functools.partial(_mixer_sample_kernel, l),
        grid=(n // bt,),
        in_specs=[
            pl.BlockSpec((bt, GQA_GROUP, LANES), lambda i: (i, 0, 0)),
            rcol(KV_WIDTH, K_BLK), rcol(KV_WIDTH, V_BLK),
            rcol(LRU_WIDTH, XR_BLK), rcol(LRU_WIDTH, GR_BLK), rcol(POOL_WIDTH, ZP_BLK),
            cache_spec, cache_spec,
            pl.BlockSpec((None, bt, LRU_WIDTH), lambda i: (l, i, 0)),
            pl.BlockSpec((None, CONV_W - 1, bt, LRU_WIDTH), lambda i: (l, 0, i, 0)),
            pl.BlockSpec((None, POOL_CTX, bt, POOL_WIDTH), lambda i: (l, 0, i, 0)),
            _whole_spec(bias_s.shape),
            _whole_spec(sinks_t.shape),
            _layer_spec(pw["conv_w"].shape, l), _whole_spec(pw["conv_b"].shape),
            _layer_spec(pw["w_gate"].shape, l), _whole_spec(pw["gate_a_b"].shape),
            _whole_spec(pw["gate_x_b"].shape), _whole_spec(pw["lam"].shape),
            _layer_spec(pw["w_pool"].shape, l), _whole_spec(pw["pool_scale"].shape),
        ],
        out_specs=[
            pl.BlockSpec((bt, GQA_GROUP, LANES), lambda i: (i, 0, 0)),
            pl.BlockSpec((bt, LRU_WIDTH + POOL_WIDTH), lambda i: (i, 0)),
            pl.BlockSpec((bt, LRU_WIDTH), lambda i: (i, 0)),
        ],
        out_shape=[
            jax.ShapeDtypeStruct((n, GQA_GROUP, LANES), BF16),
            jax.ShapeDtypeStruct((n, LRU_WIDTH + POOL_WIDTH), BF16),
            jax.ShapeDtypeStruct((n, LRU_WIDTH), F32),
        ],
        compiler_params=_cparams(1),
        name="mixer_sample",
    )(q4, rs, rs, rs, rs, rs, ck_t, cv_t, state_h, state_conv_t, state_pool_t, bias_s, sinks_t,
      pw["conv_w"], pw["conv_b"], pw["w_gate"], pw["gate_a_b"], pw["gate_x_b"], pw["lam"],
      pw["w_pool"], pw["pool_scale"])


def _cache_update_kernel(ck_ref, cv_ref, kn_ref, vn_ref, ok_ref, ov_ref):
    bt = CACHE_BT
    last = lax.broadcasted_iota(jnp.int32, (KV_WIDTH, WINDOW), 1) == WINDOW - 1
    pad = jnp.zeros((LANES - bt, KV_WIDTH), F32)
    for src, new, dst in ((ck_ref, kn_ref, ok_ref), (cv_ref, vn_ref, ov_ref)):
        new_t = jnp.concatenate([new[...], pad], axis=0).T
        for s in range(bt):
            shifted = pltpu.roll(src[s], WINDOW - 1, 1)
            col = pltpu.roll(new_t, WINDOW - 1 - s, 1)
            dst[s] = jnp.where(last, col, shifted)


def cache_update(ck_t, cv_t, k_new, v_new):
    depth, n = k_new.shape[:2]
    bt = CACHE_BT
    cache_spec = pl.BlockSpec((None, bt, KV_WIDTH, WINDOW), lambda l, i: (l, i, 0, 0))
    new_spec = pl.BlockSpec((None, bt, KV_WIDTH), lambda l, i: (l, i, 0))
    shape = jax.ShapeDtypeStruct(ck_t.shape, F32)
    return pl.pallas_call(
        _cache_update_kernel,
        grid=(depth, n // bt),
        in_specs=[cache_spec, cache_spec, new_spec, new_spec],
        out_specs=[cache_spec, cache_spec],
        out_shape=[shape, shape],
        compiler_params=_cparams(2),
        name="cache_update",
    )(ck_t, cv_t, k_new, v_new)


def _out_ffn_kernel(l, x_ref, at_ref, rp_ref, wo_ref, g1_ref, b1_ref,
                    w1_ref, w2_ref, g2_ref, b2_ref, y_ref, acc_ref):
    mix = jnp.dot(at_ref[...], wo_ref[:ATTN_WIDTH, :], preferred_element_type=F32)
    mix = mix + jnp.dot(rp_ref[...], wo_ref[ATTN_WIDTH:, :], preferred_element_type=F32)
    x1 = _layer_norm(ALPHA * x_ref[...] + mix, g1_ref[l:l + 1, :], b1_ref[l:l + 1, :])
    x1b = x1.astype(BF16)
    for c in range(D_FF // FFN_FC):
        cols = slice(c * FFN_FC, (c + 1) * FFN_FC)
        hid = jnp.dot(x1b, w1_ref[:, cols], preferred_element_type=F32)
        hid = jnp.square(jnp.maximum(hid, 0.0)).astype(BF16)
        part = jnp.dot(hid, w2_ref[cols, :], preferred_element_type=F32)
        if c == 0:
            acc_ref[...] = part
        else:
            acc_ref[...] += part
    y_ref[...] = _layer_norm(ALPHA * x1 + acc_ref[...], g2_ref[l:l + 1, :], b2_ref[l:l + 1, :])


def out_ffn(l, x2d, attn2d, rp2d, fw, tm):
    n = x2d.shape[0]
    row = lambda width: pl.BlockSpec((tm, width), lambda i: (i, 0))
    vec = _whole_spec((DEPTH, D_MODEL))
    return pl.pallas_call(
        functools.partial(_out_ffn_kernel, l),
        grid=(n // tm,),
        in_specs=[
            row(D_MODEL), row(ATTN_WIDTH), row(LRU_WIDTH + POOL_WIDTH),
            _layer_spec(fw["w_out"].shape, l), vec, vec,
            _layer_spec(fw["w_ff1"].shape, l), _layer_spec(fw["w_ff2"].shape, l), vec, vec,
        ],
        out_specs=row(D_MODEL),
        out_shape=jax.ShapeDtypeStruct((n, D_MODEL), F32),
        scratch_shapes=[pltpu.VMEM((tm, D_MODEL), F32)],
        compiler_params=_cparams(1),
        name="out_ffn",
    )(x2d, attn2d, rp2d, fw["w_out"], fw["ln1_g"], fw["ln1_b"],
      fw["w_ff1"], fw["w_ff2"], fw["ln2_g"], fw["ln2_b"])


def _block_diag(w):
    depth, g, c, d = w.shape
    eye = jnp.eye(g, dtype=bool)[None, :, None, :, None]
    return jnp.where(eye, w[:, :, :, None, :], 0.0).reshape(depth, g * c, g * d)


def _alibi_slopes():
    return jnp.exp2(-8.0 * (jnp.arange(N_HEADS, dtype=F32) + 1.0) / N_HEADS)


def _prompt_bias_tables():
    slopes = _alibi_slopes()
    jk = jnp.arange(2 * WINDOW)[:, None]
    tq = jnp.arange(WINDOW)[None, :]
    delta = tq + WINDOW - jk
    visible = (delta >= 0) & (delta <= WINDOW)
    bias = -slopes[None, :, None] * delta.astype(F32)[:, None, :]
    full = jnp.where(visible[:, None, :], bias, NEG_INF)
    first = jnp.where((visible & (jk >= WINDOW))[:, None, :], bias, NEG_INF)
    return jnp.stack([first, full]).reshape(2, 2 * WINDOW, N_HEADS * WINDOW)


def kernel(x_prompt, x_sample, cache_k, cache_v, state_h, state_conv, state_pool, w_in, attn_sinks, conv_w, conv_b, gate_a_w, gate_a_b, gate_x_w, gate_x_b, lru_lambda, pool_w, pool_scale, w_out, ln1_g, ln1_b, w_ff1, w_ff2, ln2_g, ln2_b):
    batch, seq, _ = x_prompt.shape
    dec = x_sample.shape[0]
    bias_prompt = _prompt_bias_tables()
    bias_sample = -_alibi_slopes()[:, None] * (WINDOW - jnp.arange(WINDOW, dtype=F32))[None, :]
    sink_rows = jnp.repeat(attn_sinks, WINDOW, axis=1)
    sinks_t = attn_sinks.T

    pw = {
        "conv_w": conv_w, "conv_b": conv_b,
        "w_gate": jnp.concatenate([_block_diag(gate_a_w), _block_diag(gate_x_w)], axis=2).astype(BF16),
        "gate_a_b": gate_a_b, "gate_x_b": gate_x_b, "lam": lru_lambda,
        "w_pool": _block_diag(pool_w).astype(BF16), "pool_scale": pool_scale,
    }
    fw = {
        "w_out": w_out.astype(BF16), "w_ff1": w_ff1.astype(BF16), "w_ff2": w_ff2.astype(BF16),
        "ln1_g": ln1_g, "ln1_b": ln1_b, "ln2_g": ln2_g, "ln2_b": ln2_b,
    }

    def kv_view(c):
        return jnp.transpose(c, (0, 1, 3, 4, 2)).reshape(c.shape[0], c.shape[1], KV_WIDTH, WINDOW)

    def kv_unview(c_t):
        c5 = c_t.reshape(c_t.shape[0], c_t.shape[1], N_KV_HEADS, HEAD_DIM, WINDOW)
        return jnp.transpose(c5, (0, 1, 4, 2, 3))

    ck_t, cv_t = kv_view(cache_k), kv_view(cache_v)
    state_conv_t = jnp.swapaxes(state_conv, 1, 2)
    state_pool_t = jnp.swapaxes(state_pool, 1, 2)

    yp = x_prompt.reshape(batch * seq, D_MODEL)
    ys = x_sample.reshape(dec, D_MODEL)
    xr_cols = slice(XR_BLK * LRU_WIDTH, (XR_BLK + 1) * LRU_WIDTH)
    zp_cols = slice(ZP_BLK * LRU_WIDTH, (ZP_BLK + 1) * LRU_WIDTH)
    outs = {k: [] for k in ("pk", "pv", "ph", "pc", "pp", "kn", "vn", "sh", "sc", "sp")}
    for l in range(DEPTH):
        q2, r2 = in_proj(l, yp, w_in, PROJ_TM)
        r3 = r2.reshape(batch, seq, REST_WIDTH)
        attn, k_last_t, v_last_t = attn_prompt(l, q2.reshape(batch, seq, ATTN_WIDTH), r3, bias_prompt, sink_rows)
        rp, h_last = mixer_prompt(l, r3, pw)
        yp = out_ffn(l, yp, attn.reshape(batch * seq, ATTN_WIDTH), rp.reshape(batch * seq, -1), fw, FFN_TM)
        outs["pk"].append(k_last_t)
        outs["pv"].append(v_last_t)
        outs["ph"].append(h_last.reshape(batch, LRU_WIDTH))
        outs["pc"].append(r3[:, seq - (CONV_W - 1):, xr_cols])
        outs["pp"].append(r3[:, seq - POOL_CTX:, zp_cols])

        qs, rs = in_proj(l, ys, w_in, dec)
        attn_s, rp_s, h_s = mixer_sample(
            l, qs.reshape(dec, GQA_GROUP, LANES), rs, ck_t, cv_t, state_h, state_conv_t, state_pool_t,
            bias_sample, sinks_t, pw)
        attn_s = attn_s.reshape(dec, N_HEADS, HEAD_DIM)[:, HEAD_ORDER_INV, :].reshape(dec, ATTN_WIDTH)
        ys = out_ffn(l, ys, attn_s, rp_s, fw, dec)
        outs["kn"].append(rs[:, K_BLK * KV_WIDTH:(K_BLK + 1) * KV_WIDTH])
        outs["vn"].append(rs[:, V_BLK * KV_WIDTH:(V_BLK + 1) * KV_WIDTH])
        outs["sh"].append(h_s)
        outs["sc"].append(jnp.concatenate([state_conv[l][:, 1:], rs[:, None, xr_cols]], axis=1))
        outs["sp"].append(jnp.concatenate([state_pool[l][:, 1:], rs[:, None, zp_cols]], axis=1))

    st = {k: jnp.stack(v) for k, v in outs.items()}
    sk_t, sv_t = cache_update(ck_t, cv_t, st["kn"], st["vn"])
    return (yp.reshape(batch, seq, D_MODEL), ys.reshape(dec, 1, D_MODEL),
            kv_unview(st["pk"]), kv_unview(st["pv"]), st["ph"], st["pc"], st["pp"],
            kv_unview(sk_t), kv_unview(sv_t), st["sh"], st["sc"], st["sp"])
```

```python
import functools

import jax
import jax.numpy as jnp
import numpy as np
from jax import lax
from jax.experimental import pallas as pl
from jax.experimental.pallas import tpu as pltpu

D_MODEL = 1024
DEPTH = 2
HEAD_DIM = 64
ATTN_WIDTH = 512
N_HEADS = 8
N_KV_HEADS = 2
GQA_GROUP = 4
KV_WIDTH = 128
WINDOW = 128
LRU_WIDTH = 256
LRU_C = 8.0
CONV_W = 4
POOL_WINDOWS = (2, 4, 8, 16)
POOL_WIDTH = 256
POOL_GROUP_W = 64
POOL_CTX = 15
IN_WIDTH = 1536
REST_WIDTH = IN_WIDTH - ATTN_WIDTH
D_FF = 4096
LN_EPS = 1e-5
NEG_INF = -1e30
ALPHA = (2.0 * DEPTH) ** 0.25
Q_SCALE = HEAD_DIM ** -0.5

K_BLK, V_BLK = 0, 1
XR_BLK, GR_BLK, ZP_BLK = 1, 2, 3

LANES = 128
SUBLANES = 8
VMEM_LIMIT_BYTES = 56 * 1024 * 1024

PROJ_TM = 512
MIX_T = 256
ATTN_QB = 4
FFN_TM = 1024
FFN_FC = 1024
FFN_SUB = 256
X_TAIL = SUBLANES
Z_TAIL = 2 * SUBLANES
SAMPLE_BT = 16
CACHE_BT = 32

BF16 = jnp.bfloat16
F32 = jnp.float32

HEAD_ORDER = (0, 4, 1, 5, 2, 6, 3, 7)
HEAD_ORDER_INV = tuple(int(i) for i in np.argsort(HEAD_ORDER))


def _cparams(n_grid):
    return pltpu.CompilerParams(
        dimension_semantics=("arbitrary",) * n_grid,
        vmem_limit_bytes=VMEM_LIMIT_BYTES,
    )


def _whole_spec(shape):
    nd = len(shape)
    return pl.BlockSpec(shape, lambda *_: (0,) * nd, pipeline_mode=pl.Buffered(1))


def _layer_spec(shape, l):
    nd = len(shape) - 1
    return pl.BlockSpec((None,) + tuple(shape[1:]), lambda *_: (l,) + (0,) * nd, pipeline_mode=pl.Buffered(1))


def _layer_norm(x, g, b):
    mu = jnp.mean(x, axis=-1, keepdims=True)
    xc = x - mu
    var = jnp.mean(xc * xc, axis=-1, keepdims=True)
    return xc * lax.rsqrt(var + LN_EPS) * g + b


def _gelu_tanh(x):
    return 0.5 * x * (1.0 + jnp.tanh(np.sqrt(2.0 / np.pi) * (x + 0.044715 * (x * x * x))))


def _sigmoid(x):
    return 1.0 / (1.0 + jnp.exp(-x))


def _softplus(x):
    return jnp.maximum(x, 0.0) + jnp.log(1.0 + jnp.exp(-jnp.abs(x)))


def _in_proj_kernel(x_ref, w_ref, q_ref, r_ref, wb_ref):
    @pl.when(pl.program_id(0) == 0)
    def _():
        lo = lax.broadcasted_iota(jnp.int32, (D_MODEL, LANES), 1) < HEAD_DIM
        src = [w_ref[:, s * LANES:(s + 1) * LANES] * Q_SCALE for s in range(ATTN_WIDTH // LANES)]
        swapped = [pltpu.roll(t, HEAD_DIM, 1) for t in src]
        for c in range(GQA_GROUP):
            s0, s1 = c // 2, GQA_GROUP // 2 + c // 2
            if c % 2 == 0:
                tile = jnp.where(lo, src[s0], swapped[s1])
            else:
                tile = jnp.where(lo, swapped[s0], src[s1])
            wb_ref[:, c * LANES:(c + 1) * LANES] = tile.astype(BF16)
        wb_ref[:, ATTN_WIDTH:] = w_ref[:, ATTN_WIDTH:].astype(BF16)

    u = jnp.dot(x_ref[...].astype(BF16), wb_ref[...], preferred_element_type=F32)
    q_ref[...] = u[:, :ATTN_WIDTH].astype(BF16)
    r_ref[...] = u[:, ATTN_WIDTH:]


def in_proj(l, x2d, w_in, tm):
    n = x2d.shape[0]
    return pl.pallas_call(
        _in_proj_kernel,
        grid=(n // tm,),
        in_specs=[
            pl.BlockSpec((tm, D_MODEL), lambda i: (i, 0)),
            _layer_spec(w_in.shape, l),
        ],
        out_specs=[
            pl.BlockSpec((tm, ATTN_WIDTH), lambda i: (i, 0)),
            pl.BlockSpec((tm, REST_WIDTH), lambda i: (i, 0)),
        ],
        out_shape=[
            jax.ShapeDtypeStruct((n, ATTN_WIDTH), BF16),
            jax.ShapeDtypeStruct((n, REST_WIDTH), F32),
        ],
        scratch_shapes=[pltpu.VMEM((D_MODEL, IN_WIDTH), BF16)],
        compiler_params=_cparams(1),
        name="in_proj",
    )(x2d, w_in)


def _attn_block(q, k2, v2, bias_t, sink, lo):
    zero = jnp.zeros((), q.dtype)
    tiles = [q[:, c * LANES:(c + 1) * LANES] for c in range(GQA_GROUP)]
    qs = jnp.concatenate([jnp.where(lo, t, zero) for t in tiles]
                         + [jnp.where(lo, zero, t) for t in tiles], axis=0)
    s = lax.dot_general(k2, qs, (((1,), (1,)), ((), ())), preferred_element_type=F32) + bias_t
    m = jnp.maximum(jnp.max(s, axis=0, keepdims=True), sink)
    p = jnp.exp(s - m)
    denom = jnp.sum(p, axis=0, keepdims=True) + jnp.exp(sink - m)
    o = lax.dot_general(v2, p.astype(BF16), (((0,), (0,)), ((), ())), preferred_element_type=F32)
    o = o * (1.0 / denom)
    cols = []
    for c in range(ATTN_WIDTH // LANES):
        kv = (2 * c) // GQA_GROUP
        rows = slice(kv * HEAD_DIM, (kv + 1) * HEAD_DIM)
        blk = jnp.concatenate([o[rows, (2 * c) * WINDOW:(2 * c + 1) * WINDOW],
                               o[rows, (2 * c + 1) * WINDOW:(2 * c + 2) * WINDOW]], axis=0)
        cols.append(blk.T)
    return jnp.concatenate(cols, axis=1)


def _attn_prompt_kernel(l, q_ref, kp_ref, kc_ref, vp_ref, vc_ref, bias0_ref, bias_ref, sink_ref,
                        o_ref, kt_ref, vt_ref):
    @pl.when(pl.program_id(1) == pl.num_programs(1) - 1)
    def _():
        tail = slice((ATTN_QB - 1) * WINDOW, ATTN_QB * WINDOW)
        kt_ref[...] = kc_ref[tail, :].T
        vt_ref[...] = vc_ref[tail, :].T

    lo = lax.broadcasted_iota(jnp.int32, (WINDOW, LANES), 1) < HEAD_DIM
    sink = sink_ref[l:l + 1, :]
    kc = kc_ref[...].astype(BF16)
    vc = vc_ref[...].astype(BF16)
    k_prev = kp_ref[...].astype(BF16)
    v_prev = vp_ref[...].astype(BF16)
    for qb in range(ATTN_QB):
        rows = slice(qb * WINDOW, (qb + 1) * WINDOW)
        k2 = jnp.concatenate([k_prev, kc[rows]], axis=0)
        v2 = jnp.concatenate([v_prev, vc[rows]], axis=0)
        bias_t = bias0_ref[...] if qb == 0 else bias_ref[...]
        o_ref[rows, :] = _attn_block(q_ref[rows, :], k2, v2, bias_t, sink, lo).astype(o_ref.dtype)
        k_prev, v_prev = kc[rows], vc[rows]


def attn_prompt(l, q3, r3, bias2, sink_rows):
    b, seq, _ = q3.shape
    tq = ATTN_QB * WINDOW
    prev = lambda col: (lambda bi, j: (bi, jnp.maximum(j * ATTN_QB - 1, 0), col))
    cur = lambda col: (lambda bi, j: (bi, j, col))
    bias_shape = (None, 2 * WINDOW, N_HEADS * WINDOW)
    return pl.pallas_call(
        functools.partial(_attn_prompt_kernel, l),
        grid=(b, seq // tq),
        in_specs=[
            pl.BlockSpec((None, tq, ATTN_WIDTH), lambda bi, j: (bi, j, 0)),
            pl.BlockSpec((None, WINDOW, KV_WIDTH), prev(K_BLK)),
            pl.BlockSpec((None, tq, KV_WIDTH), cur(K_BLK)),
            pl.BlockSpec((None, WINDOW, KV_WIDTH), prev(V_BLK)),
            pl.BlockSpec((None, tq, KV_WIDTH), cur(V_BLK)),
            pl.BlockSpec(bias_shape, lambda bi, j: (jnp.minimum(j, 1), 0, 0)),
            pl.BlockSpec(bias_shape, lambda bi, j: (1, 0, 0)),
            _whole_spec(sink_rows.shape),
        ],
        out_specs=[
            pl.BlockSpec((None, tq, ATTN_WIDTH), lambda bi, j: (bi, j, 0)),
            pl.BlockSpec((None, KV_WIDTH, WINDOW), lambda bi, j: (bi, 0, 0)),
            pl.BlockSpec((None, KV_WIDTH, WINDOW), lambda bi, j: (bi, 0, 0)),
        ],
        out_shape=[
            jax.ShapeDtypeStruct((b, seq, ATTN_WIDTH), BF16),
            jax.ShapeDtypeStruct((b, KV_WIDTH, WINDOW), F32),
            jax.ShapeDtypeStruct((b, KV_WIDTH, WINDOW), F32),
        ],
        compiler_params=_cparams(2),
        name="attn_prompt",
    )(q3, r3, r3, r3, r3, bias2, bias2, sink_rows)


def _gates(l, xc, wg_ref, ba_ref, bx_ref, lam_ref):
    g = jnp.dot(xc.astype(BF16), wg_ref[...], preferred_element_type=F32)
    r = _sigmoid(g[:, :LRU_WIDTH] + ba_ref[l:l + 1, :])
    i = _sigmoid(g[:, LRU_WIDTH:] + bx_ref[l:l + 1, :])
    log_a = (-LRU_C * r) * _softplus(-lam_ref[l:l + 1, :])
    a = jnp.exp(log_a)
    b = jnp.sqrt(1.0 - a * a) * (i * xc)
    return a, b


def _lru_scan(a, b, h_prev):
    t, w = a.shape
    groups = t // SUBLANES
    a3 = a.reshape(groups, SUBLANES, w)
    b3 = b.reshape(groups, SUBLANES, w)
    row = lax.broadcasted_iota(jnp.int32, (groups, SUBLANES, w), 1)
    s = 1
    while s < SUBLANES:
        keep = row >= s
        a_sh = jnp.where(keep, pltpu.roll(a3, s, 1), 1.0)
        b_sh = jnp.where(keep, pltpu.roll(b3, s, 1), 0.0)
        b3 = a3 * b_sh + b3
        a3 = a3 * a_sh
        s *= 2
    hs = []
    h = h_prev
    for g in range(groups):
        hg = a3[g] * h + b3[g]
        hs.append(hg)
        h = hg[SUBLANES - 1:SUBLANES, :]
    return jnp.concatenate(hs, axis=0), h


def _shift_rows(x, d):
    return pltpu.roll(x, d, 0)


def _pool_means(ze, pos1):
    t = ze.shape[0] - Z_TAIL
    lo = lax.broadcasted_iota(jnp.int32, (t, LANES), 1) < POOL_GROUP_W
    s2 = ze + _shift_rows(ze, 1)
    s4 = s2 + _shift_rows(s2, 2)
    hi4 = s4[:, LANES:]
    s8 = hi4 + _shift_rows(hi4, 4)
    s16 = s8 + _shift_rows(s8, 8)
    wins = (jnp.where(lo, s2[Z_TAIL:, :LANES], s4[Z_TAIL:, :LANES]),
            jnp.where(lo, s8[Z_TAIL:], s16[Z_TAIL:]))
    means = []
    for c, win in enumerate(wins):
        w_small, w_big = POOL_WINDOWS[2 * c], POOL_WINDOWS[2 * c + 1]
        count = jnp.where(lo, float(w_small), float(w_big))
        if pos1 is not None:
            count = jnp.minimum(pos1, count)
        means.append(win / count - ze[Z_TAIL:, c * LANES:(c + 1) * LANES])
    return jnp.concatenate(means, axis=1)


def _mixer_prompt_kernel(l, xr_ref, gr_ref, zp_ref, cw_ref, cb_ref, wg_ref, ba_ref, bx_ref, lam_ref,
                         wp_ref, ps_ref, o_ref, h_ref, xtail, ztail, hcar):
    j = pl.program_id(1)
    t = MIX_T

    @pl.when(j == 0)
    def _():
        xtail[...] = jnp.zeros_like(xtail)
        ztail[...] = jnp.zeros_like(ztail)
        hcar[...] = jnp.zeros_like(hcar)

    xr = xr_ref[...]
    zp = zp_ref[...]
    xe = jnp.concatenate([xtail[...], xr], axis=0)
    ze = jnp.concatenate([ztail[...], zp], axis=0)
    xtail[...] = xr[t - X_TAIL:, :]
    ztail[...] = zp[t - Z_TAIL:, :]

    cw = cw_ref[...]
    xc = cb_ref[l:l + 1, :] + xe[X_TAIL:] * cw[CONV_W - 1:CONV_W, :]
    for tap in range(CONV_W - 1):
        d = CONV_W - 1 - tap
        xc = xc + _shift_rows(xe, d)[X_TAIL:] * cw[tap:tap + 1, :]

    a, b = _gates(l, xc, wg_ref, ba_ref, bx_ref, lam_ref)
    hs, h_last = _lru_scan(a, b, hcar[...])
    hcar[...] = h_last
    h_ref[...] = h_last
    rec = hs * _gelu_tanh(gr_ref[...])

    pos1 = (j * t + lax.broadcasted_iota(jnp.int32, (t, LANES), 0) + 1).astype(F32)
    diff = _pool_means(ze, pos1).astype(BF16)
    pool = jnp.dot(diff, wp_ref[...], preferred_element_type=F32) * ps_ref[l:l + 1, :]

    o_ref[...] = jnp.concatenate([rec, pool], axis=1).astype(o_ref.dtype)


def mixer_prompt(l, r3, pw):
    b, seq, _ = r3.shape
    t = MIX_T
    col = lambda c: (lambda bi, j: (bi, j, c))
    return pl.pallas_call(
        functools.partial(_mixer_prompt_kernel, l),
        grid=(b, seq // t),
        in_specs=[
            pl.BlockSpec((None, t, LRU_WIDTH), col(XR_BLK)),
            pl.BlockSpec((None, t, LRU_WIDTH), col(GR_BLK)),
            pl.BlockSpec((None, t, POOL_WIDTH), col(ZP_BLK)),
            _layer_spec(pw["conv_w"].shape, l), _whole_spec(pw["conv_b"].shape),
            _layer_spec(pw["w_gate"].shape, l), _whole_spec(pw["gate_a_b"].shape),
            _whole_spec(pw["gate_x_b"].shape), _whole_spec(pw["lam"].shape),
            _layer_spec(pw["w_pool"].shape, l), _whole_spec(pw["pool_scale"].shape),
        ],
        out_specs=[
            pl.BlockSpec((None, t, LRU_WIDTH + POOL_WIDTH), lambda bi, j: (bi, j, 0)),
            pl.BlockSpec((None, 1, LRU_WIDTH), lambda bi, j: (bi, 0, 0)),
        ],
        out_shape=[
            jax.ShapeDtypeStruct((b, seq, LRU_WIDTH + POOL_WIDTH), BF16),
            jax.ShapeDtypeStruct((b, 1, LRU_WIDTH), F32),
        ],
        scratch_shapes=[
            pltpu.VMEM((X_TAIL, LRU_WIDTH), F32),
            pltpu.VMEM((Z_TAIL, POOL_WIDTH), F32),
            pltpu.VMEM((1, LRU_WIDTH), F32),
        ],
        compiler_params=_cparams(2),
        name="mixer_prompt",
    )(r3, r3, r3, pw["conv_w"], pw["conv_b"], pw["w_gate"], pw["gate_a_b"], pw["gate_x_b"], pw["lam"],
      pw["w_pool"], pw["pool_scale"])


def _mixer_sample_kernel(l, q_ref, kn_ref, vn_ref, xr_ref, gr_ref, zp_ref, ck_ref, cv_ref,
                         h0_ref, sc_ref, sp_ref, bias_ref, sink_ref,
                         cw_ref, cb_ref, wg_ref, ba_ref, bx_ref, lam_ref, wp_ref, ps_ref,
                         attn_ref, rp_ref, h_ref):
    bt = SAMPLE_BT
    lo3 = lax.broadcasted_iota(jnp.int32, (bt, GQA_GROUP, LANES), 2) < HEAD_DIM
    q4 = q_ref[...].astype(F32)
    qm = jnp.concatenate([jnp.where(lo3, q4, 0.0), jnp.where(lo3, 0.0, q4)], axis=1)
    s = jnp.einsum("bqc,bck->bqk", qm.astype(BF16), ck_ref[...].astype(BF16),
                   preferred_element_type=F32) + bias_ref[...]
    s_new = jnp.sum(qm * kn_ref[...][:, None, :], axis=2, keepdims=True)
    sink = sink_ref[:, l:l + 1]
    m = jnp.maximum(jnp.maximum(jnp.max(s, axis=2, keepdims=True), s_new), sink)
    p = jnp.exp(s - m)
    p_new = jnp.exp(s_new - m)
    denom = jnp.sum(p, axis=2, keepdims=True) + p_new + jnp.exp(sink - m)
    o = jnp.einsum("bqk,bck->bqc", p.astype(BF16), cv_ref[...].astype(BF16),
                   preferred_element_type=F32)
    o = (o + p_new * vn_ref[...][:, None, :]) / denom
    attn_ref[...] = jnp.where(lo3, o[:, :GQA_GROUP, :], o[:, GQA_GROUP:, :]).astype(attn_ref.dtype)

    xr = xr_ref[...]
    cw = cw_ref[...]
    xc = cb_ref[l:l + 1, :] + xr * cw[CONV_W - 1:CONV_W, :]
    for tap in range(CONV_W - 1):
        xc = xc + sc_ref[tap] * cw[tap:tap + 1, :]
    a, b = _gates(l, xc, wg_ref, ba_ref, bx_ref, lam_ref)
    h = a * h0_ref[...] + b
    h_ref[...] = h
    rec = h * _gelu_tanh(gr_ref[...])

    z = zp_ref[...]
    lo = lax.broadcasted_iota(jnp.int32, (bt, LANES), 1) < POOL_GROUP_W
    means = []
    for c in range(POOL_WIDTH // LANES):
        w_small, w_big = POOL_WINDOWS[2 * c], POOL_WINDOWS[2 * c + 1]
        cols = slice(c * LANES, (c + 1) * LANES)
        zc = z[:, cols]
        acc = zc
        small = None
        for d in range(1, w_big):
            acc = acc + sp_ref[POOL_CTX - d][:, cols]
            if d + 1 == w_small:
                small = acc
        win = jnp.where(lo, small, acc)
        count = jnp.where(lo, float(w_small), float(w_big))
        means.append(win / count - zc)
    diff = jnp.concatenate(means, axis=1).astype(BF16)
    pool = jnp.dot(diff, wp_ref[...], preferred_element_type=F32) * ps_ref[l:l + 1, :]
    rp_ref[...] = jnp.concatenate([rec, pool], axis=1).astype(rp_ref.dtype)


def mixer_sample(l, q4, rs, ck_t, cv_t, state_h, state_conv_t, state_pool_t, bias_s, sinks_t, pw):
    n = rs.shape[0]
    bt = SAMPLE_BT
    cache_spec = pl.BlockSpec((None, bt, KV_WIDTH, WINDOW), lambda i: (l, i, 0, 0))
    rcol = lambda width, c: pl.BlockSpec((bt, width), lambda i: (i, c))
    return pl.pallas_call(
        functools.partial(_mixer_sample_kernel, l),
        grid=(n // bt,),
        in_specs=[
            pl.BlockSpec((bt, GQA_GROUP, LANES), lambda i: (i, 0, 0)),
            rcol(KV_WIDTH, K_BLK), rcol(KV_WIDTH, V_BLK),
            rcol(LRU_WIDTH, XR_BLK), rcol(LRU_WIDTH, GR_BLK), rcol(POOL_WIDTH, ZP_BLK),
            cache_spec, cache_spec,
            pl.BlockSpec((None, bt, LRU_WIDTH), lambda i: (l, i, 0)),
            pl.BlockSpec((None, CONV_W - 1, bt, LRU_WIDTH), lambda i: (l, 0, i, 0)),
            pl.BlockSpec((None, POOL_CTX, bt, POOL_WIDTH), lambda i: (l, 0, i, 0)),
            _whole_spec(bias_s.shape),
            _whole_spec(sinks_t.shape),
            _layer_spec(pw["conv_w"].shape, l), _whole_spec(pw["conv_b"].shape),
            _layer_spec(pw["w_gate"].shape, l), _whole_spec(pw["gate_a_b"].shape),
            _whole_spec(pw["gate_x_b"].shape), _whole_spec(pw["lam"].shape),
            _layer_spec(pw["w_pool"].shape, l), _whole_spec(pw["pool_scale"].shape),
        ],
        out_specs=[
            pl.BlockSpec((bt, GQA_GROUP, LANES), lambda i: (i, 0, 0)),
            pl.BlockSpec((bt, LRU_WIDTH + POOL_WIDTH), lambda i: (i, 0)),
            pl.BlockSpec((bt, LRU_WIDTH), lambda i: (i, 0)),
        ],
        out_shape=[
            jax.ShapeDtypeStruct((n, GQA_GROUP, LANES), BF16),
            jax.ShapeDtypeStruct((n, LRU_WIDTH + POOL_WIDTH), BF16),
            jax.ShapeDtypeStruct((n, LRU_WIDTH), F32),
        ],
        compiler_params=_cparams(1),
        name="mixer_sample",
    )(q4, rs, rs, rs, rs, rs, ck_t, cv_t, state_h, state_conv_t, state_pool_t, bias_s, sinks_t,
      pw["conv_w"], pw["conv_b"], pw["w_gate"], pw["gate_a_b"], pw["gate_x_b"], pw["lam"],
      pw["w_pool"], pw["pool_scale"])


def _cache_update_kernel(ck_ref, cv_ref, kn_ref, vn_ref, ok_ref, ov_ref):
    bt = CACHE_BT
    last = lax.broadcasted_iota(jnp.int32, (KV_WIDTH, WINDOW), 1) == WINDOW - 1
    pad = jnp.zeros((LANES - bt, KV_WIDTH), F32)
    for src, new, dst in ((ck_ref, kn_ref, ok_ref), (cv_ref, vn_ref, ov_ref)):
        new_t = jnp.concatenate([new[...], pad], axis=0).T
        for s in range(bt):
            shifted = pltpu.roll(src[s], WINDOW - 1, 1)
            col = pltpu.roll(new_t, WINDOW - 1 - s, 1)
            dst[s] = jnp.where(last, col, shifted)


def cache_update(ck_t, cv_t, k_new, v_new):
    depth, n = k_new.shape[:2]
    bt = CACHE_BT
    cache_spec = pl.BlockSpec((None, bt, KV_WIDTH, WINDOW), lambda l, i: (l, i, 0, 0))
    new_spec = pl.BlockSpec((None, bt, KV_WIDTH), lambda l, i: (l, i, 0))
    shape = jax.ShapeDtypeStruct(ck_t.shape, F32)
    return pl.pallas_call(
        _cache_update_kernel,
        grid=(depth, n // bt),
        in_specs=[cache_spec, cache_spec, new_spec, new_spec],
        out_specs=[cache_spec, cache_spec],
        out_shape=[shape, shape],
        compiler_params=_cparams(2),
        name="cache_update",
    )(ck_t, cv_t, k_new, v_new)


def _out_ffn_kernel(l, x_ref, at_ref, rp_ref, wo_ref, g1_ref, b1_ref,
                    w1_ref, w2_ref, g2_ref, b2_ref, y_ref, acc_ref):
    tm = x_ref.shape[0]
    sub = min(FFN_SUB, tm)
    halves = [slice(h * sub, (h + 1) * sub) for h in range(tm // sub)]
    mixes = []
    for rows in halves:
        mix = jnp.dot(at_ref[rows, :], wo_ref[:ATTN_WIDTH, :], preferred_element_type=F32)
        mixes.append(mix + jnp.dot(rp_ref[rows, :], wo_ref[ATTN_WIDTH:, :], preferred_element_type=F32))
    x1s = [_layer_norm(ALPHA * x_ref[rows, :] + mix, g1_ref[l:l + 1, :], b1_ref[l:l + 1, :])
           for rows, mix in zip(halves, mixes)]
    x1bs = [x1.astype(BF16) for x1 in x1s]
    for c in range(D_FF // FFN_FC):
        cols = slice(c * FFN_FC, (c + 1) * FFN_FC)
        for rows, x1b in zip(halves, x1bs):
            hid = jnp.dot(x1b, w1_ref[:, cols], preferred_element_type=F32)
            hid = jnp.square(jnp.maximum(hid, 0.0)).astype(BF16)
            part = jnp.dot(hid, w2_ref[cols, :], preferred_element_type=F32)
            if c == 0:
                acc_ref[rows, :] = part
            else:
                acc_ref[rows, :] += part
    for rows, x1 in zip(halves, x1s):
        y_ref[rows, :] = _layer_norm(ALPHA * x1 + acc_ref[rows, :], g2_ref[l:l + 1, :], b2_ref[l:l + 1, :])


def out_ffn(l, x2d, attn2d, rp2d, fw, tm):
    n = x2d.shape[0]
    row = lambda width: pl.BlockSpec((tm, width), lambda i: (i, 0))
    vec = _whole_spec((DEPTH, D_MODEL))
    return pl.pallas_call(
        functools.partial(_out_ffn_kernel, l),
        grid=(n // tm,),
        in_specs=[
            row(D_MODEL), row(ATTN_WIDTH), row(LRU_WIDTH + POOL_WIDTH),
            _layer_spec(fw["w_out"].shape, l), vec, vec,
            _layer_spec(fw["w_ff1"].shape, l), _layer_spec(fw["w_ff2"].shape, l), vec, vec,
        ],
        out_specs=row(D_MODEL),
        out_shape=jax.ShapeDtypeStruct((n, D_MODEL), F32),
        scratch_shapes=[pltpu.VMEM((tm, D_MODEL), F32)],
        compiler_params=_cparams(1),
        name="out_ffn",
    )(x2d, attn2d, rp2d, fw["w_out"], fw["ln1_g"], fw["ln1_b"],
      fw["w_ff1"], fw["w_ff2"], fw["ln2_g"], fw["ln2_b"])


def _block_diag(w):
    depth, g, c, d = w.shape
    eye = jnp.eye(g, dtype=bool)[None, :, None, :, None]
    return jnp.where(eye, w[:, :, :, None, :], 0.0).reshape(depth, g * c, g * d)


def _alibi_slopes():
    return jnp.exp2(-8.0 * (jnp.arange(N_HEADS, dtype=F32) + 1.0) / N_HEADS)


def _prompt_bias_tables():
    slopes = _alibi_slopes()
    jk = jnp.arange(2 * WINDOW)[:, None]
    tq = jnp.arange(WINDOW)[None, :]
    delta = tq + WINDOW - jk
    visible = (delta >= 0) & (delta <= WINDOW)
    bias = -slopes[None, :, None] * delta.astype(F32)[:, None, :]
    full = jnp.where(visible[:, None, :], bias, NEG_INF)
    first = jnp.where((visible & (jk >= WINDOW))[:, None, :], bias, NEG_INF)
    return jnp.stack([first, full]).reshape(2, 2 * WINDOW, N_HEADS * WINDOW)


def kernel(x_prompt, x_sample, cache_k, cache_v, state_h, state_conv, state_pool, w_in, attn_sinks, conv_w, conv_b, gate_a_w, gate_a_b, gate_x_w, gate_x_b, lru_lambda, pool_w, pool_scale, w_out, ln1_g, ln1_b, w_ff1, w_ff2, ln2_g, ln2_b):
    batch, seq, _ = x_prompt.shape
    dec = x_sample.shape[0]
    bias_prompt = _prompt_bias_tables()
    bias_sample = -_alibi_slopes()[:, None] * (WINDOW - jnp.arange(WINDOW, dtype=F32))[None, :]
    sink_rows = jnp.repeat(attn_sinks, WINDOW, axis=1)
    sinks_t = attn_sinks.T

    pw = {
        "conv_w": conv_w, "conv_b": conv_b,
        "w_gate": jnp.concatenate([_block_diag(gate_a_w), _block_diag(gate_x_w)], axis=2).astype(BF16),
        "gate_a_b": gate_a_b, "gate_x_b": gate_x_b, "lam": lru_lambda,
        "w_pool": _block_diag(pool_w).astype(BF16), "pool_scale": pool_scale,
    }
    fw = {
        "w_out": w_out.astype(BF16), "w_ff1": w_ff1.astype(BF16), "w_ff2": w_ff2.astype(BF16),
        "ln1_g": ln1_g, "ln1_b": ln1_b, "ln2_g": ln2_g, "ln2_b": ln2_b,
    }

    def kv_view(c):
        return jnp.transpose(c, (0, 1, 3, 4, 2)).reshape(c.shape[0], c.shape[1], KV_WIDTH, WINDOW)

    def kv_unview(c_t):
        c5 = c_t.reshape(c_t.shape[0], c_t.shape[1], N_KV_HEADS, HEAD_DIM, WINDOW)
        return jnp.transpose(c5, (0, 1, 4, 2, 3))

    ck_t, cv_t = kv_view(cache_k), kv_view(cache_v)
    state_conv_t = jnp.swapaxes(state_conv, 1, 2)
    state_pool_t = jnp.swapaxes(state_pool, 1, 2)

    yp = x_prompt.reshape(batch * seq, D_MODEL)
    ys = x_sample.reshape(dec, D_MODEL)
    xr_cols = slice(XR_BLK * LRU_WIDTH, (XR_BLK + 1) * LRU_WIDTH)
    zp_cols = slice(ZP_BLK * LRU_WIDTH, (ZP_BLK + 1) * LRU_WIDTH)
    outs = {k: [] for k in ("pk", "pv", "ph", "pc", "pp", "kn", "vn", "sh", "sc", "sp")}
    for l in range(DEPTH):
        q2, r2 = in_proj(l, yp, w_in, PROJ_TM)
        r3 = r2.reshape(batch, seq, REST_WIDTH)
        attn, k_last_t, v_last_t = attn_prompt(l, q2.reshape(batch, seq, ATTN_WIDTH), r3, bias_prompt, sink_rows)
        rp, h_last = mixer_prompt(l, r3, pw)
        yp = out_ffn(l, yp, attn.reshape(batch * seq, ATTN_WIDTH), rp.reshape(batch * seq, -1), fw, FFN_TM)
        outs["pk"].append(k_last_t)
        outs["pv"].append(v_last_t)
        outs["ph"].append(h_last.reshape(batch, LRU_WIDTH))
        outs["pc"].append(r3[:, seq - (CONV_W - 1):, xr_cols])
        outs["pp"].append(r3[:, seq - POOL_CTX:, zp_cols])

        qs, rs = in_proj(l, ys, w_in, dec)
        attn_s, rp_s, h_s = mixer_sample(
            l, qs.reshape(dec, GQA_GROUP, LANES), rs, ck_t, cv_t, state_h, state_conv_t, state_pool_t,
            bias_sample, sinks_t, pw)
        attn_s = attn_s.reshape(dec, N_HEADS, HEAD_DIM)[:, HEAD_ORDER_INV, :].reshape(dec, ATTN_WIDTH)
        ys = out_ffn(l, ys, attn_s, rp_s, fw, dec)
        outs["kn"].append(rs[:, K_BLK * KV_WIDTH:(K_BLK + 1) * KV_WIDTH])
        outs["vn"].append(rs[:, V_BLK * KV_WIDTH:(V_BLK + 1) * KV_WIDTH])
        outs["sh"].append(h_s)
        outs["sc"].append(jnp.concatenate([state_conv[l][:, 1:], rs[:, None, xr_cols]], axis=1))
        outs["sp"].append(jnp.concatenate([state_pool[l][:, 1:], rs[:, None, zp_cols]], axis=1))

    st = {k: jnp.stack(v) for k, v in outs.items()}
    sk_t, sv_t = cache_update(ck_t, cv_t, st["kn"], st["vn"])
    return (yp.reshape(batch, seq, D_MODEL), ys.reshape(dec, 1, D_MODEL),
            kv_unview(st["pk"]), kv_unview(st["pv"]), st["ph"], st["pc"], st["pp"],
            kv_unview(sk_t), kv_unview(sv_t), st["sh"], st["sc"], st["sp"])
```

```python
import functools

import jax
import jax.numpy as jnp
import numpy as np
from jax import lax
from jax.experimental import pallas as pl
from jax.experimental.pallas import tpu as pltpu

D_MODEL = 1024
DEPTH = 2
HEAD_DIM = 64
ATTN_WIDTH = 512
N_HEADS = 8
N_KV_HEADS = 2
GQA_GROUP = 4
KV_WIDTH = 128
WINDOW = 128
LRU_WIDTH = 256
LRU_C = 8.0
CONV_W = 4
POOL_WINDOWS = (2, 4, 8, 16)
POOL_WIDTH = 256
POOL_GROUP_W = 64
POOL_CTX = 15
IN_WIDTH = 1536
REST_WIDTH = IN_WIDTH - ATTN_WIDTH
D_FF = 4096
LN_EPS = 1e-5
NEG_INF = -1e30
ALPHA = (2.0 * DEPTH) ** 0.25
Q_SCALE = HEAD_DIM ** -0.5

K_BLK, V_BLK = 0, 1
XR_BLK, GR_BLK, ZP_BLK = 1, 2, 3

LANES = 128
SUBLANES = 8
VMEM_LIMIT_BYTES = 56 * 1024 * 1024

FRONT_T = 512
FFN_TM = 1024
FFN_FC = 1024
FFN_SUB = 256
X_TAIL = SUBLANES
Z_TAIL = 2 * SUBLANES
SAMPLE_BT = 16
CACHE_BT = 32

BF16 = jnp.bfloat16
F32 = jnp.float32

HEAD_ORDER = (0, 4, 1, 5, 2, 6, 3, 7)
HEAD_ORDER_INV = tuple(int(i) for i in np.argsort(HEAD_ORDER))


def _cparams(n_grid):
    return pltpu.CompilerParams(
        dimension_semantics=("arbitrary",) * n_grid,
        vmem_limit_bytes=VMEM_LIMIT_BYTES,
    )


def _whole_spec(shape):
    nd = len(shape)
    return pl.BlockSpec(shape, lambda *_: (0,) * nd, pipeline_mode=pl.Buffered(1))


def _layer_spec(shape, l):
    nd = len(shape) - 1
    return pl.BlockSpec((None,) + tuple(shape[1:]), lambda *_: (l,) + (0,) * nd, pipeline_mode=pl.Buffered(1))


def _layer_norm(x, g, b):
    mu = jnp.mean(x, axis=-1, keepdims=True)
    xc = x - mu
    var = jnp.mean(xc * xc, axis=-1, keepdims=True)
    return xc * lax.rsqrt(var + LN_EPS) * g + b


def _gelu_tanh(x):
    return 0.5 * x * (1.0 + jnp.tanh(np.sqrt(2.0 / np.pi) * (x + 0.044715 * (x * x * x))))


def _sigmoid(x):
    return 1.0 / (1.0 + jnp.exp(-x))


def _softplus(x):
    return jnp.maximum(x, 0.0) + jnp.log(1.0 + jnp.exp(-jnp.abs(x)))


def _prep_w_in(w_ref, wb_ref):
    lo = lax.broadcasted_iota(jnp.int32, (D_MODEL, LANES), 1) < HEAD_DIM
    src = [w_ref[:, s * LANES:(s + 1) * LANES] * Q_SCALE for s in range(ATTN_WIDTH // LANES)]
    swapped = [pltpu.roll(t, HEAD_DIM, 1) for t in src]
    for c in range(GQA_GROUP):
        s0, s1 = c // 2, GQA_GROUP // 2 + c // 2
        if c % 2 == 0:
            tile = jnp.where(lo, src[s0], swapped[s1])
        else:
            tile = jnp.where(lo, swapped[s0], src[s1])
        wb_ref[:, c * LANES:(c + 1) * LANES] = tile.astype(BF16)
    wb_ref[:, ATTN_WIDTH:] = w_ref[:, ATTN_WIDTH:].astype(BF16)


def _in_proj_kernel(x_ref, w_ref, q_ref, r_ref, wb_ref):
    @pl.when(pl.program_id(0) == 0)
    def _():
        _prep_w_in(w_ref, wb_ref)

    u = jnp.dot(x_ref[...].astype(BF16), wb_ref[...], preferred_element_type=F32)
    q_ref[...] = u[:, :ATTN_WIDTH].astype(BF16)
    r_ref[...] = u[:, ATTN_WIDTH:]


def in_proj(l, x2d, w_in, tm):
    n = x2d.shape[0]
    return pl.pallas_call(
        _in_proj_kernel,
        grid=(n // tm,),
        in_specs=[
            pl.BlockSpec((tm, D_MODEL), lambda i: (i, 0)),
            _layer_spec(w_in.shape, l),
        ],
        out_specs=[
            pl.BlockSpec((tm, ATTN_WIDTH), lambda i: (i, 0)),
            pl.BlockSpec((tm, REST_WIDTH), lambda i: (i, 0)),
        ],
        out_shape=[
            jax.ShapeDtypeStruct((n, ATTN_WIDTH), BF16),
            jax.ShapeDtypeStruct((n, REST_WIDTH), F32),
        ],
        scratch_shapes=[pltpu.VMEM((D_MODEL, IN_WIDTH), BF16)],
        compiler_params=_cparams(1),
        name="in_proj",
    )(x2d, w_in)


def _attn_block(q, k2, v2, bias_t, sink, lo):
    zero = jnp.zeros((), q.dtype)
    tiles = [q[:, c * LANES:(c + 1) * LANES] for c in range(GQA_GROUP)]
    qs = jnp.concatenate([jnp.where(lo, t, zero) for t in tiles]
                         + [jnp.where(lo, zero, t) for t in tiles], axis=0)
    s = lax.dot_general(k2, qs, (((1,), (1,)), ((), ())), preferred_element_type=F32) + bias_t
    m = jnp.maximum(jnp.max(s, axis=0, keepdims=True), sink)
    p = jnp.exp(s - m)
    denom = jnp.sum(p, axis=0, keepdims=True) + jnp.exp(sink - m)
    o = lax.dot_general(v2, p.astype(BF16), (((0,), (0,)), ((), ())), preferred_element_type=F32)
    o = o * (1.0 / denom)
    cols = []
    for c in range(ATTN_WIDTH // LANES):
        kv = (2 * c) // GQA_GROUP
        rows = slice(kv * HEAD_DIM, (kv + 1) * HEAD_DIM)
        blk = jnp.concatenate([o[rows, (2 * c) * WINDOW:(2 * c + 1) * WINDOW],
                               o[rows, (2 * c + 1) * WINDOW:(2 * c + 2) * WINDOW]], axis=0)
        cols.append(blk.T)
    return jnp.concatenate(cols, axis=1)


def _gates(l, xc, wg_ref, ba_ref, bx_ref, lam_ref):
    g = jnp.dot(xc.astype(BF16), wg_ref[...], preferred_element_type=F32)
    r = _sigmoid(g[:, :LRU_WIDTH] + ba_ref[l:l + 1, :])
    i = _sigmoid(g[:, LRU_WIDTH:] + bx_ref[l:l + 1, :])
    log_a = (-LRU_C * r) * _softplus(-lam_ref[l:l + 1, :])
    a = jnp.exp(log_a)
    b = jnp.sqrt(1.0 - a * a) * (i * xc)
    return a, b


def _lru_scan(a, b, h_prev):
    t, w = a.shape
    groups = t // SUBLANES
    a3 = a.reshape(groups, SUBLANES, w)
    b3 = b.reshape(groups, SUBLANES, w)
    row = lax.broadcasted_iota(jnp.int32, (groups, SUBLANES, w), 1)
    s = 1
    while s < SUBLANES:
        keep = row >= s
        a_sh = jnp.where(keep, pltpu.roll(a3, s, 1), 1.0)
        b_sh = jnp.where(keep, pltpu.roll(b3, s, 1), 0.0)
        b3 = a3 * b_sh + b3
        a3 = a3 * a_sh
        s *= 2
    hs = []
    h = h_prev
    for g in range(groups):
        hg = a3[g] * h + b3[g]
        hs.append(hg)
        h = hg[SUBLANES - 1:SUBLANES, :]
    return jnp.concatenate(hs, axis=0), h


def _shift_rows(x, d):
    return pltpu.roll(x, d, 0)


def _pool_means(ze, pos1):
    t = ze.shape[0] - Z_TAIL
    lo = lax.broadcasted_iota(jnp.int32, (t, LANES), 1) < POOL_GROUP_W
    s2 = ze + _shift_rows(ze, 1)
    s4 = s2 + _shift_rows(s2, 2)
    hi4 = s4[:, LANES:]
    s8 = hi4 + _shift_rows(hi4, 4)
    s16 = s8 + _shift_rows(s8, 8)
    wins = (jnp.where(lo, s2[Z_TAIL:, :LANES], s4[Z_TAIL:, :LANES]),
            jnp.where(lo, s8[Z_TAIL:], s16[Z_TAIL:]))
    means = []
    for c, win in enumerate(wins):
        w_small, w_big = POOL_WINDOWS[2 * c], POOL_WINDOWS[2 * c + 1]
        count = jnp.where(lo, float(w_small), float(w_big))
        if pos1 is not None:
            count = jnp.minimum(pos1, count)
        means.append(win / count - ze[Z_TAIL:, c * LANES:(c + 1) * LANES])
    return jnp.concatenate(means, axis=1)


def _mixer_rows(l, xr, gr, zp, x_tail, z_tail, h_prev, pos1,
                cw_ref, cb_ref, wg_ref, ba_ref, bx_ref, lam_ref, wp_ref, ps_ref):
    xe = jnp.concatenate([x_tail, xr], axis=0)
    ze = jnp.concatenate([z_tail, zp], axis=0)
    cw = cw_ref[...]
    xc = cb_ref[l:l + 1, :] + xr * cw[CONV_W - 1:CONV_W, :]
    for tap in range(CONV_W - 1):
        d = CONV_W - 1 - tap
        xc = xc + _shift_rows(xe, d)[X_TAIL:] * cw[tap:tap + 1, :]
    a, b = _gates(l, xc, wg_ref, ba_ref, bx_ref, lam_ref)
    hs, h_last = _lru_scan(a, b, h_prev)
    rec = hs * _gelu_tanh(gr)
    diff = _pool_means(ze, pos1).astype(BF16)
    pool = jnp.dot(diff, wp_ref[...], preferred_element_type=F32) * ps_ref[l:l + 1, :]
    return rec, pool, h_last


def _front_kernel(l, tiles_per_seq, x_ref, w_ref, bias0_ref, bias_ref, sink_ref,
                  cw_ref, cb_ref, wg_ref, ba_ref, bx_ref, lam_ref, wp_ref, ps_ref,
                  attn_ref, rp_ref, h_ref, kt_ref, vt_ref, xt_ref, zt_ref,
                  wb_ref, q_scr, r_scr, kprev, vprev, xtail, ztail, hcar):
    s = pl.program_id(0)
    nxt = lax.rem(s, 2)
    cur = 1 - nxt
    j = lax.rem(s + tiles_per_seq - 1, tiles_per_seq)

    @pl.when(s == 0)
    def _():
        _prep_w_in(w_ref, wb_ref)
        q_scr[1] = jnp.zeros(q_scr.shape[1:], q_scr.dtype)
        r_scr[1] = jnp.zeros(r_scr.shape[1:], r_scr.dtype)
        kprev[...] = jnp.zeros_like(kprev)
        vprev[...] = jnp.zeros_like(vprev)

    @pl.when((j == 0) | (s == 0))
    def _():
        xtail[...] = jnp.zeros_like(xtail)
        ztail[...] = jnp.zeros_like(ztail)
        hcar[...] = jnp.zeros_like(hcar)

    lo = lax.broadcasted_iota(jnp.int32, (WINDOW, LANES), 1) < HEAD_DIM
    sink = sink_ref[l:l + 1, :]
    xb = x_ref[...].astype(BF16)
    k_prev, v_prev = kprev[...], vprev[...]
    x_tail, z_tail, h = xtail[...], ztail[...], hcar[...]
    k_last = v_last = None
    proj_cols = ((0, ATTN_WIDTH), (ATTN_WIDTH, ATTN_WIDTH + 2 * KV_WIDTH),
                 (ATTN_WIDTH + 2 * KV_WIDTH, ATTN_WIDTH + 2 * KV_WIDTH + 2 * LRU_WIDTH),
                 (ATTN_WIDTH + 2 * KV_WIDTH + 2 * LRU_WIDTH, IN_WIDTH))
    for c in range(FRONT_T // WINDOW):
        c0, c1 = proj_cols[c]
        u = jnp.dot(xb, wb_ref[:, c0:c1], preferred_element_type=F32)
        if c == 0:
            q_scr[nxt] = u.astype(BF16)
        else:
            r_scr[nxt, :, c0 - ATTN_WIDTH:c1 - ATTN_WIDTH] = u

        rows = slice(c * WINDOW, (c + 1) * WINDOW)
        k = r_scr[cur, rows, K_BLK * KV_WIDTH:(K_BLK + 1) * KV_WIDTH]
        v = r_scr[cur, rows, V_BLK * KV_WIDTH:(V_BLK + 1) * KV_WIDTH]
        kb, vb = k.astype(BF16), v.astype(BF16)
        bias_t = bias0_ref[...] if c == 0 else bias_ref[...]
        attn = _attn_block(q_scr[cur, rows, :], jnp.concatenate([k_prev, kb], axis=0),
                           jnp.concatenate([v_prev, vb], axis=0), bias_t, sink, lo)
        attn_ref[rows, :] = attn.astype(attn_ref.dtype)
        k_prev, v_prev, k_last, v_last = kb, vb, k, v

        xr = r_scr[cur, rows, XR_BLK * LRU_WIDTH:(XR_BLK + 1) * LRU_WIDTH]
        gr = r_scr[cur, rows, GR_BLK * LRU_WIDTH:(GR_BLK + 1) * LRU_WIDTH]
        zp = r_scr[cur, rows, ZP_BLK * LRU_WIDTH:(ZP_BLK + 1) * LRU_WIDTH]
        pos1 = (j * FRONT_T + c * WINDOW + lax.broadcasted_iota(jnp.int32, (WINDOW, LANES), 0) + 1).astype(F32)
        rec, pool, h = _mixer_rows(l, xr, gr, zp, x_tail, z_tail, h, pos1,
                                   cw_ref, cb_ref, wg_ref, ba_ref, bx_ref, lam_ref, wp_ref, ps_ref)
        rp_ref[rows, :] = jnp.concatenate([rec, pool], axis=1).astype(rp_ref.dtype)
        x_tail, z_tail = xr[WINDOW - X_TAIL:, :], zp[WINDOW - Z_TAIL:, :]

    kprev[...] = k_prev
    vprev[...] = v_prev
    xtail[...] = x_tail
    ztail[...] = z_tail
    hcar[...] = h

    @pl.when(j == tiles_per_seq - 1)
    def _():
        kt_ref[...] = k_last.T
        vt_ref[...] = v_last.T
        h_ref[...] = h
        xt_ref[...] = x_tail
        zt_ref[...] = z_tail


def front_prompt(l, x2d, seq, w_in, bias2, sink_rows, pw):
    n = x2d.shape[0]
    b = n // seq
    t = FRONT_T
    n_tiles = n // t
    tiles_per_seq = seq // t
    done = lambda s: jnp.maximum(s - 1, 0)
    seq_of = lambda s: done(s) // tiles_per_seq
    bias_shape = (None, 2 * WINDOW, N_HEADS * WINDOW)
    per_seq = lambda rows, width: pl.BlockSpec((None, rows, width), lambda s: (seq_of(s), 0, 0))
    return pl.pallas_call(
        functools.partial(_front_kernel, l, tiles_per_seq),
        grid=(n_tiles + 1,),
        in_specs=[
            pl.BlockSpec((t, D_MODEL), lambda s: (jnp.minimum(s, n_tiles - 1), 0)),
            _layer_spec(w_in.shape, l),
            pl.BlockSpec(bias_shape, lambda s: (jnp.minimum(done(s) % tiles_per_seq, 1), 0, 0)),
            pl.BlockSpec(bias_shape, lambda s: (1, 0, 0)),
            _whole_spec(sink_rows.shape),
            _layer_spec(pw["conv_w"].shape, l), _whole_spec(pw["conv_b"].shape),
            _layer_spec(pw["w_gate"].shape, l), _whole_spec(pw["gate_a_b"].shape),
            _whole_spec(pw["gate_x_b"].shape), _whole_spec(pw["lam"].shape),
            _layer_spec(pw["w_pool"].shape, l), _whole_spec(pw["pool_scale"].shape),
        ],
        out_specs=[
            pl.BlockSpec((t, ATTN_WIDTH), lambda s: (done(s), 0)),
            pl.BlockSpec((t, LRU_WIDTH + POOL_WIDTH), lambda s: (done(s), 0)),
            per_seq(1, LRU_WIDTH),
            per_seq(KV_WIDTH, WINDOW), per_seq(KV_WIDTH, WINDOW),
            per_seq(X_TAIL, LRU_WIDTH), per_seq(Z_TAIL, POOL_WIDTH),
        ],
        out_shape=[
            jax.ShapeDtypeStruct((n, ATTN_WIDTH), BF16),
            jax.ShapeDtypeStruct((n, LRU_WIDTH + POOL_WIDTH), BF16),
            jax.ShapeDtypeStruct((b, 1, LRU_WIDTH), F32),
            jax.ShapeDtypeStruct((b, KV_WIDTH, WINDOW), F32),
            jax.ShapeDtypeStruct((b, KV_WIDTH, WINDOW), F32),
            jax.ShapeDtypeStruct((b, X_TAIL, LRU_WIDTH), F32),
            jax.ShapeDtypeStruct((b, Z_TAIL, POOL_WIDTH), F32),
        ],
        scratch_shapes=[
            pltpu.VMEM((D_MODEL, IN_WIDTH), BF16),
            pltpu.VMEM((2, t, ATTN_WIDTH), BF16),
            pltpu.VMEM((2, t, REST_WIDTH), F32),
            pltpu.VMEM((WINDOW, KV_WIDTH), BF16),
            pltpu.VMEM((WINDOW, KV_WIDTH), BF16),
            pltpu.VMEM((X_TAIL, LRU_WIDTH), F32),
            pltpu.VMEM((Z_TAIL, POOL_WIDTH), F32),
            pltpu.VMEM((1, LRU_WIDTH), F32),
        ],
        compiler_params=_cparams(1),
        name="front_prompt",
    )(x2d, w_in, bias2, bias2, sink_rows, pw["conv_w"], pw["conv_b"], pw["w_gate"], pw["gate_a_b"],
      pw["gate_x_b"], pw["lam"], pw["w_pool"], pw["pool_scale"])


def _mixer_sample_kernel(l, q_ref, kn_ref, vn_ref, xr_ref, gr_ref, zp_ref, ck_ref, cv_ref,
                         h0_ref, sc_ref, sp_ref, bias_ref, sink_ref,
                         cw_ref, cb_ref, wg_ref, ba_ref, bx_ref, lam_ref, wp_ref, ps_ref,
                         attn_ref, rp_ref, h_ref):
    bt = SAMPLE_BT
    lo3 = lax.broadcasted_iota(jnp.int32, (bt, GQA_GROUP, LANES), 2) < HEAD_DIM
    q4 = q_ref[...].astype(F32)
    qm = jnp.concatenate([jnp.where(lo3, q4, 0.0), jnp.where(lo3, 0.0, q4)], axis=1)
    s = jnp.einsum("bqc,bck->bqk", qm.astype(BF16), ck_ref[...].astype(BF16),
                   preferred_element_type=F32) + bias_ref[...]
    s_new = jnp.sum(qm * kn_ref[...][:, None, :], axis=2, keepdims=True)
    sink = sink_ref[:, l:l + 1]
    m = jnp.maximum(jnp.maximum(jnp.max(s, axis=2, keepdims=True), s_new), sink)
    p = jnp.exp(s - m)
    p_new = jnp.exp(s_new - m)
    denom = jnp.sum(p, axis=2, keepdims=True) + p_new + jnp.exp(sink - m)
    o = jnp.einsum("bqk,bck->bqc", p.astype(BF16), cv_ref[...].astype(BF16),
                   preferred_element_type=F32)
    o = (o + p_new * vn_ref[...][:, None, :]) / denom
    attn_ref[...] = jnp.where(lo3, o[:, :GQA_GROUP, :], o[:, GQA_GROUP:, :]).astype(attn_ref.dtype)

    xr = xr_ref[...]
    cw = cw_ref[...]
    xc = cb_ref[l:l + 1, :] + xr * cw[CONV_W - 1:CONV_W, :]
    for tap in range(CONV_W - 1):
        xc = xc + sc_ref[tap] * cw[tap:tap + 1, :]
    a, b = _gates(l, xc, wg_ref, ba_ref, bx_ref, lam_ref)
    h = a * h0_ref[...] + b
    h_ref[...] = h
    rec = h * _gelu_tanh(gr_ref[...])

    z = zp_ref[...]
    lo = lax.broadcasted_iota(jnp.int32, (bt, LANES), 1) < POOL_GROUP_W
    means = []
    for c in range(POOL_WIDTH // LANES):
        w_small, w_big = POOL_WINDOWS[2 * c], POOL_WINDOWS[2 * c + 1]
        cols = slice(c * LANES, (c + 1) * LANES)
        zc = z[:, cols]
        acc = zc
        small = None
        for d in range(1, w_big):
            acc = acc + sp_ref[POOL_CTX - d][:, cols]
            if d + 1 == w_small:
                small = acc
        win = jnp.where(lo, small, acc)
        count = jnp.where(lo, float(w_small), float(w_big))
        means.append(win / count - zc)
    diff = jnp.concatenate(means, axis=1).astype(BF16)
    pool = jnp.dot(diff, wp_ref[...], preferred_element_type=F32) * ps_ref[l:l + 1, :]
    rp_ref[...] = jnp.concatenate([rec, pool], axis=1).astype(rp_ref.dtype)


def mixer_sample(l, q4, rs, ck_t, cv_t, state_h, state_conv_t, state_pool_t, bias_s, sinks_t, pw):
    n = rs.shape[0]
    bt = SAMPLE_BT
    cache_spec = pl.BlockSpec((None, bt, KV_WIDTH, WINDOW), lambda i: (l, i, 0, 0))
    rcol = lambda width, c: pl.BlockSpec((bt, width), lambda i: (i, c))
    return pl.pallas_call(
        functools.partial(_mixer_sample_kernel, l),
        grid=(n // bt,),
        in_specs=[
            pl.BlockSpec((bt, GQA_GROUP, LANES), lambda i: (i, 0, 0)),
            rcol(KV_WIDTH, K_BLK), rcol(KV_WIDTH, V_BLK),
            rcol(LRU_WIDTH, XR_BLK), rcol(LRU_WIDTH, GR_BLK), rcol(POOL_WIDTH, ZP_BLK),
            cache_spec, cache_spec,
            pl.BlockSpec((None, bt, LRU_WIDTH), lambda i: (l, i, 0)),
            pl.BlockSpec((None, CONV_W - 1, bt, LRU_WIDTH), lambda i: (l, 0, i, 0)),
            pl.BlockSpec((None, POOL_CTX, bt, POOL_WIDTH), lambda i: (l, 0, i, 0)),
            _whole_spec(bias_s.shape),
            _whole_spec(sinks_t.shape),
            _layer_spec(pw["conv_w"].shape, l), _whole_spec(pw["conv_b"].shape),
            _layer_spec(pw["w_gate"].shape, l), _whole_spec(pw["gate_a_b"].shape),
            _whole_spec(pw["gate_x_b"].shape), _whole_spec(pw["lam"].shape),
            _layer_spec(pw["w_pool"].shape, l), _whole_spec(pw["pool_scale"].shape),
        ],
        out_specs=[
            pl.BlockSpec((bt, GQA_GROUP, LANES), lambda i: (i, 0, 0)),
            pl.BlockSpec((bt, LRU_WIDTH + POOL_WIDTH), lambda i: (i, 0)),
            pl.BlockSpec((bt, LRU_WIDTH), lambda i: (i, 0)),
        ],
        out_shape=[
            jax.ShapeDtypeStruct((n, GQA_GROUP, LANES), BF16),
            jax.ShapeDtypeStruct((n, LRU_WIDTH + POOL_WIDTH), BF16),
            jax.ShapeDtypeStruct((n, LRU_WIDTH), F32),
        ],
        compiler_params=_cparams(1),
        name="mixer_sample",
    )(q4, rs, rs, rs, rs, rs, ck_t, cv_t, state_h, state_conv_t, state_pool_t, bias_s, sinks_t,
      pw["conv_w"], pw["conv_b"], pw["w_gate"], pw["gate_a_b"], pw["gate_x_b"], pw["lam"],
      pw["w_pool"], pw["pool_scale"])


def _cache_update_kernel(ck_ref, cv_ref, kn_ref, vn_ref, ok_ref, ov_ref):
    bt = CACHE_BT
    last = lax.broadcasted_iota(jnp.int32, (KV_WIDTH, WINDOW), 1) == WINDOW - 1
    pad = jnp.zeros((LANES - bt, KV_WIDTH), F32)
    for src, new, dst in ((ck_ref, kn_ref, ok_ref), (cv_ref, vn_ref, ov_ref)):
        new_t = jnp.concatenate([new[...], pad], axis=0).T
        for s in range(bt):
            shifted = pltpu.roll(src[s], WINDOW - 1, 1)
            col = pltpu.roll(new_t, WINDOW - 1 - s, 1)
            dst[s] = jnp.where(last, col, shifted)


def cache_update(ck_t, cv_t, k_new, v_new):
    depth, n = k_new.shape[:2]
    bt = CACHE_BT
    cache_spec = pl.BlockSpec((None, bt, KV_WIDTH, WINDOW), lambda l, i: (l, i, 0, 0))
    new_spec = pl.BlockSpec((None, bt, KV_WIDTH), lambda l, i: (l, i, 0))
    shape = jax.ShapeDtypeStruct(ck_t.shape, F32)
    return pl.pallas_call(
        _cache_update_kernel,
        grid=(depth, n // bt),
        in_specs=[cache_spec, cache_spec, new_spec, new_spec],
        out_specs=[cache_spec, cache_spec],
        out_shape=[shape, shape],
        compiler_params=_cparams(2),
        name="cache_update",
    )(ck_t, cv_t, k_new, v_new)


def _out_ffn_kernel(l, x_ref, at_ref, rp_ref, wo_ref, g1_ref, b1_ref,
                    w1_ref, w2_ref, g2_ref, b2_ref, y_ref, acc_ref):
    tm = x_ref.shape[0]
    sub = min(FFN_SUB, tm)
    halves = [slice(h * sub, (h + 1) * sub) for h in range(tm // sub)]
    mixes = []
    for rows in halves:
        mix = jnp.dot(at_ref[rows, :], wo_ref[:ATTN_WIDTH, :], preferred_element_type=F32)
        mixes.append(mix + jnp.dot(rp_ref[rows, :], wo_ref[ATTN_WIDTH:, :], preferred_element_type=F32))
    x1s = [_layer_norm(ALPHA * x_ref[rows, :] + mix, g1_ref[l:l + 1, :], b1_ref[l:l + 1, :])
           for rows, mix in zip(halves, mixes)]
    x1bs = [x1.astype(BF16) for x1 in x1s]
    for c in range(D_FF // FFN_FC):
        cols = slice(c * FFN_FC, (c + 1) * FFN_FC)
        for rows, x1b in zip(halves, x1bs):
            hid = jnp.dot(x1b, w1_ref[:, cols], preferred_element_type=F32)
            hid = jnp.square(jnp.maximum(hid, 0.0)).astype(BF16)
            part = jnp.dot(hid, w2_ref[cols, :], preferred_element_type=F32)
            if c == 0:
                acc_ref[rows, :] = part
            else:
                acc_ref[rows, :] += part
    for rows, x1 in zip(halves, x1s):
        y_ref[rows, :] = _layer_norm(ALPHA * x1 + acc_ref[rows, :], g2_ref[l:l + 1, :], b2_ref[l:l + 1, :])


def out_ffn(l, x2d, attn2d, rp2d, fw, tm):
    n = x2d.shape[0]
    row = lambda width: pl.BlockSpec((tm, width), lambda i: (i, 0))
    vec = _whole_spec((DEPTH, D_MODEL))
    return pl.pallas_call(
        functools.partial(_out_ffn_kernel, l),
        grid=(n // tm,),
        in_specs=[
            row(D_MODEL), row(ATTN_WIDTH), row(LRU_WIDTH + POOL_WIDTH),
            _layer_spec(fw["w_out"].shape, l), vec, vec,
            _layer_spec(fw["w_ff1"].shape, l), _layer_spec(fw["w_ff2"].shape, l), vec, vec,
        ],
        out_specs=row(D_MODEL),
        out_shape=jax.ShapeDtypeStruct((n, D_MODEL), F32),
        scratch_shapes=[pltpu.VMEM((tm, D_MODEL), F32)],
        compiler_params=_cparams(1),
        name="out_ffn",
    )(x2d, attn2d, rp2d, fw["w_out"], fw["ln1_g"], fw["ln1_b"],
      fw["w_ff1"], fw["w_ff2"], fw["ln2_g"], fw["ln2_b"])


def _block_diag(w):
    depth, g, c, d = w.shape
    eye = jnp.eye(g, dtype=bool)[None, :, None, :, None]
    return jnp.where(eye, w[:, :, :, None, :], 0.0).reshape(depth, g * c, g * d)


def _alibi_slopes():
    return jnp.exp2(-8.0 * (jnp.arange(N_HEADS, dtype=F32) + 1.0) / N_HEADS)


def _prompt_bias_tables():
    slopes = _alibi_slopes()
    jk = jnp.arange(2 * WINDOW)[:, None]
    tq = jnp.arange(WINDOW)[None, :]
    delta = tq + WINDOW - jk
    visible = (delta >= 0) & (delta <= WINDOW)
    bias = -slopes[None, :, None] * delta.astype(F32)[:, None, :]
    full = jnp.where(visible[:, None, :], bias, NEG_INF)
    first = jnp.where((visible & (jk >= WINDOW))[:, None, :], bias, NEG_INF)
    return jnp.stack([first, full]).reshape(2, 2 * WINDOW, N_HEADS * WINDOW)


def kernel(x_prompt, x_sample, cache_k, cache_v, state_h, state_conv, state_pool, w_in, attn_sinks, conv_w, conv_b, gate_a_w, gate_a_b, gate_x_w, gate_x_b, lru_lambda, pool_w, pool_scale, w_out, ln1_g, ln1_b, w_ff1, w_ff2, ln2_g, ln2_b):
    batch, seq, _ = x_prompt.shape
    dec = x_sample.shape[0]
    bias_prompt = _prompt_bias_tables()
    bias_sample = -_alibi_slopes()[:, None] * (WINDOW - jnp.arange(WINDOW, dtype=F32))[None, :]
    sink_rows = jnp.repeat(attn_sinks, WINDOW, axis=1)
    sinks_t = attn_sinks.T

    pw = {
        "conv_w": conv_w, "conv_b": conv_b,
        "w_gate": jnp.concatenate([_block_diag(gate_a_w), _block_diag(gate_x_w)], axis=2).astype(BF16),
        "gate_a_b": gate_a_b, "gate_x_b": gate_x_b, "lam": lru_lambda,
        "w_pool": _block_diag(pool_w).astype(BF16), "pool_scale": pool_scale,
    }
    fw = {
        "w_out": w_out.astype(BF16), "w_ff1": w_ff1.astype(BF16), "w_ff2": w_ff2.astype(BF16),
        "ln1_g": ln1_g, "ln1_b": ln1_b, "ln2_g": ln2_g, "ln2_b": ln2_b,
    }

    def kv_view(c):
        return jnp.transpose(c, (0, 1, 3, 4, 2)).reshape(c.shape[0], c.shape[1], KV_WIDTH, WINDOW)

    def kv_unview(c_t):
        c5 = c_t.reshape(c_t.shape[0], c_t.shape[1], N_KV_HEADS, HEAD_DIM, WINDOW)
        return jnp.transpose(c5, (0, 1, 4, 2, 3))

    ck_t, cv_t = kv_view(cache_k), kv_view(cache_v)
    state_conv_t = jnp.swapaxes(state_conv, 1, 2)
    state_pool_t = jnp.swapaxes(state_pool, 1, 2)

    yp = x_prompt.reshape(batch * seq, D_MODEL)
    ys = x_sample.reshape(dec, D_MODEL)
    xr_cols = slice(XR_BLK * LRU_WIDTH, (XR_BLK + 1) * LRU_WIDTH)
    zp_cols = slice(ZP_BLK * LRU_WIDTH, (ZP_BLK + 1) * LRU_WIDTH)
    outs = {k: [] for k in ("pk", "pv", "ph", "pc", "pp", "kn", "vn", "sh", "sc", "sp")}
    for l in range(DEPTH):
        attn, rp, h_last, k_last_t, v_last_t, x_tail, z_tail = front_prompt(
            l, yp, seq, w_in, bias_prompt, sink_rows, pw)
        yp = out_ffn(l, yp, attn, rp, fw, FFN_TM)
        outs["pk"].append(k_last_t)
        outs["pv"].append(v_last_t)
        outs["ph"].append(h_last.reshape(batch, LRU_WIDTH))
        outs["pc"].append(x_tail[:, X_TAIL - (CONV_W - 1):, :])
        outs["pp"].append(z_tail[:, Z_TAIL - POOL_CTX:, :])

        qs, rs = in_proj(l, ys, w_in, dec)
        attn_s, rp_s, h_s = mixer_sample(
            l, qs.reshape(dec, GQA_GROUP, LANES), rs, ck_t, cv_t, state_h, state_conv_t, state_pool_t,
            bias_sample, sinks_t, pw)
        attn_s = attn_s.reshape(dec, N_HEADS, HEAD_DIM)[:, HEAD_ORDER_INV, :].reshape(dec, ATTN_WIDTH)
        ys = out_ffn(l, ys, attn_s, rp_s, fw, dec)
        outs["kn"].append(rs[:, K_BLK * KV_WIDTH:(K_BLK + 1) * KV_WIDTH])
        outs["vn"].append(rs[:, V_BLK * KV_WIDTH:(V_BLK + 1) * KV_WIDTH])
        outs["sh"].append(h_s)
        outs["sc"].append(jnp.concatenate([state_conv[l][:, 1:], rs[:, None, xr_cols]], axis=1))
        outs["sp"].append(jnp.concatenate([state_pool[l][:, 1:], rs[:, None, zp_cols]], axis=1))

    st = {k: jnp.stack(v) for k, v in outs.items()}
    sk_t, sv_t = cache_update(ck_t, cv_t, st["kn"], st["vn"])
    return (yp.reshape(batch, seq, D_MODEL), ys.reshape(dec, 1, D_MODEL),
            kv_unview(st["pk"]), kv_unview(st["pv"]), st["ph"], st["pc"], st["pp"],
            kv_unview(sk_t), kv_unview(sv_t), st["sh"], st["sc"], st["sp"])
```

```python
import functools

import jax
import jax.numpy as jnp
import numpy as np
from jax import lax
from jax.experimental import pallas as pl
from jax.experimental.pallas import tpu as pltpu

D_MODEL = 1024
DEPTH = 2
HEAD_DIM = 64
ATTN_WIDTH = 512
N_HEADS = 8
N_KV_HEADS = 2
GQA_GROUP = 4
KV_WIDTH = 128
WINDOW = 128
LRU_WIDTH = 256
LRU_C = 8.0
CONV_W = 4
POOL_WINDOWS = (2, 4, 8, 16)
POOL_WIDTH = 256
POOL_GROUP_W = 64
POOL_CTX = 15
MIX_WIDTH = ATTN_WIDTH + LRU_WIDTH + POOL_WIDTH
IN_WIDTH = 1536
REST_WIDTH = IN_WIDTH - ATTN_WIDTH
D_FF = 4096
LN_EPS = 1e-5
NEG_INF = -1e30
ALPHA = (2.0 * DEPTH) ** 0.25
Q_SCALE = HEAD_DIM ** -0.5

K_BLK, V_BLK = 0, 1
XR_BLK, GR_BLK, ZP_BLK = 1, 2, 3

LANES = 128
SUBLANES = 8
VMEM_LIMIT_BYTES = 56 * 1024 * 1024

TILE = 512
FFN_TM = 1024
FFN_FC = 1024
FFN_SUB = 256
X_TAIL = SUBLANES
Z_TAIL = 2 * SUBLANES
SAMPLE_BT = 16
CACHE_BT = 32

BF16 = jnp.bfloat16
F32 = jnp.float32

HEAD_ORDER = (0, 4, 1, 5, 2, 6, 3, 7)
HEAD_ORDER_INV = tuple(int(i) for i in np.argsort(HEAD_ORDER))

PROJ_COLS = ((0, ATTN_WIDTH), (ATTN_WIDTH, ATTN_WIDTH + 2 * KV_WIDTH),
             (ATTN_WIDTH + 2 * KV_WIDTH, ATTN_WIDTH + 2 * KV_WIDTH + 2 * LRU_WIDTH),
             (ATTN_WIDTH + 2 * KV_WIDTH + 2 * LRU_WIDTH, IN_WIDTH))


def _cparams(n_grid):
    return pltpu.CompilerParams(
        dimension_semantics=("arbitrary",) * n_grid,
        vmem_limit_bytes=VMEM_LIMIT_BYTES,
    )


def _whole_spec(shape):
    nd = len(shape)
    return pl.BlockSpec(shape, lambda *_: (0,) * nd, pipeline_mode=pl.Buffered(1))


def _layer_spec(shape, l):
    nd = len(shape) - 1
    return pl.BlockSpec((None,) + tuple(shape[1:]), lambda *_: (l,) + (0,) * nd, pipeline_mode=pl.Buffered(1))


def _layer_norm(x, g, b):
    mu = jnp.mean(x, axis=-1, keepdims=True)
    xc = x - mu
    var = jnp.mean(xc * xc, axis=-1, keepdims=True)
    return xc * lax.rsqrt(var + LN_EPS) * g + b


def _gelu_tanh(x):
    return 0.5 * x * (1.0 + jnp.tanh(np.sqrt(2.0 / np.pi) * (x + 0.044715 * (x * x * x))))


def _sigmoid(x):
    return 1.0 / (1.0 + jnp.exp(-x))


def _softplus(x):
    return jnp.maximum(x, 0.0) + jnp.log(1.0 + jnp.exp(-jnp.abs(x)))


def _interleave(*gens):
    active = list(gens)
    while active:
        for g in list(active):
            try:
                next(g)
            except StopIteration:
                active.remove(g)


def _attn_scores(q, k2, bias_t, lo):
    zero = jnp.zeros((), q.dtype)
    tiles = [q[:, c * LANES:(c + 1) * LANES] for c in range(GQA_GROUP)]
    qs = jnp.concatenate([jnp.where(lo, t, zero) for t in tiles]
                         + [jnp.where(lo, zero, t) for t in tiles], axis=0)
    return lax.dot_general(k2, qs, (((1,), (1,)), ((), ())), preferred_element_type=F32) + bias_t


def _attn_probs(s, sink):
    m = jnp.maximum(jnp.max(s, axis=0, keepdims=True), sink)
    p = jnp.exp(s - m)
    denom = jnp.sum(p, axis=0, keepdims=True) + jnp.exp(sink - m)
    return p.astype(BF16), 1.0 / denom


def _attn_values(p, v2):
    return lax.dot_general(v2, p, (((0,), (0,)), ((), ())), preferred_element_type=F32)


def _attn_output(o, inv_denom):
    o = o * inv_denom
    cols = []
    for c in range(ATTN_WIDTH // LANES):
        kv = (2 * c) // GQA_GROUP
        rows = slice(kv * HEAD_DIM, (kv + 1) * HEAD_DIM)
        blk = jnp.concatenate([o[rows, (2 * c) * WINDOW:(2 * c + 1) * WINDOW],
                               o[rows, (2 * c + 1) * WINDOW:(2 * c + 2) * WINDOW]], axis=0)
        cols.append(blk.T)
    return jnp.concatenate(cols, axis=1)


def _conv(l, xr, x_tail, cw_ref, cb_ref):
    xe = jnp.concatenate([x_tail, xr], axis=0)
    cw = cw_ref[...]
    xc = cb_ref[l:l + 1, :] + xr * cw[CONV_W - 1:CONV_W, :]
    for tap in range(CONV_W - 1):
        d = CONV_W - 1 - tap
        xc = xc + _shift_rows(xe, d)[X_TAIL:] * cw[tap:tap + 1, :]
    return xc


def _gate_logits(xc, wg_ref):
    return jnp.dot(xc.astype(BF16), wg_ref[...], preferred_element_type=F32)


def _gates(l, g, xc, ba_ref, bx_ref, lam_ref):
    r = _sigmoid(g[:, :LRU_WIDTH] + ba_ref[l:l + 1, :])
    i = _sigmoid(g[:, LRU_WIDTH:] + bx_ref[l:l + 1, :])
    log_a = (-LRU_C * r) * _softplus(-lam_ref[l:l + 1, :])
    a = jnp.exp(log_a)
    b = jnp.sqrt(1.0 - a * a) * (i * xc)
    return a, b


def _lru_scan(a, b, h_prev):
    t, w = a.shape
    groups = t // SUBLANES
    a3 = a.reshape(groups, SUBLANES, w)
    b3 = b.reshape(groups, SUBLANES, w)
    row = lax.broadcasted_iota(jnp.int32, (groups, SUBLANES, w), 1)
    s = 1
    while s < SUBLANES:
        keep = row >= s
        a_sh = jnp.where(keep, pltpu.roll(a3, s, 1), 1.0)
        b_sh = jnp.where(keep, pltpu.roll(b3, s, 1), 0.0)
        b3 = a3 * b_sh + b3
        a3 = a3 * a_sh
        s *= 2
    hs = []
    h = h_prev
    for g in range(groups):
        hg = a3[g] * h + b3[g]
        hs.append(hg)
        h = hg[SUBLANES - 1:SUBLANES, :]
    return jnp.concatenate(hs, axis=0), h


def _shift_rows(x, d):
    return pltpu.roll(x, d, 0)


def _pool_means(zp, z_tail, pos1):
    ze = jnp.concatenate([z_tail, zp], axis=0)
    t = zp.shape[0]
    lo = lax.broadcasted_iota(jnp.int32, (t, LANES), 1) < POOL_GROUP_W
    s2 = ze + _shift_rows(ze, 1)
    s4 = s2 + _shift_rows(s2, 2)
    hi4 = s4[:, LANES:]
    s8 = hi4 + _shift_rows(hi4, 4)
    s16 = s8 + _shift_rows(s8, 8)
    wins = (jnp.where(lo, s2[Z_TAIL:, :LANES], s4[Z_TAIL:, :LANES]),
            jnp.where(lo, s8[Z_TAIL:], s16[Z_TAIL:]))
    means = []
    for c, win in enumerate(wins):
        w_small, w_big = POOL_WINDOWS[2 * c], POOL_WINDOWS[2 * c + 1]
        count = jnp.minimum(pos1, jnp.where(lo, float(w_small), float(w_big)))
        means.append(win / count - zp[:, c * LANES:(c + 1) * LANES])
    return jnp.concatenate(means, axis=1)


def _layer_kernel(l, tiles_per_seq, xp_ref, xb_ref, wb_ref, bias0_ref, bias_ref, sink_ref,
                  cw_ref, cb_ref, wg_ref, ba_ref, bx_ref, lam_ref, wp_ref, ps_ref,
                  wo_ref, g1_ref, b1_ref, w1_ref, w2_ref, g2_ref, b2_ref,
                  y_ref, h_ref, kt_ref, vt_ref, xt_ref, zt_ref,
                  q_scr, r_scr, mix_scr, acc_ref, kprev, vprev, xtail, ztail, hcar):
    s = pl.program_id(0)
    slot_p = lax.rem(s, 2)
    slot_m = 1 - slot_p
    j = lax.rem(s + tiles_per_seq - 1, tiles_per_seq)

    @pl.when(s == 0)
    def _():
        q_scr[1] = jnp.zeros(q_scr.shape[1:], q_scr.dtype)
        r_scr[1] = jnp.zeros(r_scr.shape[1:], r_scr.dtype)
        mix_scr[0] = jnp.zeros(mix_scr.shape[1:], mix_scr.dtype)
        kprev[...] = jnp.zeros_like(kprev)
        vprev[...] = jnp.zeros_like(vprev)

    @pl.when((j == 0) | (s == 0))
    def _():
        xtail[...] = jnp.zeros_like(xtail)
        ztail[...] = jnp.zeros_like(ztail)
        hcar[...] = jnp.zeros_like(hcar)

    carry = {}

    def front():
        lo = lax.broadcasted_iota(jnp.int32, (WINDOW, LANES), 1) < HEAD_DIM
        sink = sink_ref[l:l + 1, :]
        xb16 = xp_ref[...].astype(BF16)
        k_prev, v_prev = kprev[...], vprev[...]
        x_tail, z_tail, h = xtail[...], ztail[...], hcar[...]
        k = v = None
        for c in range(TILE // WINDOW):
            c0, c1 = PROJ_COLS[c]
            u = jnp.dot(xb16, wb_ref[:, c0:c1], preferred_element_type=F32)
            if c == 0:
                q_scr[slot_p] = u.astype(BF16)
            else:
                r_scr[slot_p, :, c0 - ATTN_WIDTH:c1 - ATTN_WIDTH] = u
            yield

            rows = slice(c * WINDOW, (c + 1) * WINDOW)
            k = r_scr[slot_m, rows, K_BLK * KV_WIDTH:(K_BLK + 1) * KV_WIDTH]
            v = r_scr[slot_m, rows, V_BLK * KV_WIDTH:(V_BLK + 1) * KV_WIDTH]
            kb, vb = k.astype(BF16), v.astype(BF16)
            bias_t = bias0_ref[...] if c == 0 else bias_ref[...]
            sc = _attn_scores(q_scr[slot_m, rows, :], jnp.concatenate([k_prev, kb], axis=0), bias_t, lo)
            yield
            p, inv_denom = _attn_probs(sc, sink)
            o = _attn_values(p, jnp.concatenate([v_prev, vb], axis=0))
            yield
            mix_scr[slot_m, rows, :ATTN_WIDTH] = _attn_output(o, inv_denom).astype(BF16)
            k_prev, v_prev = kb, vb

            xr = r_scr[slot_m, rows, XR_BLK * LRU_WIDTH:(XR_BLK + 1) * LRU_WIDTH]
            gr = r_scr[slot_m, rows, GR_BLK * LRU_WIDTH:(GR_BLK + 1) * LRU_WIDTH]
            zp = r_scr[slot_m, rows, ZP_BLK * LRU_WIDTH:(ZP_BLK + 1) * LRU_WIDTH]
            xc = _conv(l, xr, x_tail, cw_ref, cb_ref)
            g = _gate_logits(xc, wg_ref)
            yield
            a, b = _gates(l, g, xc, ba_ref, bx_ref, lam_ref)
            hs, h = _lru_scan(a, b, h)
            rec = hs * _gelu_tanh(gr)
            pos1 = (j * TILE + c * WINDOW + lax.broadcasted_iota(jnp.int32, (WINDOW, LANES), 0) + 1).astype(F32)
            diff = _pool_means(zp, z_tail, pos1).astype(BF16)
            pool = jnp.dot(diff, wp_ref[...], preferred_element_type=F32) * ps_ref[l:l + 1, :]
            yield
            mix_scr[slot_m, rows, ATTN_WIDTH:] = jnp.concatenate([rec, pool], axis=1).astype(BF16)
            x_tail, z_tail = xr[WINDOW - X_TAIL:, :], zp[WINDOW - Z_TAIL:, :]
        carry.update(k_prev=k_prev, v_prev=v_prev, x_tail=x_tail, z_tail=z_tail, h=h, k=k, v=v)

    def back():
        groups = [slice(g * FFN_SUB, (g + 1) * FFN_SUB) for g in range(TILE // FFN_SUB)]
        mixes = []
        for rows in groups:
            mixes.append(jnp.dot(mix_scr[slot_p, rows, :], wo_ref[...], preferred_element_type=F32))
            yield
        x1s = [_layer_norm(ALPHA * xb_ref[rows, :] + mix, g1_ref[l:l + 1, :], b1_ref[l:l + 1, :])
               for rows, mix in zip(groups, mixes)]
        x1bs = [x1.astype(BF16) for x1 in x1s]
        for c in range(D_FF // FFN_FC):
            cols = slice(c * FFN_FC, (c + 1) * FFN_FC)
            for rows, x1b in zip(groups, x1bs):
                hid = jnp.dot(x1b, w1_ref[:, cols], preferred_element_type=F32)
                yield
                hid = jnp.square(jnp.maximum(hid, 0.0)).astype(BF16)
                part = jnp.dot(hid, w2_ref[cols, :], preferred_element_type=F32)
                yield
                if c == 0:
                    acc_ref[rows, :] = part
                else:
                    acc_ref[rows, :] += part
        for rows, x1 in zip(groups, x1s):
            y_ref[rows, :] = _layer_norm(ALPHA * x1 + acc_ref[rows, :], g2_ref[l:l + 1, :], b2_ref[l:l + 1, :])

    _interleave(back(), front())

    kprev[...] = carry["k_prev"]
    vprev[...] = carry["v_prev"]
    xtail[...] = carry["x_tail"]
    ztail[...] = carry["z_tail"]
    hcar[...] = carry["h"]

    @pl.when(j == tiles_per_seq - 1)
    def _():
        kt_ref[...] = carry["k"].T
        vt_ref[...] = carry["v"].T
        h_ref[...] = carry["h"]
        xt_ref[...] = carry["x_tail"]
        zt_ref[...] = carry["z_tail"]


def layer_prompt(l, x2d, seq, w_in_b, bias2, sink_rows, pw, fw):
    n = x2d.shape[0]
    b = n // seq
    n_tiles = n // TILE
    tiles_per_seq = seq // TILE
    clamp = lambda t: jnp.clip(t, 0, n_tiles - 1)
    seq_of = lambda s: clamp(s - 1) // tiles_per_seq
    bias_shape = (None, 2 * WINDOW, N_HEADS * WINDOW)
    per_seq = lambda rows, width: pl.BlockSpec((None, rows, width), lambda s: (seq_of(s), 0, 0))
    vec = _whole_spec((DEPTH, D_MODEL))
    return pl.pallas_call(
        functools.partial(_layer_kernel, l, tiles_per_seq),
        grid=(n_tiles + 2,),
        in_specs=[
            pl.BlockSpec((TILE, D_MODEL), lambda s: (clamp(s), 0)),
            pl.BlockSpec((TILE, D_MODEL), lambda s: (clamp(s - 2), 0)),
            _layer_spec(w_in_b.shape, l),
            pl.BlockSpec(bias_shape, lambda s: (jnp.minimum(clamp(s - 1) % tiles_per_seq, 1), 0, 0)),
            pl.BlockSpec(bias_shape, lambda s: (1, 0, 0), pipeline_mode=pl.Buffered(1)),
            _whole_spec(sink_rows.shape),
            _layer_spec(pw["conv_w"].shape, l), _whole_spec(pw["conv_b"].shape),
            _layer_spec(pw["w_gate"].shape, l), _whole_spec(pw["gate_a_b"].shape),
            _whole_spec(pw["gate_x_b"].shape), _whole_spec(pw["lam"].shape),
            _layer_spec(pw["w_pool"].shape, l), _whole_spec(pw["pool_scale"].shape),
            _layer_spec(fw["w_out"].shape, l), vec, vec,
            _layer_spec(fw["w_ff1"].shape, l), _layer_spec(fw["w_ff2"].shape, l), vec, vec,
        ],
        out_specs=[
            pl.BlockSpec((TILE, D_MODEL), lambda s: (clamp(s - 2), 0)),
            per_seq(1, LRU_WIDTH),
            per_seq(KV_WIDTH, WINDOW), per_seq(KV_WIDTH, WINDOW),
            per_seq(X_TAIL, LRU_WIDTH), per_seq(Z_TAIL, POOL_WIDTH),
        ],
        out_shape=[
            jax.ShapeDtypeStruct((n, D_MODEL), F32),
            jax.ShapeDtypeStruct((b, 1, LRU_WIDTH), F32),
            jax.ShapeDtypeStruct((b, KV_WIDTH, WINDOW), F32),
            jax.ShapeDtypeStruct((b, KV_WIDTH, WINDOW), F32),
            jax.ShapeDtypeStruct((b, X_TAIL, LRU_WIDTH), F32),
            jax.ShapeDtypeStruct((b, Z_TAIL, POOL_WIDTH), F32),
        ],
        scratch_shapes=[
            pltpu.VMEM((2, TILE, ATTN_WIDTH), BF16),
            pltpu.VMEM((2, TILE, REST_WIDTH), F32),
            pltpu.VMEM((2, TILE, MIX_WIDTH), BF16),
            pltpu.VMEM((TILE, D_MODEL), F32),
            pltpu.VMEM((WINDOW, KV_WIDTH), BF16),
            pltpu.VMEM((WINDOW, KV_WIDTH), BF16),
            pltpu.VMEM((X_TAIL, LRU_WIDTH), F32),
            pltpu.VMEM((Z_TAIL, POOL_WIDTH), F32),
            pltpu.VMEM((1, LRU_WIDTH), F32),
        ],
        compiler_params=_cparams(1),
        name="layer_prompt",
    )(x2d, x2d, w_in_b, bias2, bias2, sink_rows, pw["conv_w"], pw["conv_b"], pw["w_gate"], pw["gate_a_b"],
      pw["gate_x_b"], pw["lam"], pw["w_pool"], pw["pool_scale"],
      fw["w_out"], fw["ln1_g"], fw["ln1_b"], fw["w_ff1"], fw["w_ff2"], fw["ln2_g"], fw["ln2_b"])


def _in_proj_kernel(x_ref, w_ref, q_ref, r_ref):
    u = jnp.dot(x_ref[...].astype(BF16), w_ref[...], preferred_element_type=F32)
    q_ref[...] = u[:, :ATTN_WIDTH].astype(BF16)
    r_ref[...] = u[:, ATTN_WIDTH:]


def in_proj(l, x2d, w_in_b):
    n = x2d.shape[0]
    return pl.pallas_call(
        _in_proj_kernel,
        grid=(1,),
        in_specs=[_whole_spec(x2d.shape), _layer_spec(w_in_b.shape, l)],
        out_specs=[
            pl.BlockSpec((n, ATTN_WIDTH), lambda i: (0, 0)),
            pl.BlockSpec((n, REST_WIDTH), lambda i: (0, 0)),
        ],
        out_shape=[
            jax.ShapeDtypeStruct((n, ATTN_WIDTH), BF16),
            jax.ShapeDtypeStruct((n, REST_WIDTH), F32),
        ],
        compiler_params=_cparams(1),
        name="in_proj",
    )(x2d, w_in_b)


def _mixer_sample_kernel(l, q_ref, kn_ref, vn_ref, xr_ref, gr_ref, zp_ref, ck_ref, cv_ref,
                         h0_ref, sc_ref, sp_ref, bias_ref, sink_ref,
                         cw_ref, cb_ref, wg_ref, ba_ref, bx_ref, lam_ref, wp_ref, ps_ref,
                         attn_ref, rp_ref, h_ref):
    bt = SAMPLE_BT
    lo3 = lax.broadcasted_iota(jnp.int32, (bt, GQA_GROUP, LANES), 2) < HEAD_DIM
    q4 = q_ref[...].astype(F32)
    qm = jnp.concatenate([jnp.where(lo3, q4, 0.0), jnp.where(lo3, 0.0, q4)], axis=1)
    s = jnp.einsum("bqc,bck->bqk", qm.astype(BF16), ck_ref[...].astype(BF16),
                   preferred_element_type=F32) + bias_ref[...]
    s_new = jnp.sum(qm * kn_ref[...][:, None, :], axis=2, keepdims=True)
    sink = sink_ref[:, l:l + 1]
    m = jnp.maximum(jnp.maximum(jnp.max(s, axis=2, keepdims=True), s_new), sink)
    p = jnp.exp(s - m)
    p_new = jnp.exp(s_new - m)
    denom = jnp.sum(p, axis=2, keepdims=True) + p_new + jnp.exp(sink - m)
    o = jnp.einsum("bqk,bck->bqc", p.astype(BF16), cv_ref[...].astype(BF16),
                   preferred_element_type=F32)
    o = (o + p_new * vn_ref[...][:, None, :]) / denom
    attn_ref[...] = jnp.where(lo3, o[:, :GQA_GROUP, :], o[:, GQA_GROUP:, :]).astype(attn_ref.dtype)

    xr = xr_ref[...]
    cw = cw_ref[...]
    xc = cb_ref[l:l + 1, :] + xr * cw[CONV_W - 1:CONV_W, :]
    for tap in range(CONV_W - 1):
        xc = xc + sc_ref[tap] * cw[tap:tap + 1, :]
    a, b = _gates(l, _gate_logits(xc, wg_ref), xc, ba_ref, bx_ref, lam_ref)
    h = a * h0_ref[...] + b
    h_ref[...] = h
    rec = h * _gelu_tanh(gr_ref[...])

    z = zp_ref[...]
    lo = lax.broadcasted_iota(jnp.int32, (bt, LANES), 1) < POOL_GROUP_W
    means = []
    for c in range(POOL_WIDTH // LANES):
        w_small, w_big = POOL_WINDOWS[2 * c], POOL_WINDOWS[2 * c + 1]
        cols = slice(c * LANES, (c + 1) * LANES)
        zc = z[:, cols]
        acc = zc
        small = None
        for d in range(1, w_big):
            acc = acc + sp_ref[POOL_CTX - d][:, cols]
            if d + 1 == w_small:
                small = acc
        win = jnp.where(lo, small, acc)
        count = jnp.where(lo, float(w_small), float(w_big))
        means.append(win / count - zc)
    diff = jnp.concatenate(means, axis=1).astype(BF16)
    pool = jnp.dot(diff, wp_ref[...], preferred_element_type=F32) * ps_ref[l:l + 1, :]
    rp_ref[...] = jnp.concatenate([rec, pool], axis=1).astype(rp_ref.dtype)


def mixer_sample(l, q4, rs, ck_t, cv_t, state_h, state_conv_t, state_pool_t, bias_s, sinks_t, pw):
    n = rs.shape[0]
    bt = SAMPLE_BT
    cache_spec = pl.BlockSpec((None, bt, KV_WIDTH, WINDOW), lambda i: (l, i, 0, 0))
    rcol = lambda width, c: pl.BlockSpec((bt, width), lambda i: (i, c))
    return pl.pallas_call(
        functools.partial(_mixer_sample_kernel, l),
        grid=(n // bt,),
        in_specs=[
            pl.BlockSpec((bt, GQA_GROUP, LANES), lambda i: (i, 0, 0)),
            rcol(KV_WIDTH, K_BLK), rcol(KV_WIDTH, V_BLK),
            rcol(LRU_WIDTH, XR_BLK), rcol(LRU_WIDTH, GR_BLK), rcol(POOL_WIDTH, ZP_BLK),
            cache_spec, cache_spec,
            pl.BlockSpec((None, bt, LRU_WIDTH), lambda i: (l, i, 0)),
            pl.BlockSpec((None, CONV_W - 1, bt, LRU_WIDTH), lambda i: (l, 0, i, 0)),
            pl.BlockSpec((None, POOL_CTX, bt, POOL_WIDTH), lambda i: (l, 0, i, 0)),
            _whole_spec(bias_s.shape),
            _whole_spec(sinks_t.shape),
            _layer_spec(pw["conv_w"].shape, l), _whole_spec(pw["conv_b"].shape),
            _layer_spec(pw["w_gate"].shape, l), _whole_spec(pw["gate_a_b"].shape),
            _whole_spec(pw["gate_x_b"].shape), _whole_spec(pw["lam"].shape),
            _layer_spec(pw["w_pool"].shape, l), _whole_spec(pw["pool_scale"].shape),
        ],
        out_specs=[
            pl.BlockSpec((bt, GQA_GROUP, LANES), lambda i: (i, 0, 0)),
            pl.BlockSpec((bt, LRU_WIDTH + POOL_WIDTH), lambda i: (i, 0)),
            pl.BlockSpec((bt, LRU_WIDTH), lambda i: (i, 0)),
        ],
        out_shape=[
            jax.ShapeDtypeStruct((n, GQA_GROUP, LANES), BF16),
            jax.ShapeDtypeStruct((n, LRU_WIDTH + POOL_WIDTH), BF16),
            jax.ShapeDtypeStruct((n, LRU_WIDTH), F32),
        ],
        compiler_params=_cparams(1),
        name="mixer_sample",
    )(q4, rs, rs, rs, rs, rs, ck_t, cv_t, state_h, state_conv_t, state_pool_t, bias_s, sinks_t,
      pw["conv_w"], pw["conv_b"], pw["w_gate"], pw["gate_a_b"], pw["gate_x_b"], pw["lam"],
      pw["w_pool"], pw["pool_scale"])


def _cache_update_kernel(ck_ref, cv_ref, kn_ref, vn_ref, ok_ref, ov_ref):
    bt = CACHE_BT
    last = lax.broadcasted_iota(jnp.int32, (KV_WIDTH, WINDOW), 1) == WINDOW - 1
    pad = jnp.zeros((LANES - bt, KV_WIDTH), F32)
    for src, new, dst in ((ck_ref, kn_ref, ok_ref), (cv_ref, vn_ref, ov_ref)):
        new_t = jnp.concatenate([new[...], pad], axis=0).T
        for s in range(bt):
            shifted = pltpu.roll(src[s], WINDOW - 1, 1)
            col = pltpu.roll(new_t, WINDOW - 1 - s, 1)
            dst[s] = jnp.where(last, col, shifted)


def cache_update(ck_t, cv_t, k_new, v_new):
    depth, n = k_new.shape[:2]
    bt = CACHE_BT
    cache_spec = pl.BlockSpec((None, bt, KV_WIDTH, WINDOW), lambda l, i: (l, i, 0, 0))
    new_spec = pl.BlockSpec((None, bt, KV_WIDTH), lambda l, i: (l, i, 0))
    shape = jax.ShapeDtypeStruct(ck_t.shape, F32)
    return pl.pallas_call(
        _cache_update_kernel,
        grid=(depth, n // bt),
        in_specs=[cache_spec, cache_spec, new_spec, new_spec],
        out_specs=[cache_spec, cache_spec],
        out_shape=[shape, shape],
        compiler_params=_cparams(2),
        name="cache_update",
    )(ck_t, cv_t, k_new, v_new)


def _out_ffn_kernel(l, x_ref, at_ref, rp_ref, wo_ref, g1_ref, b1_ref,
                    w1_ref, w2_ref, g2_ref, b2_ref, y_ref):
    mix = jnp.dot(at_ref[...], wo_ref[:ATTN_WIDTH, :], preferred_element_type=F32)
    mix = mix + jnp.dot(rp_ref[...], wo_ref[ATTN_WIDTH:, :], preferred_element_type=F32)
    x1 = _layer_norm(ALPHA * x_ref[...] + mix, g1_ref[l:l + 1, :], b1_ref[l:l + 1, :])
    x1b = x1.astype(BF16)
    acc = None
    for c in range(D_FF // FFN_FC):
        cols = slice(c * FFN_FC, (c + 1) * FFN_FC)
        hid = jnp.dot(x1b, w1_ref[:, cols], preferred_element_type=F32)
        hid = jnp.square(jnp.maximum(hid, 0.0)).astype(BF16)
        part = jnp.dot(hid, w2_ref[cols, :], preferred_element_type=F32)
        acc = part if acc is None else acc + part
    y_ref[...] = _layer_norm(ALPHA * x1 + acc, g2_ref[l:l + 1, :], b2_ref[l:l + 1, :])


def out_ffn(l, x2d, attn2d, rp2d, fw):
    n = x2d.shape[0]
    whole = lambda width: pl.BlockSpec((n, width), lambda i: (0, 0))
    vec = _whole_spec((DEPTH, D_MODEL))
    return pl.pallas_call(
        functools.partial(_out_ffn_kernel, l),
        grid=(1,),
        in_specs=[
            whole(D_MODEL), whole(ATTN_WIDTH), whole(LRU_WIDTH + POOL_WIDTH),
            _layer_spec(fw["w_out"].shape, l), vec, vec,
            _layer_spec(fw["w_ff1"].shape, l), _layer_spec(fw["w_ff2"].shape, l), vec, vec,
        ],
        out_specs=whole(D_MODEL),
        out_shape=jax.ShapeDtypeStruct((n, D_MODEL), F32),
        compiler_params=_cparams(1),
        name="out_ffn",
    )(x2d, attn2d, rp2d, fw["w_out"], fw["ln1_g"], fw["ln1_b"],
      fw["w_ff1"], fw["w_ff2"], fw["ln2_g"], fw["ln2_b"])


def _block_diag(w):
    depth, g, c, d = w.shape
    eye = jnp.eye(g, dtype=bool)[None, :, None, :, None]
    return jnp.where(eye, w[:, :, :, None, :], 0.0).reshape(depth, g * c, g * d)


def _proj_weights(w_in):
    depth = w_in.shape[0]
    wq = w_in[:, :, :ATTN_WIDTH].reshape(depth, D_MODEL, N_HEADS, HEAD_DIM)[:, :, HEAD_ORDER, :] * Q_SCALE
    return jnp.concatenate([wq.reshape(depth, D_MODEL, ATTN_WIDTH), w_in[:, :, ATTN_WIDTH:]], axis=2).astype(BF16)


def _alibi_slopes():
    return jnp.exp2(-8.0 * (jnp.arange(N_HEADS, dtype=F32) + 1.0) / N_HEADS)


def _prompt_bias_tables():
    slopes = _alibi_slopes()
    jk = jnp.arange(2 * WINDOW)[:, None]
    tq = jnp.arange(WINDOW)[None, :]
    delta = tq + WINDOW - jk
    visible = (delta >= 0) & (delta <= WINDOW)
    bias = -slopes[None, :, None] * delta.astype(F32)[:, None, :]
    full = jnp.where(visible[:, None, :], bias, NEG_INF)
    first = jnp.where((visible & (jk >= WINDOW))[:, None, :], bias, NEG_INF)
    return jnp.stack([first, full]).reshape(2, 2 * WINDOW, N_HEADS * WINDOW)


def kernel(x_prompt, x_sample, cache_k, cache_v, state_h, state_conv, state_pool, w_in, attn_sinks, conv_w, conv_b, gate_a_w, gate_a_b, gate_x_w, gate_x_b, lru_lambda, pool_w, pool_scale, w_out, ln1_g, ln1_b, w_ff1, w_ff2, ln2_g, ln2_b):
    batch, seq, _ = x_prompt.shape
    dec = x_sample.shape[0]
    bias_prompt = _prompt_bias_tables()
    bias_sample = -_alibi_slopes()[:, None] * (WINDOW - jnp.arange(WINDOW, dtype=F32))[None, :]
    sink_rows = jnp.repeat(attn_sinks, WINDOW, axis=1)
    sinks_t = attn_sinks.T

    w_in_b = _proj_weights(w_in)
    pw = {
        "conv_w": conv_w, "conv_b": conv_b,
        "w_gate": jnp.concatenate([_block_diag(gate_a_w), _block_diag(gate_x_w)], axis=2).astype(BF16),
        "gate_a_b": gate_a_b, "gate_x_b": gate_x_b, "lam": lru_lambda,
        "w_pool": _block_diag(pool_w).astype(BF16), "pool_scale": pool_scale,
    }
    fw = {
        "w_out": w_out.astype(BF16), "w_ff1": w_ff1.astype(BF16), "w_ff2": w_ff2.astype(BF16),
        "ln1_g": ln1_g, "ln1_b": ln1_b, "ln2_g": ln2_g, "ln2_b": ln2_b,
    }

    def kv_view(c):
        return jnp.transpose(c, (0, 1, 3, 4, 2)).reshape(c.shape[0], c.shape[1], KV_WIDTH, WINDOW)

    def kv_unview(c_t):
        c5 = c_t.reshape(c_t.shape[0], c_t.shape[1], N_KV_HEADS, HEAD_DIM, WINDOW)
        return jnp.transpose(c5, (0, 1, 4, 2, 3))

    ck_t, cv_t = kv_view(cache_k), kv_view(cache_v)
    state_conv_t = jnp.swapaxes(state_conv, 1, 2)
    state_pool_t = jnp.swapaxes(state_pool, 1, 2)

    yp = x_prompt.reshape(batch * seq, D_MODEL)
    ys = x_sample.reshape(dec, D_MODEL)
    xr_cols = slice(XR_BLK * LRU_WIDTH, (XR_BLK + 1) * LRU_WIDTH)
    zp_cols = slice(ZP_BLK * LRU_WIDTH, (ZP_BLK + 1) * LRU_WIDTH)
    outs = {k: [] for k in ("pk", "pv", "ph", "pc", "pp", "kn", "vn", "sh", "sc", "sp")}
    for l in range(DEPTH):
        yp, h_last, k_last_t, v_last_t, x_tail, z_tail = layer_prompt(
            l, yp, seq, w_in_b, bias_prompt, sink_rows, pw, fw)
        outs["pk"].append(k_last_t)
        outs["pv"].append(v_last_t)
        outs["ph"].append(h_last.reshape(batch, LRU_WIDTH))
        outs["pc"].append(x_tail[:, X_TAIL - (CONV_W - 1):, :])
        outs["pp"].append(z_tail[:, Z_TAIL - POOL_CTX:, :])

        qs, rs = in_proj(l, ys, w_in_b)
        attn_s, rp_s, h_s = mixer_sample(
            l, qs.reshape(dec, GQA_GROUP, LANES), rs, ck_t, cv_t, state_h, state_conv_t, state_pool_t,
            bias_sample, sinks_t, pw)
        attn_s = attn_s.reshape(dec, N_HEADS, HEAD_DIM)[:, HEAD_ORDER_INV, :].reshape(dec, ATTN_WIDTH)
        ys = out_ffn(l, ys, attn_s, rp_s, fw)
        outs["kn"].append(rs[:, K_BLK * KV_WIDTH:(K_BLK + 1) * KV_WIDTH])
        outs["vn"].append(rs[:, V_BLK * KV_WIDTH:(V_BLK + 1) * KV_WIDTH])
        outs["sh"].append(h_s)
        outs["sc"].append(jnp.concatenate([state_conv[l][:, 1:], rs[:, None, xr_cols]], axis=1))
        outs["sp"].append(jnp.concatenate([state_pool[l][:, 1:], rs[:, None, zp_cols]], axis=1))

    st = {k: jnp.stack(v) for k, v in outs.items()}
    sk_t, sv_t = cache_update(ck_t, cv_t, st["kn"], st["vn"])
    return (yp.reshape(batch, seq, D_MODEL), ys.reshape(dec, 1, D_MODEL),
            kv_unview(st["pk"]), kv_unview(st["pv"]), st["ph"], st["pc"], st["pp"],
            kv_unview(sk_t), kv_unview(sv_t), st["sh"], st["sc"], st["sp"])
```

```python
import functools

import jax
import jax.numpy as jnp
import numpy as np
from jax import lax
from jax.experimental import pallas as pl
from jax.experimental.pallas import tpu as pltpu

D_MODEL = 1024
DEPTH = 2
HEAD_DIM = 64
ATTN_WIDTH = 512
N_HEADS = 8
N_KV_HEADS = 2
GQA_GROUP = 4
KV_WIDTH = 128
WINDOW = 128
LRU_WIDTH = 256
LRU_C = 8.0
CONV_W = 4
POOL_WINDOWS = (2, 4, 8, 16)
POOL_WIDTH = 256
POOL_GROUP_W = 64
POOL_CTX = 15
MIX_WIDTH = ATTN_WIDTH + LRU_WIDTH + POOL_WIDTH
IN_WIDTH = 1536
REST_WIDTH = IN_WIDTH - ATTN_WIDTH
D_FF = 4096
LN_EPS = 1e-5
NEG_INF = -1e30
ALPHA = (2.0 * DEPTH) ** 0.25
Q_SCALE = HEAD_DIM ** -0.5

K_BLK, V_BLK = 0, 1
XR_BLK, GR_BLK, ZP_BLK = 1, 2, 3

LANES = 128
SUBLANES = 8
VMEM_LIMIT_BYTES = 56 * 1024 * 1024

TILE = 512
FFN_TM = 1024
FFN_FC = 1024
FFN_SUB = 256
X_TAIL = SUBLANES
Z_TAIL = 2 * SUBLANES
SAMPLE_BT = 16
CACHE_BT = 32

BF16 = jnp.bfloat16
F32 = jnp.float32

HEAD_ORDER = (0, 4, 1, 5, 2, 6, 3, 7)
HEAD_ORDER_INV = tuple(int(i) for i in np.argsort(HEAD_ORDER))

PROJ_COLS = ((0, ATTN_WIDTH), (ATTN_WIDTH, ATTN_WIDTH + 2 * KV_WIDTH),
             (ATTN_WIDTH + 2 * KV_WIDTH, ATTN_WIDTH + 2 * KV_WIDTH + 2 * LRU_WIDTH),
             (ATTN_WIDTH + 2 * KV_WIDTH + 2 * LRU_WIDTH, IN_WIDTH))


def _cparams(n_grid):
    return pltpu.CompilerParams(
        dimension_semantics=("arbitrary",) * n_grid,
        vmem_limit_bytes=VMEM_LIMIT_BYTES,
    )


def _whole_spec(shape):
    nd = len(shape)
    return pl.BlockSpec(shape, lambda *_: (0,) * nd, pipeline_mode=pl.Buffered(1))


def _layer_spec(shape, l):
    nd = len(shape) - 1
    return pl.BlockSpec((None,) + tuple(shape[1:]), lambda *_: (l,) + (0,) * nd, pipeline_mode=pl.Buffered(1))


def _layer_norm(x, g, b):
    mu = jnp.mean(x, axis=-1, keepdims=True)
    xc = x - mu
    var = jnp.mean(xc * xc, axis=-1, keepdims=True)
    return xc * lax.rsqrt(var + LN_EPS) * g + b


def _gelu_tanh(x):
    return 0.5 * x * (1.0 + jnp.tanh(np.sqrt(2.0 / np.pi) * (x + 0.044715 * (x * x * x))))


def _sigmoid(x):
    return 1.0 / (1.0 + jnp.exp(-x))


def _softplus(x):
    return jnp.maximum(x, 0.0) + jnp.log(1.0 + jnp.exp(-jnp.abs(x)))


def _interleave(*gens):
    active = list(gens)
    while active:
        for g in list(active):
            try:
                next(g)
            except StopIteration:
                active.remove(g)


def _attn_scores(q, k2, bias_t, lo):
    zero = jnp.zeros((), q.dtype)
    tiles = [q[:, c * LANES:(c + 1) * LANES] for c in range(GQA_GROUP)]
    qs = jnp.concatenate([jnp.where(lo, t, zero) for t in tiles]
                         + [jnp.where(lo, zero, t) for t in tiles], axis=0)
    return lax.dot_general(k2, qs, (((1,), (1,)), ((), ())), preferred_element_type=F32) + bias_t


def _attn_probs(s, sink):
    m = jnp.maximum(jnp.max(s, axis=0, keepdims=True), sink)
    p = jnp.exp(s - m)
    denom = jnp.sum(p, axis=0, keepdims=True) + jnp.exp(sink - m)
    return p.astype(BF16), 1.0 / denom


def _attn_values(p, v2):
    return lax.dot_general(v2, p, (((0,), (0,)), ((), ())), preferred_element_type=F32)


def _attn_output(o, inv_denom):
    o = o * inv_denom
    cols = []
    for c in range(ATTN_WIDTH // LANES):
        kv = (2 * c) // GQA_GROUP
        rows = slice(kv * HEAD_DIM, (kv + 1) * HEAD_DIM)
        blk = jnp.concatenate([o[rows, (2 * c) * WINDOW:(2 * c + 1) * WINDOW],
                               o[rows, (2 * c + 1) * WINDOW:(2 * c + 2) * WINDOW]], axis=0)
        cols.append(blk.T)
    return jnp.concatenate(cols, axis=1)


def _conv(l, xr, x_tail, cw_ref, cb_ref):
    xe = jnp.concatenate([x_tail, xr], axis=0)
    cw = cw_ref[...]
    xc = cb_ref[l:l + 1, :] + xr * cw[CONV_W - 1:CONV_W, :]
    for tap in range(CONV_W - 1):
        d = CONV_W - 1 - tap
        xc = xc + _shift_rows(xe, d)[X_TAIL:] * cw[tap:tap + 1, :]
    return xc


def _gate_logits(xc, wg_ref):
    return jnp.dot(xc.astype(BF16), wg_ref[...], preferred_element_type=F32)


def _gates(l, g, xc, ba_ref, bx_ref, lam_ref):
    r = _sigmoid(g[:, :LRU_WIDTH] + ba_ref[l:l + 1, :])
    i = _sigmoid(g[:, LRU_WIDTH:] + bx_ref[l:l + 1, :])
    log_a = (-LRU_C * r) * _softplus(-lam_ref[l:l + 1, :])
    a = jnp.exp(log_a)
    b = jnp.sqrt(1.0 - a * a) * (i * xc)
    return a, b


def _lru_scan(a, b, h_prev):
    t, w = a.shape
    groups = t // SUBLANES
    a3 = a.reshape(groups, SUBLANES, w)
    b3 = b.reshape(groups, SUBLANES, w)
    row = lax.broadcasted_iota(jnp.int32, (groups, SUBLANES, w), 1)
    s = 1
    while s < SUBLANES:
        keep = row >= s
        a_sh = jnp.where(keep, pltpu.roll(a3, s, 1), 1.0)
        b_sh = jnp.where(keep, pltpu.roll(b3, s, 1), 0.0)
        b3 = a3 * b_sh + b3
        a3 = a3 * a_sh
        s *= 2
    hs = []
    h = h_prev
    for g in range(groups):
        hg = a3[g] * h + b3[g]
        hs.append(hg)
        h = hg[SUBLANES - 1:SUBLANES, :]
    return jnp.concatenate(hs, axis=0), h


def _shift_rows(x, d):
    return pltpu.roll(x, d, 0)


def _pool_means(zp, z_tail, pos1):
    ze = jnp.concatenate([z_tail, zp], axis=0)
    t = zp.shape[0]
    lo = lax.broadcasted_iota(jnp.int32, (t, LANES), 1) < POOL_GROUP_W
    s2 = ze + _shift_rows(ze, 1)
    s4 = s2 + _shift_rows(s2, 2)
    hi4 = s4[:, LANES:]
    s8 = hi4 + _shift_rows(hi4, 4)
    s16 = s8 + _shift_rows(s8, 8)
    wins = (jnp.where(lo, s2[Z_TAIL:, :LANES], s4[Z_TAIL:, :LANES]),
            jnp.where(lo, s8[Z_TAIL:], s16[Z_TAIL:]))
    means = []
    for c, win in enumerate(wins):
        w_small, w_big = POOL_WINDOWS[2 * c], POOL_WINDOWS[2 * c + 1]
        count = jnp.minimum(pos1, jnp.where(lo, float(w_small), float(w_big)))
        means.append(win / count - zp[:, c * LANES:(c + 1) * LANES])
    return jnp.concatenate(means, axis=1)


def _prep_w_in(w_ref, wb_ref):
    lo = lax.broadcasted_iota(jnp.int32, (D_MODEL, LANES), 1) < HEAD_DIM
    src = [w_ref[:, s * LANES:(s + 1) * LANES] * Q_SCALE for s in range(ATTN_WIDTH // LANES)]
    swapped = [pltpu.roll(t, HEAD_DIM, 1) for t in src]
    for c in range(GQA_GROUP):
        s0, s1 = c // 2, GQA_GROUP // 2 + c // 2
        if c % 2 == 0:
            tile = jnp.where(lo, src[s0], swapped[s1])
        else:
            tile = jnp.where(lo, swapped[s0], src[s1])
        wb_ref[:, c * LANES:(c + 1) * LANES] = tile.astype(BF16)
    wb_ref[:, ATTN_WIDTH:] = w_ref[:, ATTN_WIDTH:].astype(BF16)


def _front_kernel(l, tiles_per_seq, xp_ref, w_ref, bias0_ref, bias_ref, sink_ref,
                  cw_ref, cb_ref, wg_ref, ba_ref, bx_ref, lam_ref, wp_ref, ps_ref,
                  attn_ref, rp_ref, h_ref, kt_ref, vt_ref, xt_ref, zt_ref,
                  wb_ref, q_scr, r_scr, kprev, vprev, xtail, ztail, hcar):
    s = pl.program_id(0)
    slot_p = lax.rem(s, 2)
    slot_m = 1 - slot_p
    j = lax.rem(s + tiles_per_seq - 1, tiles_per_seq)

    @pl.when(s == 0)
    def _():
        _prep_w_in(w_ref, wb_ref)
        q_scr[1] = jnp.zeros(q_scr.shape[1:], q_scr.dtype)
        r_scr[1] = jnp.zeros(r_scr.shape[1:], r_scr.dtype)
        kprev[...] = jnp.zeros_like(kprev)
        vprev[...] = jnp.zeros_like(vprev)

    @pl.when((j == 0) | (s == 0))
    def _():
        xtail[...] = jnp.zeros_like(xtail)
        ztail[...] = jnp.zeros_like(ztail)
        hcar[...] = jnp.zeros_like(hcar)

    carry = {}
    sub_blocks = [slice(c * WINDOW, (c + 1) * WINDOW) for c in range(TILE // WINDOW)]

    def attention():
        lo = lax.broadcasted_iota(jnp.int32, (WINDOW, LANES), 1) < HEAD_DIM
        sink = sink_ref[l:l + 1, :]
        xb16 = xp_ref[...].astype(BF16)
        k_prev, v_prev = kprev[...], vprev[...]
        k = v = None
        for c, rows in enumerate(sub_blocks):
            k = r_scr[slot_m, rows, K_BLK * KV_WIDTH:(K_BLK + 1) * KV_WIDTH]
            v = r_scr[slot_m, rows, V_BLK * KV_WIDTH:(V_BLK + 1) * KV_WIDTH]
            kb, vb = k.astype(BF16), v.astype(BF16)
            bias_t = bias0_ref[...] if c == 0 else bias_ref[...]
            sc = _attn_scores(q_scr[slot_m, rows, :], jnp.concatenate([k_prev, kb], axis=0), bias_t, lo)
            yield
            c0, c1 = PROJ_COLS[c]
            u = jnp.dot(xb16, wb_ref[:, c0:c1], preferred_element_type=F32)
            if c == 0:
                q_scr[slot_p] = u.astype(BF16)
            else:
                r_scr[slot_p, :, c0 - ATTN_WIDTH:c1 - ATTN_WIDTH] = u
            yield
            p, inv_denom = _attn_probs(sc, sink)
            o = _attn_values(p, jnp.concatenate([v_prev, vb], axis=0))
            yield
            attn_ref[rows, :] = _attn_output(o, inv_denom).astype(attn_ref.dtype)
            k_prev, v_prev = kb, vb
        carry.update(k_prev=k_prev, v_prev=v_prev, k=k, v=v)

    def mixers():
        x_tail, z_tail, h = xtail[...], ztail[...], hcar[...]
        for c, rows in enumerate(sub_blocks):
            xr = r_scr[slot_m, rows, XR_BLK * LRU_WIDTH:(XR_BLK + 1) * LRU_WIDTH]
            gr = r_scr[slot_m, rows, GR_BLK * LRU_WIDTH:(GR_BLK + 1) * LRU_WIDTH]
            zp = r_scr[slot_m, rows, ZP_BLK * LRU_WIDTH:(ZP_BLK + 1) * LRU_WIDTH]
            xc = _conv(l, xr, x_tail, cw_ref, cb_ref)
            g = _gate_logits(xc, wg_ref)
            yield
            a, b = _gates(l, g, xc, ba_ref, bx_ref, lam_ref)
            hs, h = _lru_scan(a, b, h)
            rec = hs * _gelu_tanh(gr)
            pos1 = (j * TILE + c * WINDOW + lax.broadcasted_iota(jnp.int32, (WINDOW, LANES), 0) + 1).astype(F32)
            diff = _pool_means(zp, z_tail, pos1).astype(BF16)
            pool = jnp.dot(diff, wp_ref[...], preferred_element_type=F32) * ps_ref[l:l + 1, :]
            yield
            rp_ref[rows, :] = jnp.concatenate([rec, pool], axis=1).astype(rp_ref.dtype)
            x_tail, z_tail = xr[WINDOW - X_TAIL:, :], zp[WINDOW - Z_TAIL:, :]
        carry.update(x_tail=x_tail, z_tail=z_tail, h=h)

    _interleave(attention(), mixers())

    kprev[...] = carry["k_prev"]
    vprev[...] = carry["v_prev"]
    xtail[...] = carry["x_tail"]
    ztail[...] = carry["z_tail"]
    hcar[...] = carry["h"]

    @pl.when(j == tiles_per_seq - 1)
    def _():
        kt_ref[...] = carry["k"].T
        vt_ref[...] = carry["v"].T
        h_ref[...] = carry["h"]
        xt_ref[...] = carry["x_tail"]
        zt_ref[...] = carry["z_tail"]


def front_prompt(l, x2d, seq, w_in, bias2, sink_rows, pw):
    n = x2d.shape[0]
    b = n // seq
    n_tiles = n // TILE
    tiles_per_seq = seq // TILE
    done = lambda s: jnp.maximum(s - 1, 0)
    seq_of = lambda s: done(s) // tiles_per_seq
    bias_shape = (None, 2 * WINDOW, N_HEADS * WINDOW)
    per_seq = lambda rows, width: pl.BlockSpec((None, rows, width), lambda s: (seq_of(s), 0, 0))
    return pl.pallas_call(
        functools.partial(_front_kernel, l, tiles_per_seq),
        grid=(n_tiles + 1,),
        in_specs=[
            pl.BlockSpec((TILE, D_MODEL), lambda s: (jnp.minimum(s, n_tiles - 1), 0)),
            _layer_spec(w_in.shape, l),
            pl.BlockSpec(bias_shape, lambda s: (jnp.minimum(done(s) % tiles_per_seq, 1), 0, 0)),
            pl.BlockSpec(bias_shape, lambda s: (1, 0, 0), pipeline_mode=pl.Buffered(1)),
            _whole_spec(sink_rows.shape),
            _layer_spec(pw["conv_w"].shape, l), _whole_spec(pw["conv_b"].shape),
            _layer_spec(pw["w_gate"].shape, l), _whole_spec(pw["gate_a_b"].shape),
            _whole_spec(pw["gate_x_b"].shape), _whole_spec(pw["lam"].shape),
            _layer_spec(pw["w_pool"].shape, l), _whole_spec(pw["pool_scale"].shape),
        ],
        out_specs=[
            pl.BlockSpec((TILE, ATTN_WIDTH), lambda s: (done(s), 0)),
            pl.BlockSpec((TILE, LRU_WIDTH + POOL_WIDTH), lambda s: (done(s), 0)),
            per_seq(1, LRU_WIDTH),
            per_seq(KV_WIDTH, WINDOW), per_seq(KV_WIDTH, WINDOW),
            per_seq(X_TAIL, LRU_WIDTH), per_seq(Z_TAIL, POOL_WIDTH),
        ],
        out_shape=[
            jax.ShapeDtypeStruct((n, ATTN_WIDTH), BF16),
            jax.ShapeDtypeStruct((n, LRU_WIDTH + POOL_WIDTH), BF16),
            jax.ShapeDtypeStruct((b, 1, LRU_WIDTH), F32),
            jax.ShapeDtypeStruct((b, KV_WIDTH, WINDOW), F32),
            jax.ShapeDtypeStruct((b, KV_WIDTH, WINDOW), F32),
            jax.ShapeDtypeStruct((b, X_TAIL, LRU_WIDTH), F32),
            jax.ShapeDtypeStruct((b, Z_TAIL, POOL_WIDTH), F32),
        ],
        scratch_shapes=[
            pltpu.VMEM((D_MODEL, IN_WIDTH), BF16),
            pltpu.VMEM((2, TILE, ATTN_WIDTH), BF16),
            pltpu.VMEM((2, TILE, REST_WIDTH), F32),
            pltpu.VMEM((WINDOW, KV_WIDTH), BF16),
            pltpu.VMEM((WINDOW, KV_WIDTH), BF16),
            pltpu.VMEM((X_TAIL, LRU_WIDTH), F32),
            pltpu.VMEM((Z_TAIL, POOL_WIDTH), F32),
            pltpu.VMEM((1, LRU_WIDTH), F32),
        ],
        compiler_params=_cparams(1),
        name="front_prompt",
    )(x2d, w_in, bias2, bias2, sink_rows, pw["conv_w"], pw["conv_b"], pw["w_gate"], pw["gate_a_b"],
      pw["gate_x_b"], pw["lam"], pw["w_pool"], pw["pool_scale"])


def _in_proj_kernel(x_ref, w_ref, q_ref, r_ref, wb_ref):
    _prep_w_in(w_ref, wb_ref)
    u = jnp.dot(x_ref[...].astype(BF16), wb_ref[...], preferred_element_type=F32)
    q_ref[...] = u[:, :ATTN_WIDTH].astype(BF16)
    r_ref[...] = u[:, ATTN_WIDTH:]


def in_proj(l, x2d, w_in):
    n = x2d.shape[0]
    return pl.pallas_call(
        _in_proj_kernel,
        grid=(1,),
        in_specs=[_whole_spec(x2d.shape), _layer_spec(w_in.shape, l)],
        out_specs=[
            pl.BlockSpec((n, ATTN_WIDTH), lambda i: (0, 0)),
            pl.BlockSpec((n, REST_WIDTH), lambda i: (0, 0)),
        ],
        out_shape=[
            jax.ShapeDtypeStruct((n, ATTN_WIDTH), BF16),
            jax.ShapeDtypeStruct((n, REST_WIDTH), F32),
        ],
        scratch_shapes=[pltpu.VMEM((D_MODEL, IN_WIDTH), BF16)],
        compiler_params=_cparams(1),
        name="in_proj",
    )(x2d, w_in)


def _mixer_sample_kernel(l, q_ref, kn_ref, vn_ref, xr_ref, gr_ref, zp_ref, ck_ref, cv_ref,
                         h0_ref, sc_ref, sp_ref, bias_ref, sink_ref,
                         cw_ref, cb_ref, wg_ref, ba_ref, bx_ref, lam_ref, wp_ref, ps_ref,
                         attn_ref, rp_ref, h_ref):
    bt = SAMPLE_BT
    lo3 = lax.broadcasted_iota(jnp.int32, (bt, GQA_GROUP, LANES), 2) < HEAD_DIM
    q4 = q_ref[...].astype(F32)
    qm = jnp.concatenate([jnp.where(lo3, q4, 0.0), jnp.where(lo3, 0.0, q4)], axis=1)
    s = jnp.einsum("bqc,bck->bqk", qm.astype(BF16), ck_ref[...].astype(BF16),
                   preferred_element_type=F32) + bias_ref[...]
    s_new = jnp.sum(qm * kn_ref[...][:, None, :], axis=2, keepdims=True)
    sink = sink_ref[:, l:l + 1]
    m = jnp.maximum(jnp.maximum(jnp.max(s, axis=2, keepdims=True), s_new), sink)
    p = jnp.exp(s - m)
    p_new = jnp.exp(s_new - m)
    denom = jnp.sum(p, axis=2, keepdims=True) + p_new + jnp.exp(sink - m)
    o = jnp.einsum("bqk,bck->bqc", p.astype(BF16), cv_ref[...].astype(BF16),
                   preferred_element_type=F32)
    o = (o + p_new * vn_ref[...][:, None, :]) / denom
    attn_ref[...] = jnp.where(lo3, o[:, :GQA_GROUP, :], o[:, GQA_GROUP:, :]).astype(attn_ref.dtype)

    xr = xr_ref[...]
    cw = cw_ref[...]
    xc = cb_ref[l:l + 1, :] + xr * cw[CONV_W - 1:CONV_W, :]
    for tap in range(CONV_W - 1):
        xc = xc + sc_ref[tap] * cw[tap:tap + 1, :]
    a, b = _gates(l, _gate_logits(xc, wg_ref), xc, ba_ref, bx_ref, lam_ref)
    h = a * h0_ref[...] + b
    h_ref[...] = h
    rec = h * _gelu_tanh(gr_ref[...])

    z = zp_ref[...]
    lo = lax.broadcasted_iota(jnp.int32, (bt, LANES), 1) < POOL_GROUP_W
    means = []
    for c in range(POOL_WIDTH // LANES):
        w_small, w_big = POOL_WINDOWS[2 * c], POOL_WINDOWS[2 * c + 1]
        cols = slice(c * LANES, (c + 1) * LANES)
        zc = z[:, cols]
        acc = zc
        small = None
        for d in range(1, w_big):
            acc = acc + sp_ref[POOL_CTX - d][:, cols]
            if d + 1 == w_small:
                small = acc
        win = jnp.where(lo, small, acc)
        count = jnp.where(lo, float(w_small), float(w_big))
        means.append(win / count - zc)
    diff = jnp.concatenate(means, axis=1).astype(BF16)
    pool = jnp.dot(diff, wp_ref[...], preferred_element_type=F32) * ps_ref[l:l + 1, :]
    rp_ref[...] = jnp.concatenate([rec, pool], axis=1).astype(rp_ref.dtype)


def mixer_sample(l, q4, rs, ck_t, cv_t, state_h, state_conv_t, state_pool_t, bias_s, sinks_t, pw):
    n = rs.shape[0]
    bt = SAMPLE_BT
    cache_spec = pl.BlockSpec((None, bt, KV_WIDTH, WINDOW), lambda i: (l, i, 0, 0))
    rcol = lambda width, c: pl.BlockSpec((bt, width), lambda i: (i, c))
    return pl.pallas_call(
        functools.partial(_mixer_sample_kernel, l),
        grid=(n // bt,),
        in_specs=[
            pl.BlockSpec((bt, GQA_GROUP, LANES), lambda i: (i, 0, 0)),
            rcol(KV_WIDTH, K_BLK), rcol(KV_WIDTH, V_BLK),
            rcol(LRU_WIDTH, XR_BLK), rcol(LRU_WIDTH, GR_BLK), rcol(POOL_WIDTH, ZP_BLK),
            cache_spec, cache_spec,
            pl.BlockSpec((None, bt, LRU_WIDTH), lambda i: (l, i, 0)),
            pl.BlockSpec((None, CONV_W - 1, bt, LRU_WIDTH), lambda i: (l, 0, i, 0)),
            pl.BlockSpec((None, POOL_CTX, bt, POOL_WIDTH), lambda i: (l, 0, i, 0)),
            _whole_spec(bias_s.shape),
            _whole_spec(sinks_t.shape),
            _layer_spec(pw["conv_w"].shape, l), _whole_spec(pw["conv_b"].shape),
            _layer_spec(pw["w_gate"].shape, l), _whole_spec(pw["gate_a_b"].shape),
            _whole_spec(pw["gate_x_b"].shape), _whole_spec(pw["lam"].shape),
            _layer_spec(pw["w_pool"].shape, l), _whole_spec(pw["pool_scale"].shape),
        ],
        out_specs=[
            pl.BlockSpec((bt, GQA_GROUP, LANES), lambda i: (i, 0, 0)),
            pl.BlockSpec((bt, LRU_WIDTH + POOL_WIDTH), lambda i: (i, 0)),
            pl.BlockSpec((bt, LRU_WIDTH), lambda i: (i, 0)),
        ],
        out_shape=[
            jax.ShapeDtypeStruct((n, GQA_GROUP, LANES), BF16),
            jax.ShapeDtypeStruct((n, LRU_WIDTH + POOL_WIDTH), BF16),
            jax.ShapeDtypeStruct((n, LRU_WIDTH), F32),
        ],
        compiler_params=_cparams(1),
        name="mixer_sample",
    )(q4, rs, rs, rs, rs, rs, ck_t, cv_t, state_h, state_conv_t, state_pool_t, bias_s, sinks_t,
      pw["conv_w"], pw["conv_b"], pw["w_gate"], pw["gate_a_b"], pw["gate_x_b"], pw["lam"],
      pw["w_pool"], pw["pool_scale"])


def _cache_update_kernel(ck_ref, cv_ref, kn_ref, vn_ref, ok_ref, ov_ref):
    bt = CACHE_BT
    last = lax.broadcasted_iota(jnp.int32, (KV_WIDTH, WINDOW), 1) == WINDOW - 1
    pad = jnp.zeros((LANES - bt, KV_WIDTH), F32)
    for src, new, dst in ((ck_ref, kn_ref, ok_ref), (cv_ref, vn_ref, ov_ref)):
        new_t = jnp.concatenate([new[...], pad], axis=0).T
        for s in range(bt):
            shifted = pltpu.roll(src[s], WINDOW - 1, 1)
            col = pltpu.roll(new_t, WINDOW - 1 - s, 1)
            dst[s] = jnp.where(last, col, shifted)


def cache_update(ck_t, cv_t, k_new, v_new):
    depth, n = k_new.shape[:2]
    bt = CACHE_BT
    cache_spec = pl.BlockSpec((None, bt, KV_WIDTH, WINDOW), lambda l, i: (l, i, 0, 0))
    new_spec = pl.BlockSpec((None, bt, KV_WIDTH), lambda l, i: (l, i, 0))
    shape = jax.ShapeDtypeStruct(ck_t.shape, F32)
    return pl.pallas_call(
        _cache_update_kernel,
        grid=(depth, n // bt),
        in_specs=[cache_spec, cache_spec, new_spec, new_spec],
        out_specs=[cache_spec, cache_spec],
        out_shape=[shape, shape],
        compiler_params=_cparams(2),
        name="cache_update",
    )(ck_t, cv_t, k_new, v_new)


def _out_ffn_kernel(l, x_ref, at_ref, rp_ref, wo_ref, g1_ref, b1_ref,
                    w1_ref, w2_ref, g2_ref, b2_ref, y_ref, acc_ref):
    tm = x_ref.shape[0]
    sub = min(FFN_SUB, tm)
    groups = [slice(g * sub, (g + 1) * sub) for g in range(tm // sub)]
    mixes = []
    for rows in groups:
        mix = jnp.dot(at_ref[rows, :], wo_ref[:ATTN_WIDTH, :], preferred_element_type=F32)
        mixes.append(mix + jnp.dot(rp_ref[rows, :], wo_ref[ATTN_WIDTH:, :], preferred_element_type=F32))
    x1s = [_layer_norm(ALPHA * x_ref[rows, :] + mix, g1_ref[l:l + 1, :], b1_ref[l:l + 1, :])
           for rows, mix in zip(groups, mixes)]
    x1bs = [x1.astype(BF16) for x1 in x1s]
    for c in range(D_FF // FFN_FC):
        cols = slice(c * FFN_FC, (c + 1) * FFN_FC)
        for rows, x1b in zip(groups, x1bs):
            hid = jnp.dot(x1b, w1_ref[:, cols], preferred_element_type=F32)
            hid = jnp.square(jnp.maximum(hid, 0.0)).astype(BF16)
            part = jnp.dot(hid, w2_ref[cols, :], preferred_element_type=F32)
            if c == 0:
                acc_ref[rows, :] = part
            else:
                acc_ref[rows, :] += part
    for rows, x1 in zip(groups, x1s):
        y_ref[rows, :] = _layer_norm(ALPHA * x1 + acc_ref[rows, :], g2_ref[l:l + 1, :], b2_ref[l:l + 1, :])


def out_ffn(l, x2d, attn2d, rp2d, fw, tm):
    n = x2d.shape[0]
    row = lambda width: pl.BlockSpec((tm, width), lambda i: (i, 0))
    vec = _whole_spec((DEPTH, D_MODEL))
    return pl.pallas_call(
        functools.partial(_out_ffn_kernel, l),
        grid=(n // tm,),
        in_specs=[
            row(D_MODEL), row(ATTN_WIDTH), row(LRU_WIDTH + POOL_WIDTH),
            _layer_spec(fw["w_out"].shape, l), vec, vec,
            _layer_spec(fw["w_ff1"].shape, l), _layer_spec(fw["w_ff2"].shape, l), vec, vec,
        ],
        out_specs=row(D_MODEL),
        out_shape=jax.ShapeDtypeStruct((n, D_MODEL), F32),
        scratch_shapes=[pltpu.VMEM((tm, D_MODEL), F32)],
        compiler_params=_cparams(1),
        name="out_ffn",
    )(x2d, attn2d, rp2d, fw["w_out"], fw["ln1_g"], fw["ln1_b"],
      fw["w_ff1"], fw["w_ff2"], fw["ln2_g"], fw["ln2_b"])


def _block_diag(w):
    depth, g, c, d = w.shape
    eye = jnp.eye(g, dtype=bool)[None, :, None, :, None]
    return jnp.where(eye, w[:, :, :, None, :], 0.0).reshape(depth, g * c, g * d)


def _alibi_slopes():
    return jnp.exp2(-8.0 * (jnp.arange(N_HEADS, dtype=F32) + 1.0) / N_HEADS)


def _prompt_bias_tables():
    slopes = _alibi_slopes()
    jk = jnp.arange(2 * WINDOW)[:, None]
    tq = jnp.arange(WINDOW)[None, :]
    delta = tq + WINDOW - jk
    visible = (delta >= 0) & (delta <= WINDOW)
    bias = -slopes[None, :, None] * delta.astype(F32)[:, None, :]
    full = jnp.where(visible[:, None, :], bias, NEG_INF)
    first = jnp.where((visible & (jk >= WINDOW))[:, None, :], bias, NEG_INF)
    return jnp.stack([first, full]).reshape(2, 2 * WINDOW, N_HEADS * WINDOW)


def kernel(x_prompt, x_sample, cache_k, cache_v, state_h, state_conv, state_pool, w_in, attn_sinks, conv_w, conv_b, gate_a_w, gate_a_b, gate_x_w, gate_x_b, lru_lambda, pool_w, pool_scale, w_out, ln1_g, ln1_b, w_ff1, w_ff2, ln2_g, ln2_b):
    batch, seq, _ = x_prompt.shape
    dec = x_sample.shape[0]
    bias_prompt = _prompt_bias_tables()
    bias_sample = -_alibi_slopes()[:, None] * (WINDOW - jnp.arange(WINDOW, dtype=F32))[None, :]
    sink_rows = jnp.repeat(attn_sinks, WINDOW, axis=1)
    sinks_t = attn_sinks.T

    pw = {
        "conv_w": conv_w, "conv_b": conv_b,
        "w_gate": jnp.concatenate([_block_diag(gate_a_w), _block_diag(gate_x_w)], axis=2).astype(BF16),
        "gate_a_b": gate_a_b, "gate_x_b": gate_x_b, "lam": lru_lambda,
        "w_pool": _block_diag(pool_w).astype(BF16), "pool_scale": pool_scale,
    }
    fw = {
        "w_out": w_out.astype(BF16), "w_ff1": w_ff1.astype(BF16), "w_ff2": w_ff2.astype(BF16),
        "ln1_g": ln1_g, "ln1_b": ln1_b, "ln2_g": ln2_g, "ln2_b": ln2_b,
    }

    def kv_view(c):
        return jnp.transpose(c, (0, 1, 3, 4, 2)).reshape(c.shape[0], c.shape[1], KV_WIDTH, WINDOW)

    def kv_unview(c_t):
        c5 = c_t.reshape(c_t.shape[0], c_t.shape[1], N_KV_HEADS, HEAD_DIM, WINDOW)
        return jnp.transpose(c5, (0, 1, 4, 2, 3))

    ck_t, cv_t = kv_view(cache_k), kv_view(cache_v)
    state_conv_t = jnp.swapaxes(state_conv, 1, 2)
    state_pool_t = jnp.swapaxes(state_pool, 1, 2)

    yp = x_prompt.reshape(batch * seq, D_MODEL)
    ys = x_sample.reshape(dec, D_MODEL)
    xr_cols = slice(XR_BLK * LRU_WIDTH, (XR_BLK + 1) * LRU_WIDTH)
    zp_cols = slice(ZP_BLK * LRU_WIDTH, (ZP_BLK + 1) * LRU_WIDTH)
    outs = {k: [] for k in ("pk", "pv", "ph", "pc", "pp", "kn", "vn", "sh", "sc", "sp")}
    for l in range(DEPTH):
        attn, rp, h_last, k_last_t, v_last_t, x_tail, z_tail = front_prompt(
            l, yp, seq, w_in, bias_prompt, sink_rows, pw)
        yp = out_ffn(l, yp, attn, rp, fw, FFN_TM)
        outs["pk"].append(k_last_t)
        outs["pv"].append(v_last_t)
        outs["ph"].append(h_last.reshape(batch, LRU_WIDTH))
        outs["pc"].append(x_tail[:, X_TAIL - (CONV_W - 1):, :])
        outs["pp"].append(z_tail[:, Z_TAIL - POOL_CTX:, :])

        qs, rs = in_proj(l, ys, w_in)
        attn_s, rp_s, h_s = mixer_sample(
            l, qs.reshape(dec, GQA_GROUP, LANES), rs, ck_t, cv_t, state_h, state_conv_t, state_pool_t,
            bias_sample, sinks_t, pw)
        attn_s = attn_s.reshape(dec, N_HEADS, HEAD_DIM)[:, HEAD_ORDER_INV, :].reshape(dec, ATTN_WIDTH)
        ys = out_ffn(l, ys, attn_s, rp_s, fw, dec)
        outs["kn"].append(rs[:, K_BLK * KV_WIDTH:(K_BLK + 1) * KV_WIDTH])
        outs["vn"].append(rs[:, V_BLK * KV_WIDTH:(V_BLK + 1) * KV_WIDTH])
        outs["sh"].append(h_s)
        outs["sc"].append(jnp.concatenate([state_conv[l][:, 1:], rs[:, None, xr_cols]], axis=1))
        outs["sp"].append(jnp.concatenate([state_pool[l][:, 1:], rs[:, None, zp_cols]], axis=1))

    st = {k: jnp.stack(v) for k, v in outs.items()}
    sk_t, sv_t = cache_update(ck_t, cv_t, st["kn"], st["vn"])
    return (yp.reshape(batch, seq, D_MODEL), ys.reshape(dec, 1, D_MODEL),
            kv_unview(st["pk"]), kv_unview(st["pv"]), st["ph"], st["pc"], st["pp"],
            kv_unview(sk_t), kv_unview(sv_t), st["sh"], st["sc"], st["sp"])
```

```python
import functools

import jax
import jax.numpy as jnp
import numpy as np
from jax import lax
from jax.experimental import pallas as pl
from jax.experimental.pallas import tpu as pltpu

D_MODEL = 1024
DEPTH = 2
HEAD_DIM = 64
ATTN_WIDTH = 512
N_HEADS = 8
N_KV_HEADS = 2
GQA_GROUP = 4
KV_WIDTH = 128
WINDOW = 128
LRU_WIDTH = 256
LRU_C = 8.0
CONV_W = 4
POOL_WINDOWS = (2, 4, 8, 16)
POOL_WIDTH = 256
POOL_GROUP_W = 64
POOL_CTX = 15
MIX_WIDTH = ATTN_WIDTH + LRU_WIDTH + POOL_WIDTH
IN_WIDTH = 1536
REST_WIDTH = IN_WIDTH - ATTN_WIDTH
D_FF = 4096
LN_EPS = 1e-5
NEG_INF = -1e30
ALPHA = (2.0 * DEPTH) ** 0.25
Q_SCALE = HEAD_DIM ** -0.5

K_BLK, V_BLK = 0, 1
XR_BLK, GR_BLK, ZP_BLK = 1, 2, 3

LANES = 128
SUBLANES = 8
VMEM_LIMIT_BYTES = 56 * 1024 * 1024

TILE = 512
FFN_TM = 1024
FFN_FC = 1024
FFN_SUB = 256
X_TAIL = SUBLANES
Z_TAIL = 2 * SUBLANES
SAMPLE_BT = 16
CACHE_BT = 32

BF16 = jnp.bfloat16
F32 = jnp.float32

HEAD_ORDER = (0, 4, 1, 5, 2, 6, 3, 7)
HEAD_ORDER_INV = tuple(int(i) for i in np.argsort(HEAD_ORDER))

PROJ_COLS = ((0, ATTN_WIDTH), (ATTN_WIDTH, ATTN_WIDTH + 2 * KV_WIDTH),
             (ATTN_WIDTH + 2 * KV_WIDTH, ATTN_WIDTH + 2 * KV_WIDTH + 2 * LRU_WIDTH),
             (ATTN_WIDTH + 2 * KV_WIDTH + 2 * LRU_WIDTH, IN_WIDTH))


def _cparams(n_grid):
    return pltpu.CompilerParams(
        dimension_semantics=("arbitrary",) * n_grid,
        vmem_limit_bytes=VMEM_LIMIT_BYTES,
    )


def _whole_spec(shape):
    nd = len(shape)
    return pl.BlockSpec(shape, lambda *_: (0,) * nd, pipeline_mode=pl.Buffered(1))


def _layer_spec(shape, l):
    nd = len(shape) - 1
    return pl.BlockSpec((None,) + tuple(shape[1:]), lambda *_: (l,) + (0,) * nd, pipeline_mode=pl.Buffered(1))


def _layer_norm(x, g, b):
    mu = jnp.mean(x, axis=-1, keepdims=True)
    xc = x - mu
    var = jnp.mean(xc * xc, axis=-1, keepdims=True)
    return xc * lax.rsqrt(var + LN_EPS) * g + b


def _gelu_tanh(x):
    return 0.5 * x * (1.0 + jnp.tanh(np.sqrt(2.0 / np.pi) * (x + 0.044715 * (x * x * x))))


def _sigmoid(x):
    return 1.0 / (1.0 + jnp.exp(-x))


def _softplus(x):
    return jnp.maximum(x, 0.0) + jnp.log(1.0 + jnp.exp(-jnp.abs(x)))


def _interleave(*gens):
    active = list(gens)
    while active:
        for g in list(active):
            try:
                next(g)
            except StopIteration:
                active.remove(g)


def _attn_scores(q, k2, bias_t, lo):
    zero = jnp.zeros((), q.dtype)
    tiles = [q[:, c * LANES:(c + 1) * LANES] for c in range(GQA_GROUP)]
    qs = jnp.concatenate([jnp.where(lo, t, zero) for t in tiles]
                         + [jnp.where(lo, zero, t) for t in tiles], axis=0)
    return lax.dot_general(k2, qs, (((1,), (1,)), ((), ())), preferred_element_type=F32) + bias_t


def _attn_probs(s, sink):
    m = jnp.maximum(jnp.max(s, axis=0, keepdims=True), sink)
    p = jnp.exp(s - m)
    denom = jnp.sum(p, axis=0, keepdims=True) + jnp.exp(sink - m)
    return p.astype(BF16), 1.0 / denom


def _attn_values(p, v2):
    return lax.dot_general(v2, p, (((0,), (0,)), ((), ())), preferred_element_type=F32)


def _attn_output(o, inv_denom):
    o = o * inv_denom
    cols = []
    for c in range(ATTN_WIDTH // LANES):
        kv = (2 * c) // GQA_GROUP
        rows = slice(kv * HEAD_DIM, (kv + 1) * HEAD_DIM)
        blk = jnp.concatenate([o[rows, (2 * c) * WINDOW:(2 * c + 1) * WINDOW],
                               o[rows, (2 * c + 1) * WINDOW:(2 * c + 2) * WINDOW]], axis=0)
        cols.append(blk.T)
    return jnp.concatenate(cols, axis=1)


def _conv(l, xr, x_tail, cw_ref, cb_ref):
    xe = jnp.concatenate([x_tail, xr], axis=0)
    cw = cw_ref[...]
    xc = cb_ref[l:l + 1, :] + xr * cw[CONV_W - 1:CONV_W, :]
    for tap in range(CONV_W - 1):
        d = CONV_W - 1 - tap
        xc = xc + _shift_rows(xe, d)[X_TAIL:] * cw[tap:tap + 1, :]
    return xc


def _gate_logits(xc, wg_ref):
    return jnp.dot(xc.astype(BF16), wg_ref[...], preferred_element_type=F32)


def _gates(l, g, xc, ba_ref, bx_ref, lam_ref):
    r = _sigmoid(g[:, :LRU_WIDTH] + ba_ref[l:l + 1, :])
    i = _sigmoid(g[:, LRU_WIDTH:] + bx_ref[l:l + 1, :])
    log_a = (-LRU_C * r) * _softplus(-lam_ref[l:l + 1, :])
    a = jnp.exp(log_a)
    b = jnp.sqrt(1.0 - a * a) * (i * xc)
    return a, b


def _lru_scan(a, b, h_prev):
    t, w = a.shape
    groups = t // SUBLANES
    a3 = a.reshape(groups, SUBLANES, w)
    b3 = b.reshape(groups, SUBLANES, w)
    row = lax.broadcasted_iota(jnp.int32, (groups, SUBLANES, w), 1)
    s = 1
    while s < SUBLANES:
        keep = row >= s
        a_sh = jnp.where(keep, pltpu.roll(a3, s, 1), 1.0)
        b_sh = jnp.where(keep, pltpu.roll(b3, s, 1), 0.0)
        b3 = a3 * b_sh + b3
        a3 = a3 * a_sh
        s *= 2
    hs = []
    h = h_prev
    for g in range(groups):
        hg = a3[g] * h + b3[g]
        hs.append(hg)
        h = hg[SUBLANES - 1:SUBLANES, :]
    return jnp.concatenate(hs, axis=0), h


def _shift_rows(x, d):
    return pltpu.roll(x, d, 0)


def _pool_means(zp, z_tail, pos1):
    ze = jnp.concatenate([z_tail, zp], axis=0)
    t = zp.shape[0]
    lo = lax.broadcasted_iota(jnp.int32, (t, LANES), 1) < POOL_GROUP_W
    s2 = ze + _shift_rows(ze, 1)
    s4 = s2 + _shift_rows(s2, 2)
    hi4 = s4[:, LANES:]
    s8 = hi4 + _shift_rows(hi4, 4)
    s16 = s8 + _shift_rows(s8, 8)
    wins = (jnp.where(lo, s2[Z_TAIL:, :LANES], s4[Z_TAIL:, :LANES]),
            jnp.where(lo, s8[Z_TAIL:], s16[Z_TAIL:]))
    means = []
    for c, win in enumerate(wins):
        w_small, w_big = POOL_WINDOWS[2 * c], POOL_WINDOWS[2 * c + 1]
        count = jnp.minimum(pos1, jnp.where(lo, float(w_small), float(w_big)))
        means.append(win / count - zp[:, c * LANES:(c + 1) * LANES])
    return jnp.concatenate(means, axis=1)


def _prep_w_in(w_ref, wb_ref):
    lo = lax.broadcasted_iota(jnp.int32, (D_MODEL, LANES), 1) < HEAD_DIM
    src = [w_ref[:, s * LANES:(s + 1) * LANES] * Q_SCALE for s in range(ATTN_WIDTH // LANES)]
    swapped = [pltpu.roll(t, HEAD_DIM, 1) for t in src]
    for c in range(GQA_GROUP):
        s0, s1 = c // 2, GQA_GROUP // 2 + c // 2
        if c % 2 == 0:
            tile = jnp.where(lo, src[s0], swapped[s1])
        else:
            tile = jnp.where(lo, swapped[s0], src[s1])
        wb_ref[:, c * LANES:(c + 1) * LANES] = tile.astype(BF16)
    wb_ref[:, ATTN_WIDTH:] = w_ref[:, ATTN_WIDTH:].astype(BF16)


def _front_kernel(l, tiles_per_seq, xp_ref, xs_ref, w_ref, bias0_ref, bias_ref, sink_ref,
                  cw_ref, cb_ref, wg_ref, ba_ref, bx_ref, lam_ref, wp_ref, ps_ref,
                  attn_ref, rp_ref, h_ref, kt_ref, vt_ref, xt_ref, zt_ref, qs_ref, rs_ref,
                  wb_ref, q_scr, r_scr, kprev, vprev, xtail, ztail, hcar):
    s = pl.program_id(0)
    slot_p = lax.rem(s, 2)
    slot_m = 1 - slot_p
    j = lax.rem(s + tiles_per_seq - 1, tiles_per_seq)

    @pl.when(s == 0)
    def _():
        _prep_w_in(w_ref, wb_ref)
        us = jnp.dot(xs_ref[...].astype(BF16), wb_ref[...], preferred_element_type=F32)
        qs_ref[...] = us[:, :ATTN_WIDTH].astype(BF16)
        rs_ref[...] = us[:, ATTN_WIDTH:]
        q_scr[1] = jnp.zeros(q_scr.shape[1:], q_scr.dtype)
        r_scr[1] = jnp.zeros(r_scr.shape[1:], r_scr.dtype)
        kprev[...] = jnp.zeros_like(kprev)
        vprev[...] = jnp.zeros_like(vprev)

    @pl.when((j == 0) | (s == 0))
    def _():
        xtail[...] = jnp.zeros_like(xtail)
        ztail[...] = jnp.zeros_like(ztail)
        hcar[...] = jnp.zeros_like(hcar)

    carry = {}
    sub_blocks = [slice(c * WINDOW, (c + 1) * WINDOW) for c in range(TILE // WINDOW)]

    def attention():
        lo = lax.broadcasted_iota(jnp.int32, (WINDOW, LANES), 1) < HEAD_DIM
        sink = sink_ref[l:l + 1, :]
        xb16 = xp_ref[...].astype(BF16)
        k_prev, v_prev = kprev[...], vprev[...]
        k = v = None
        for c, rows in enumerate(sub_blocks):
            k = r_scr[slot_m, rows, K_BLK * KV_WIDTH:(K_BLK + 1) * KV_WIDTH]
            v = r_scr[slot_m, rows, V_BLK * KV_WIDTH:(V_BLK + 1) * KV_WIDTH]
            kb, vb = k.astype(BF16), v.astype(BF16)
            bias_t = bias0_ref[...] if c == 0 else bias_ref[...]
            sc = _attn_scores(q_scr[slot_m, rows, :], jnp.concatenate([k_prev, kb], axis=0), bias_t, lo)
            yield
            c0, c1 = PROJ_COLS[c]
            u = jnp.dot(xb16, wb_ref[:, c0:c1], preferred_element_type=F32)
            if c == 0:
                q_scr[slot_p] = u.astype(BF16)
            else:
                r_scr[slot_p, :, c0 - ATTN_WIDTH:c1 - ATTN_WIDTH] = u
            yield
            p, inv_denom = _attn_probs(sc, sink)
            o = _attn_values(p, jnp.concatenate([v_prev, vb], axis=0))
            yield
            attn_ref[rows, :] = _attn_output(o, inv_denom).astype(attn_ref.dtype)
            k_prev, v_prev = kb, vb
        carry.update(k_prev=k_prev, v_prev=v_prev, k=k, v=v)

    def mixers():
        x_tail, z_tail, h = xtail[...], ztail[...], hcar[...]
        for c, rows in enumerate(sub_blocks):
            xr = r_scr[slot_m, rows, XR_BLK * LRU_WIDTH:(XR_BLK + 1) * LRU_WIDTH]
            gr = r_scr[slot_m, rows, GR_BLK * LRU_WIDTH:(GR_BLK + 1) * LRU_WIDTH]
            zp = r_scr[slot_m, rows, ZP_BLK * LRU_WIDTH:(ZP_BLK + 1) * LRU_WIDTH]
            xc = _conv(l, xr, x_tail, cw_ref, cb_ref)
            g = _gate_logits(xc, wg_ref)
            yield
            a, b = _gates(l, g, xc, ba_ref, bx_ref, lam_ref)
            hs, h = _lru_scan(a, b, h)
            rec = hs * _gelu_tanh(gr)
            pos1 = (j * TILE + c * WINDOW + lax.broadcasted_iota(jnp.int32, (WINDOW, LANES), 0) + 1).astype(F32)
            diff = _pool_means(zp, z_tail, pos1).astype(BF16)
            pool = jnp.dot(diff, wp_ref[...], preferred_element_type=F32) * ps_ref[l:l + 1, :]
            yield
            rp_ref[rows, :] = jnp.concatenate([rec, pool], axis=1).astype(rp_ref.dtype)
            x_tail, z_tail = xr[WINDOW - X_TAIL:, :], zp[WINDOW - Z_TAIL:, :]
        carry.update(x_tail=x_tail, z_tail=z_tail, h=h)

    _interleave(attention(), mixers())

    kprev[...] = carry["k_prev"]
    vprev[...] = carry["v_prev"]
    xtail[...] = carry["x_tail"]
    ztail[...] = carry["z_tail"]
    hcar[...] = carry["h"]

    @pl.when(j == tiles_per_seq - 1)
    def _():
        kt_ref[...] = carry["k"].T
        vt_ref[...] = carry["v"].T
        h_ref[...] = carry["h"]
        xt_ref[...] = carry["x_tail"]
        zt_ref[...] = carry["z_tail"]


def front_prompt(l, x2d, seq, xs, w_in, bias2, sink_rows, pw):
    n = x2d.shape[0]
    n_s = xs.shape[0]
    b = n // seq
    n_tiles = n // TILE
    tiles_per_seq = seq // TILE
    done = lambda s: jnp.maximum(s - 1, 0)
    seq_of = lambda s: done(s) // tiles_per_seq
    bias_shape = (None, 2 * WINDOW, N_HEADS * WINDOW)
    per_seq = lambda rows, width: pl.BlockSpec((None, rows, width), lambda s: (seq_of(s), 0, 0))
    return pl.pallas_call(
        functools.partial(_front_kernel, l, tiles_per_seq),
        grid=(n_tiles + 1,),
        in_specs=[
            pl.BlockSpec((TILE, D_MODEL), lambda s: (jnp.minimum(s, n_tiles - 1), 0)),
            _whole_spec(xs.shape),
            _layer_spec(w_in.shape, l),
            pl.BlockSpec(bias_shape, lambda s: (jnp.minimum(done(s) % tiles_per_seq, 1), 0, 0)),
            pl.BlockSpec(bias_shape, lambda s: (1, 0, 0), pipeline_mode=pl.Buffered(1)),
            _whole_spec(sink_rows.shape),
            _layer_spec(pw["conv_w"].shape, l), _whole_spec(pw["conv_b"].shape),
            _layer_spec(pw["w_gate"].shape, l), _whole_spec(pw["gate_a_b"].shape),
            _whole_spec(pw["gate_x_b"].shape), _whole_spec(pw["lam"].shape),
            _layer_spec(pw["w_pool"].shape, l), _whole_spec(pw["pool_scale"].shape),
        ],
        out_specs=[
            pl.BlockSpec((TILE, ATTN_WIDTH), lambda s: (done(s), 0)),
            pl.BlockSpec((TILE, LRU_WIDTH + POOL_WIDTH), lambda s: (done(s), 0)),
            per_seq(1, LRU_WIDTH),
            per_seq(KV_WIDTH, WINDOW), per_seq(KV_WIDTH, WINDOW),
            per_seq(X_TAIL, LRU_WIDTH), per_seq(Z_TAIL, POOL_WIDTH),
            pl.BlockSpec((n_s, ATTN_WIDTH), lambda s: (0, 0)),
            pl.BlockSpec((n_s, REST_WIDTH), lambda s: (0, 0)),
        ],
        out_shape=[
            jax.ShapeDtypeStruct((n, ATTN_WIDTH), BF16),
            jax.ShapeDtypeStruct((n, LRU_WIDTH + POOL_WIDTH), BF16),
            jax.ShapeDtypeStruct((b, 1, LRU_WIDTH), F32),
            jax.ShapeDtypeStruct((b, KV_WIDTH, WINDOW), F32),
            jax.ShapeDtypeStruct((b, KV_WIDTH, WINDOW), F32),
            jax.ShapeDtypeStruct((b, X_TAIL, LRU_WIDTH), F32),
            jax.ShapeDtypeStruct((b, Z_TAIL, POOL_WIDTH), F32),
            jax.ShapeDtypeStruct((n_s, ATTN_WIDTH), BF16),
            jax.ShapeDtypeStruct((n_s, REST_WIDTH), F32),
        ],
        scratch_shapes=[
            pltpu.VMEM((D_MODEL, IN_WIDTH), BF16),
            pltpu.VMEM((2, TILE, ATTN_WIDTH), BF16),
            pltpu.VMEM((2, TILE, REST_WIDTH), F32),
            pltpu.VMEM((WINDOW, KV_WIDTH), BF16),
            pltpu.VMEM((WINDOW, KV_WIDTH), BF16),
            pltpu.VMEM((X_TAIL, LRU_WIDTH), F32),
            pltpu.VMEM((Z_TAIL, POOL_WIDTH), F32),
            pltpu.VMEM((1, LRU_WIDTH), F32),
        ],
        compiler_params=_cparams(1),
        name="front_prompt",
    )(x2d, xs, w_in, bias2, bias2, sink_rows, pw["conv_w"], pw["conv_b"], pw["w_gate"], pw["gate_a_b"],
      pw["gate_x_b"], pw["lam"], pw["w_pool"], pw["pool_scale"])


def _mixer_sample_kernel(l, q_ref, kn_ref, vn_ref, xr_ref, gr_ref, zp_ref, ck_ref, cv_ref,
                         h0_ref, sc_ref, sp_ref, bias_ref, sink_ref,
                         cw_ref, cb_ref, wg_ref, ba_ref, bx_ref, lam_ref, wp_ref, ps_ref,
                         attn_ref, rp_ref, h_ref):
    bt = SAMPLE_BT
    lo3 = lax.broadcasted_iota(jnp.int32, (bt, GQA_GROUP, LANES), 2) < HEAD_DIM
    q4 = q_ref[...].astype(F32)
    qm = jnp.concatenate([jnp.where(lo3, q4, 0.0), jnp.where(lo3, 0.0, q4)], axis=1)
    s = jnp.einsum("bqc,bck->bqk", qm.astype(BF16), ck_ref[...].astype(BF16),
                   preferred_element_type=F32) + bias_ref[...]
    s_new = jnp.sum(qm * kn_ref[...][:, None, :], axis=2, keepdims=True)
    sink = sink_ref[:, l:l + 1]
    m = jnp.maximum(jnp.maximum(jnp.max(s, axis=2, keepdims=True), s_new), sink)
    p = jnp.exp(s - m)
    p_new = jnp.exp(s_new - m)
    denom = jnp.sum(p, axis=2, keepdims=True) + p_new + jnp.exp(sink - m)
    o = jnp.einsum("bqk,bck->bqc", p.astype(BF16), cv_ref[...].astype(BF16),
                   preferred_element_type=F32)
    o = (o + p_new * vn_ref[...][:, None, :]) / denom
    attn_ref[...] = jnp.where(lo3, o[:, :GQA_GROUP, :], o[:, GQA_GROUP:, :]).astype(attn_ref.dtype)

    xr = xr_ref[...]
    cw = cw_ref[...]
    xc = cb_ref[l:l + 1, :] + xr * cw[CONV_W - 1:CONV_W, :]
    for tap in range(CONV_W - 1):
        xc = xc + sc_ref[tap] * cw[tap:tap + 1, :]
    a, b = _gates(l, _gate_logits(xc, wg_ref), xc, ba_ref, bx_ref, lam_ref)
    h = a * h0_ref[...] + b
    h_ref[...] = h
    rec = h * _gelu_tanh(gr_ref[...])

    z = zp_ref[...]
    lo = lax.broadcasted_iota(jnp.int32, (bt, LANES), 1) < POOL_GROUP_W
    means = []
    for c in range(POOL_WIDTH // LANES):
        w_small, w_big = POOL_WINDOWS[2 * c], POOL_WINDOWS[2 * c + 1]
        cols = slice(c * LANES, (c + 1) * LANES)
        zc = z[:, cols]
        acc = zc
        small = None
        for d in range(1, w_big):
            acc = acc + sp_ref[POOL_CTX - d][:, cols]
            if d + 1 == w_small:
                small = acc
        win = jnp.where(lo, small, acc)
        count = jnp.where(lo, float(w_small), float(w_big))
        means.append(win / count - zc)
    diff = jnp.concatenate(means, axis=1).astype(BF16)
    pool = jnp.dot(diff, wp_ref[...], preferred_element_type=F32) * ps_ref[l:l + 1, :]
    rp_ref[...] = jnp.concatenate([rec, pool], axis=1).astype(rp_ref.dtype)


def mixer_sample(l, q4, rs, ck_t, cv_t, state_h, state_conv_t, state_pool_t, bias_s, sinks_t, pw):
    n = rs.shape[0]
    bt = SAMPLE_BT
    cache_spec = pl.BlockSpec((None, bt, KV_WIDTH, WINDOW), lambda i: (l, i, 0, 0))
    rcol = lambda width, c: pl.BlockSpec((bt, width), lambda i: (i, c))
    return pl.pallas_call(
        functools.partial(_mixer_sample_kernel, l),
        grid=(n // bt,),
        in_specs=[
            pl.BlockSpec((bt, GQA_GROUP, LANES), lambda i: (i, 0, 0)),
            rcol(KV_WIDTH, K_BLK), rcol(KV_WIDTH, V_BLK),
            rcol(LRU_WIDTH, XR_BLK), rcol(LRU_WIDTH, GR_BLK), rcol(POOL_WIDTH, ZP_BLK),
            cache_spec, cache_spec,
            pl.BlockSpec((None, bt, LRU_WIDTH), lambda i: (l, i, 0)),
            pl.BlockSpec((None, CONV_W - 1, bt, LRU_WIDTH), lambda i: (l, 0, i, 0)),
            pl.BlockSpec((None, POOL_CTX, bt, POOL_WIDTH), lambda i: (l, 0, i, 0)),
            _whole_spec(bias_s.shape),
            _whole_spec(sinks_t.shape),
            _layer_spec(pw["conv_w"].shape, l), _whole_spec(pw["conv_b"].shape),
            _layer_spec(pw["w_gate"].shape, l), _whole_spec(pw["gate_a_b"].shape),
            _whole_spec(pw["gate_x_b"].shape), _whole_spec(pw["lam"].shape),
            _layer_spec(pw["w_pool"].shape, l), _whole_spec(pw["pool_scale"].shape),
        ],
        out_specs=[
            pl.BlockSpec((bt, GQA_GROUP, LANES), lambda i: (i, 0, 0)),
            pl.BlockSpec((bt, LRU_WIDTH + POOL_WIDTH), lambda i: (i, 0)),
            pl.BlockSpec((bt, LRU_WIDTH), lambda i: (i, 0)),
        ],
        out_shape=[
            jax.ShapeDtypeStruct((n, GQA_GROUP, LANES), BF16),
            jax.ShapeDtypeStruct((n, LRU_WIDTH + POOL_WIDTH), BF16),
            jax.ShapeDtypeStruct((n, LRU_WIDTH), F32),
        ],
        compiler_params=_cparams(1),
        name="mixer_sample",
    )(q4, rs, rs, rs, rs, rs, ck_t, cv_t, state_h, state_conv_t, state_pool_t, bias_s, sinks_t,
      pw["conv_w"], pw["conv_b"], pw["w_gate"], pw["gate_a_b"], pw["gate_x_b"], pw["lam"],
      pw["w_pool"], pw["pool_scale"])


def _cache_update_kernel(ck_ref, cv_ref, kn_ref, vn_ref, ok_ref, ov_ref):
    bt = CACHE_BT
    last = lax.broadcasted_iota(jnp.int32, (KV_WIDTH, WINDOW), 1) == WINDOW - 1
    pad = jnp.zeros((LANES - bt, KV_WIDTH), F32)
    for src, new, dst in ((ck_ref, kn_ref, ok_ref), (cv_ref, vn_ref, ov_ref)):
        new_t = jnp.concatenate([new[...], pad], axis=0).T
        for s in range(bt):
            shifted = pltpu.roll(src[s], WINDOW - 1, 1)
            col = pltpu.roll(new_t, WINDOW - 1 - s, 1)
            dst[s] = jnp.where(last, col, shifted)


def cache_update(ck_t, cv_t, k_new, v_new):
    depth, n = k_new.shape[:2]
    bt = CACHE_BT
    cache_spec = pl.BlockSpec((None, bt, KV_WIDTH, WINDOW), lambda l, i: (l, i, 0, 0))
    new_spec = pl.BlockSpec((None, bt, KV_WIDTH), lambda l, i: (l, i, 0))
    shape = jax.ShapeDtypeStruct(ck_t.shape, F32)
    return pl.pallas_call(
        _cache_update_kernel,
        grid=(depth, n // bt),
        in_specs=[cache_spec, cache_spec, new_spec, new_spec],
        out_specs=[cache_spec, cache_spec],
        out_shape=[shape, shape],
        compiler_params=_cparams(2),
        name="cache_update",
    )(ck_t, cv_t, k_new, v_new)


def _out_ffn_kernel(l, x_ref, at_ref, rp_ref, xs_ref, ats_ref, rps_ref, wo_ref, g1_ref, b1_ref,
                    w1_ref, w2_ref, g2_ref, b2_ref, y_ref, ys_ref, acc_ref):
    @pl.when(pl.program_id(0) == pl.num_programs(0) - 1)
    def _():
        mix = jnp.dot(ats_ref[...], wo_ref[:ATTN_WIDTH, :], preferred_element_type=F32)
        mix = mix + jnp.dot(rps_ref[...], wo_ref[ATTN_WIDTH:, :], preferred_element_type=F32)
        x1 = _layer_norm(ALPHA * xs_ref[...] + mix, g1_ref[l:l + 1, :], b1_ref[l:l + 1, :])
        hid = jnp.dot(x1.astype(BF16), w1_ref[...], preferred_element_type=F32)
        hid = jnp.square(jnp.maximum(hid, 0.0)).astype(BF16)
        ffn = jnp.dot(hid, w2_ref[...], preferred_element_type=F32)
        ys_ref[...] = _layer_norm(ALPHA * x1 + ffn, g2_ref[l:l + 1, :], b2_ref[l:l + 1, :])

    tm = x_ref.shape[0]
    sub = FFN_SUB
    groups = [slice(g * sub, (g + 1) * sub) for g in range(tm // sub)]
    mixes = []
    for rows in groups:
        mix = jnp.dot(at_ref[rows, :], wo_ref[:ATTN_WIDTH, :], preferred_element_type=F32)
        mixes.append(mix + jnp.dot(rp_ref[rows, :], wo_ref[ATTN_WIDTH:, :], preferred_element_type=F32))
    x1s = [_layer_norm(ALPHA * x_ref[rows, :] + mix, g1_ref[l:l + 1, :], b1_ref[l:l + 1, :])
           for rows, mix in zip(groups, mixes)]
    x1bs = [x1.astype(BF16) for x1 in x1s]
    for c in range(D_FF // FFN_FC):
        cols = slice(c * FFN_FC, (c + 1) * FFN_FC)
        for rows, x1b in zip(groups, x1bs):
            hid = jnp.dot(x1b, w1_ref[:, cols], preferred_element_type=F32)
            hid = jnp.square(jnp.maximum(hid, 0.0)).astype(BF16)
            part = jnp.dot(hid, w2_ref[cols, :], preferred_element_type=F32)
            if c == 0:
                acc_ref[rows, :] = part
            else:
                acc_ref[rows, :] += part
    for rows, x1 in zip(groups, x1s):
        y_ref[rows, :] = _layer_norm(ALPHA * x1 + acc_ref[rows, :], g2_ref[l:l + 1, :], b2_ref[l:l + 1, :])


def out_ffn(l, x2d, attn2d, rp2d, xs, attn_s, rp_s, fw):
    n = x2d.shape[0]
    tm = FFN_TM
    row = lambda width: pl.BlockSpec((tm, width), lambda i: (i, 0))
    vec = _whole_spec((DEPTH, D_MODEL))
    return pl.pallas_call(
        functools.partial(_out_ffn_kernel, l),
        grid=(n // tm,),
        in_specs=[
            row(D_MODEL), row(ATTN_WIDTH), row(LRU_WIDTH + POOL_WIDTH),
            _whole_spec(xs.shape), _whole_spec(attn_s.shape), _whole_spec(rp_s.shape),
            _layer_spec(fw["w_out"].shape, l), vec, vec,
            _layer_spec(fw["w_ff1"].shape, l), _layer_spec(fw["w_ff2"].shape, l), vec, vec,
        ],
        out_specs=[row(D_MODEL), pl.BlockSpec(xs.shape, lambda i: (0, 0))],
        out_shape=[jax.ShapeDtypeStruct((n, D_MODEL), F32), jax.ShapeDtypeStruct(xs.shape, F32)],
        scratch_shapes=[pltpu.VMEM((tm, D_MODEL), F32)],
        compiler_params=_cparams(1),
        name="out_ffn",
    )(x2d, attn2d, rp2d, xs, attn_s, rp_s, fw["w_out"], fw["ln1_g"], fw["ln1_b"],
      fw["w_ff1"], fw["w_ff2"], fw["ln2_g"], fw["ln2_b"])


def _block_diag(w):
    depth, g, c, d = w.shape
    eye = jnp.eye(g, dtype=bool)[None, :, None, :, None]
    return jnp.where(eye, w[:, :, :, None, :], 0.0).reshape(depth, g * c, g * d)


def _alibi_slopes():
    return np.exp2(-8.0 * (np.arange(N_HEADS, dtype=np.float32) + 1.0) / N_HEADS).astype(np.float32)


def _prompt_bias_tables():
    slopes = _alibi_slopes()
    jk = np.arange(2 * WINDOW)[:, None]
    tq = np.arange(WINDOW)[None, :]
    delta = tq + WINDOW - jk
    visible = (delta >= 0) & (delta <= WINDOW)
    bias = -slopes[None, :, None] * delta.astype(np.float32)[:, None, :]
    full = np.where(visible[:, None, :], bias, np.float32(NEG_INF))
    first = np.where((visible & (jk >= WINDOW))[:, None, :], bias, np.float32(NEG_INF))
    return np.stack([first, full]).reshape(2, 2 * WINDOW, N_HEADS * WINDOW).astype(np.float32)


def kernel(x_prompt, x_sample, cache_k, cache_v, state_h, state_conv, state_pool, w_in, attn_sinks, conv_w, conv_b, gate_a_w, gate_a_b, gate_x_w, gate_x_b, lru_lambda, pool_w, pool_scale, w_out, ln1_g, ln1_b, w_ff1, w_ff2, ln2_g, ln2_b):
    batch, seq, _ = x_prompt.shape
    dec = x_sample.shape[0]
    bias_prompt = jnp.asarray(_prompt_bias_tables())
    bias_sample = jnp.asarray(-_alibi_slopes()[:, None] * (WINDOW - np.arange(WINDOW, dtype=np.float32))[None, :])
    sink_rows = jnp.repeat(attn_sinks, WINDOW, axis=1)
    sinks_t = attn_sinks.T

    pw = {
        "conv_w": conv_w, "conv_b": conv_b,
        "w_gate": jnp.concatenate([_block_diag(gate_a_w), _block_diag(gate_x_w)], axis=2).astype(BF16),
        "gate_a_b": gate_a_b, "gate_x_b": gate_x_b, "lam": lru_lambda,
        "w_pool": _block_diag(pool_w).astype(BF16), "pool_scale": pool_scale,
    }
    fw = {
        "w_out": w_out.astype(BF16), "w_ff1": w_ff1.astype(BF16), "w_ff2": w_ff2.astype(BF16),
        "ln1_g": ln1_g, "ln1_b": ln1_b, "ln2_g": ln2_g, "ln2_b": ln2_b,
    }

    def kv_view(c):
        return jnp.transpose(c, (0, 1, 3, 4, 2)).reshape(c.shape[0], c.shape[1], KV_WIDTH, WINDOW)

    def kv_unview(c_t):
        c5 = c_t.reshape(c_t.shape[0], c_t.shape[1], N_KV_HEADS, HEAD_DIM, WINDOW)
        return jnp.transpose(c5, (0, 1, 4, 2, 3))

    ck_t, cv_t = kv_view(cache_k), kv_view(cache_v)
    state_conv_t = jnp.swapaxes(state_conv, 1, 2)
    state_pool_t = jnp.swapaxes(state_pool, 1, 2)

    yp = x_prompt.reshape(batch * seq, D_MODEL)
    ys = x_sample.reshape(dec, D_MODEL)
    xr_cols = slice(XR_BLK * LRU_WIDTH, (XR_BLK + 1) * LRU_WIDTH)
    zp_cols = slice(ZP_BLK * LRU_WIDTH, (ZP_BLK + 1) * LRU_WIDTH)
    outs = {k: [] for k in ("pk", "pv", "ph", "pc", "pp", "kn", "vn", "sh", "sc", "sp")}
    for l in range(DEPTH):
        attn, rp, h_last, k_last_t, v_last_t, x_tail, z_tail, qs, rs = front_prompt(
            l, yp, seq, ys, w_in, bias_prompt, sink_rows, pw)
        attn_s, rp_s, h_s = mixer_sample(
            l, qs.reshape(dec, GQA_GROUP, LANES), rs, ck_t, cv_t, state_h, state_conv_t, state_pool_t,
            bias_sample, sinks_t, pw)
        attn_s = attn_s.reshape(dec, N_HEADS, HEAD_DIM)[:, HEAD_ORDER_INV, :].reshape(dec, ATTN_WIDTH)
        yp, ys = out_ffn(l, yp, attn, rp, ys, attn_s, rp_s, fw)
        outs["pk"].append(k_last_t)
        outs["pv"].append(v_last_t)
        outs["ph"].append(h_last.reshape(batch, LRU_WIDTH))
        outs["pc"].append(x_tail[:, X_TAIL - (CONV_W - 1):, :])
        outs["pp"].append(z_tail[:, Z_TAIL - POOL_CTX:, :])
        outs["kn"].append(rs[:, K_BLK * KV_WIDTH:(K_BLK + 1) * KV_WIDTH])
        outs["vn"].append(rs[:, V_BLK * KV_WIDTH:(V_BLK + 1) * KV_WIDTH])
        outs["sh"].append(h_s)
        outs["sc"].append(jnp.concatenate([state_conv[l][:, 1:], rs[:, None, xr_cols]], axis=1))
        outs["sp"].append(jnp.concatenate([state_pool[l][:, 1:], rs[:, None, zp_cols]], axis=1))

    st = {k: jnp.stack(v) for k, v in outs.items()}
    sk_t, sv_t = cache_update(ck_t, cv_t, st["kn"], st["vn"])
    return (yp.reshape(batch, seq, D_MODEL), ys.reshape(dec, 1, D_MODEL),
            kv_unview(st["pk"]), kv_unview(st["pv"]), st["ph"], st["pc"], st["pp"],
            kv_unview(sk_t), kv_unview(sv_t), st["sh"], st["sc"], st["sp"])
```

```python
import functools

import jax
import jax.numpy as jnp
import numpy as np
from jax import lax
from jax.experimental import pallas as pl
from jax.experimental.pallas import tpu as pltpu

D_MODEL = 1024
DEPTH = 2
HEAD_DIM = 64
ATTN_WIDTH = 512
N_HEADS = 8
N_KV_HEADS = 2
GQA_GROUP = 4
KV_WIDTH = 128
WINDOW = 128
LRU_WIDTH = 256
LRU_C = 8.0
CONV_W = 4
POOL_WINDOWS = (2, 4, 8, 16)
POOL_WIDTH = 256
POOL_GROUP_W = 64
POOL_CTX = 15
MIX_WIDTH = ATTN_WIDTH + LRU_WIDTH + POOL_WIDTH
IN_WIDTH = 1536
REST_WIDTH = IN_WIDTH - ATTN_WIDTH
D_FF = 4096
LN_EPS = 1e-5
NEG_INF = -1e30
ALPHA = (2.0 * DEPTH) ** 0.25
Q_SCALE = HEAD_DIM ** -0.5

K_BLK, V_BLK = 0, 1
XR_BLK, GR_BLK, ZP_BLK = 1, 2, 3

LANES = 128
SUBLANES = 8
VMEM_LIMIT_BYTES = 56 * 1024 * 1024

TILE = 512
FFN_TM = 1024
FFN_FC = 1024
FFN_SUB = 256
X_TAIL = SUBLANES
Z_TAIL = 2 * SUBLANES
SAMPLE_BT = 16
CACHE_BT = 32

BF16 = jnp.bfloat16
F32 = jnp.float32

HEAD_ORDER = (0, 4, 1, 5, 2, 6, 3, 7)
HEAD_ORDER_INV = tuple(int(i) for i in np.argsort(HEAD_ORDER))

PROJ_COLS = ((0, ATTN_WIDTH), (ATTN_WIDTH, ATTN_WIDTH + 2 * KV_WIDTH),
             (ATTN_WIDTH + 2 * KV_WIDTH, ATTN_WIDTH + 2 * KV_WIDTH + 2 * LRU_WIDTH),
             (ATTN_WIDTH + 2 * KV_WIDTH + 2 * LRU_WIDTH, IN_WIDTH))


def _cparams(n_grid):
    return pltpu.CompilerParams(
        dimension_semantics=("arbitrary",) * n_grid,
        vmem_limit_bytes=VMEM_LIMIT_BYTES,
    )


def _whole_spec(shape):
    nd = len(shape)
    return pl.BlockSpec(shape, lambda *_: (0,) * nd, pipeline_mode=pl.Buffered(1))


def _layer_spec(shape, l):
    nd = len(shape) - 1
    return pl.BlockSpec((None,) + tuple(shape[1:]), lambda *_: (l,) + (0,) * nd, pipeline_mode=pl.Buffered(1))


def _layer_norm(x, g, b):
    mu = jnp.mean(x, axis=-1, keepdims=True)
    xc = x - mu
    var = jnp.mean(xc * xc, axis=-1, keepdims=True)
    return xc * lax.rsqrt(var + LN_EPS) * g + b


def _gelu_tanh(x):
    return 0.5 * x * (1.0 + jnp.tanh(np.sqrt(2.0 / np.pi) * (x + 0.044715 * (x * x * x))))


def _sigmoid(x):
    return 1.0 / (1.0 + jnp.exp(-x))


def _softplus(x):
    return jnp.maximum(x, 0.0) + jnp.log(1.0 + jnp.exp(-jnp.abs(x)))


def _interleave(*gens):
    active = list(gens)
    while active:
        for g in list(active):
            try:
                next(g)
            except StopIteration:
                active.remove(g)


def _attn_scores(q, k2, bias_t, lo):
    zero = jnp.zeros((), q.dtype)
    tiles = [q[:, c * LANES:(c + 1) * LANES] for c in range(GQA_GROUP)]
    qs = jnp.concatenate([jnp.where(lo, t, zero) for t in tiles]
                         + [jnp.where(lo, zero, t) for t in tiles], axis=0)
    return lax.dot_general(k2, qs, (((1,), (1,)), ((), ())), preferred_element_type=F32) + bias_t


def _attn_probs(s, sink):
    m = jnp.maximum(jnp.max(s, axis=0, keepdims=True), sink)
    p = jnp.exp(s - m)
    denom = jnp.sum(p, axis=0, keepdims=True) + jnp.exp(sink - m)
    return p.astype(BF16), 1.0 / denom


def _attn_values(p, v2):
    return lax.dot_general(v2, p, (((0,), (0,)), ((), ())), preferred_element_type=F32)


def _attn_output(o, inv_denom):
    o = o * inv_denom
    cols = []
    for c in range(ATTN_WIDTH // LANES):
        kv = (2 * c) // GQA_GROUP
        rows = slice(kv * HEAD_DIM, (kv + 1) * HEAD_DIM)
        blk = jnp.concatenate([o[rows, (2 * c) * WINDOW:(2 * c + 1) * WINDOW],
                               o[rows, (2 * c + 1) * WINDOW:(2 * c + 2) * WINDOW]], axis=0)
        cols.append(blk.T)
    return jnp.concatenate(cols, axis=1)


def _conv(l, xr, x_tail, cw_ref, cb_ref):
    xe = jnp.concatenate([x_tail, xr], axis=0)
    cw = cw_ref[...]
    xc = cb_ref[l:l + 1, :] + xr * cw[CONV_W - 1:CONV_W, :]
    for tap in range(CONV_W - 1):
        d = CONV_W - 1 - tap
        xc = xc + _shift_rows(xe, d)[X_TAIL:] * cw[tap:tap + 1, :]
    return xc


def _gate_logits(xc, wg_ref):
    return jnp.dot(xc.astype(BF16), wg_ref[...], preferred_element_type=F32)


def _gates(l, g, xc, ba_ref, bx_ref, lam_ref):
    r = _sigmoid(g[:, :LRU_WIDTH] + ba_ref[l:l + 1, :])
    i = _sigmoid(g[:, LRU_WIDTH:] + bx_ref[l:l + 1, :])
    log_a = (-LRU_C * r) * _softplus(-lam_ref[l:l + 1, :])
    a = jnp.exp(log_a)
    b = jnp.sqrt(1.0 - a * a) * (i * xc)
    return a, b


def _lru_scan(a, b, h_prev):
    t, w = a.shape
    groups = t // SUBLANES
    a3 = a.reshape(groups, SUBLANES, w)
    b3 = b.reshape(groups, SUBLANES, w)
    row = lax.broadcasted_iota(jnp.int32, (groups, SUBLANES, w), 1)
    s = 1
    while s < SUBLANES:
        keep = row >= s
        a_sh = jnp.where(keep, pltpu.roll(a3, s, 1), 1.0)
        b_sh = jnp.where(keep, pltpu.roll(b3, s, 1), 0.0)
        b3 = a3 * b_sh + b3
        a3 = a3 * a_sh
        s *= 2
    hs = []
    h = h_prev
    for g in range(groups):
        hg = a3[g] * h + b3[g]
        hs.append(hg)
        h = hg[SUBLANES - 1:SUBLANES, :]
    return jnp.concatenate(hs, axis=0), h


def _shift_rows(x, d):
    return pltpu.roll(x, d, 0)


def _pool_means(zp, z_tail, pos1):
    ze = jnp.concatenate([z_tail, zp], axis=0)
    t = zp.shape[0]
    lo = lax.broadcasted_iota(jnp.int32, (t, LANES), 1) < POOL_GROUP_W
    s2 = ze + _shift_rows(ze, 1)
    s4 = s2 + _shift_rows(s2, 2)
    hi4 = s4[:, LANES:]
    s8 = hi4 + _shift_rows(hi4, 4)
    s16 = s8 + _shift_rows(s8, 8)
    wins = (jnp.where(lo, s2[Z_TAIL:, :LANES], s4[Z_TAIL:, :LANES]),
            jnp.where(lo, s8[Z_TAIL:], s16[Z_TAIL:]))
    means = []
    for c, win in enumerate(wins):
        w_small, w_big = POOL_WINDOWS[2 * c], POOL_WINDOWS[2 * c + 1]
        count = jnp.minimum(pos1, jnp.where(lo, float(w_small), float(w_big)))
        means.append(win / count - zp[:, c * LANES:(c + 1) * LANES])
    return jnp.concatenate(means, axis=1)


def _prep_w_in(w_ref, wb_ref):
    lo = lax.broadcasted_iota(jnp.int32, (D_MODEL, LANES), 1) < HEAD_DIM
    src = [w_ref[:, s * LANES:(s + 1) * LANES] * Q_SCALE for s in range(ATTN_WIDTH // LANES)]
    swapped = [pltpu.roll(t, HEAD_DIM, 1) for t in src]
    for c in range(GQA_GROUP):
        s0, s1 = c // 2, GQA_GROUP // 2 + c // 2
        if c % 2 == 0:
            tile = jnp.where(lo, src[s0], swapped[s1])
        else:
            tile = jnp.where(lo, swapped[s0], src[s1])
        wb_ref[:, c * LANES:(c + 1) * LANES] = tile.astype(BF16)
    wb_ref[:, ATTN_WIDTH:] = w_ref[:, ATTN_WIDTH:].astype(BF16)


def _front_kernel(l, tiles_per_seq, xp_ref, xs_ref, w_ref, bias0_ref, bias_ref, sink_ref,
                  cw_ref, cb_ref, wg_ref, ba_ref, bx_ref, lam_ref, wp_ref, ps_ref,
                  wo_f_ref, w1_f_ref, w2_f_ref,
                  attn_ref, rp_ref, h_ref, kt_ref, vt_ref, xt_ref, zt_ref, qs_ref, rs_ref,
                  wo_b_ref, w1_b_ref, w2_b_ref,
                  wb_ref, q_scr, r_scr, kprev, vprev, xtail, ztail, hcar):
    wo_b_ref[...] = wo_f_ref[...].astype(BF16)
    w1_b_ref[...] = w1_f_ref[...].astype(BF16)
    w2_b_ref[...] = w2_f_ref[...].astype(BF16)

    s = pl.program_id(0)
    slot_p = lax.rem(s, 2)
    slot_m = 1 - slot_p
    j = lax.rem(s + tiles_per_seq - 1, tiles_per_seq)

    @pl.when(s == 0)
    def _():
        _prep_w_in(w_ref, wb_ref)
        us = jnp.dot(xs_ref[...].astype(BF16), wb_ref[...], preferred_element_type=F32)
        qs_ref[...] = us[:, :ATTN_WIDTH].astype(BF16)
        rs_ref[...] = us[:, ATTN_WIDTH:]
        q_scr[1] = jnp.zeros(q_scr.shape[1:], q_scr.dtype)
        r_scr[1] = jnp.zeros(r_scr.shape[1:], r_scr.dtype)
        kprev[...] = jnp.zeros_like(kprev)
        vprev[...] = jnp.zeros_like(vprev)

    @pl.when((j == 0) | (s == 0))
    def _():
        xtail[...] = jnp.zeros_like(xtail)
        ztail[...] = jnp.zeros_like(ztail)
        hcar[...] = jnp.zeros_like(hcar)

    carry = {}
    sub_blocks = [slice(c * WINDOW, (c + 1) * WINDOW) for c in range(TILE // WINDOW)]

    def attention():
        lo = lax.broadcasted_iota(jnp.int32, (WINDOW, LANES), 1) < HEAD_DIM
        sink = sink_ref[l:l + 1, :]
        xb16 = xp_ref[...].astype(BF16)
        k_prev, v_prev = kprev[...], vprev[...]
        k = v = None
        for c, rows in enumerate(sub_blocks):
            k = r_scr[slot_m, rows, K_BLK * KV_WIDTH:(K_BLK + 1) * KV_WIDTH]
            v = r_scr[slot_m, rows, V_BLK * KV_WIDTH:(V_BLK + 1) * KV_WIDTH]
            kb, vb = k.astype(BF16), v.astype(BF16)
            bias_t = bias0_ref[...] if c == 0 else bias_ref[...]
            sc = _attn_scores(q_scr[slot_m, rows, :], jnp.concatenate([k_prev, kb], axis=0), bias_t, lo)
            yield
            c0, c1 = PROJ_COLS[c]
            u = jnp.dot(xb16, wb_ref[:, c0:c1], preferred_element_type=F32)
            if c == 0:
                q_scr[slot_p] = u.astype(BF16)
            else:
                r_scr[slot_p, :, c0 - ATTN_WIDTH:c1 - ATTN_WIDTH] = u
            yield
            p, inv_denom = _attn_probs(sc, sink)
            o = _attn_values(p, jnp.concatenate([v_prev, vb], axis=0))
            yield
            attn_ref[rows, :] = _attn_output(o, inv_denom).astype(attn_ref.dtype)
            k_prev, v_prev = kb, vb
        carry.update(k_prev=k_prev, v_prev=v_prev, k=k, v=v)

    def mixers():
        x_tail, z_tail, h = xtail[...], ztail[...], hcar[...]
        for c, rows in enumerate(sub_blocks):
            xr = r_scr[slot_m, rows, XR_BLK * LRU_WIDTH:(XR_BLK + 1) * LRU_WIDTH]
            gr = r_scr[slot_m, rows, GR_BLK * LRU_WIDTH:(GR_BLK + 1) * LRU_WIDTH]
            zp = r_scr[slot_m, rows, ZP_BLK * LRU_WIDTH:(ZP_BLK + 1) * LRU_WIDTH]
            xc = _conv(l, xr, x_tail, cw_ref, cb_ref)
            g = _gate_logits(xc, wg_ref)
            yield
            a, b = _gates(l, g, xc, ba_ref, bx_ref, lam_ref)
            hs, h = _lru_scan(a, b, h)
            rec = hs * _gelu_tanh(gr)
            pos1 = (j * TILE + c * WINDOW + lax.broadcasted_iota(jnp.int32, (WINDOW, LANES), 0) + 1).astype(F32)
            diff = _pool_means(zp, z_tail, pos1).astype(BF16)
            pool = jnp.dot(diff, wp_ref[...], preferred_element_type=F32) * ps_ref[l:l + 1, :]
            yield
            rp_ref[rows, :] = jnp.concatenate([rec, pool], axis=1).astype(rp_ref.dtype)
            x_tail, z_tail = xr[WINDOW - X_TAIL:, :], zp[WINDOW - Z_TAIL:, :]
        carry.update(x_tail=x_tail, z_tail=z_tail, h=h)

    _interleave(attention(), mixers())

    kprev[...] = carry["k_prev"]
    vprev[...] = carry["v_prev"]
    xtail[...] = carry["x_tail"]
    ztail[...] = carry["z_tail"]
    hcar[...] = carry["h"]

    @pl.when(j == tiles_per_seq - 1)
    def _():
        kt_ref[...] = carry["k"].T
        vt_ref[...] = carry["v"].T
        h_ref[...] = carry["h"]
        xt_ref[...] = carry["x_tail"]
        zt_ref[...] = carry["z_tail"]


def front_prompt(l, x2d, seq, xs, w_in, bias2, sink_rows, pw, later_w):
    n = x2d.shape[0]
    n_s = xs.shape[0]
    slab = lambda s: jnp.minimum(s, n // TILE - 1)
    slab_rows = [w.shape[1] // (n // TILE) for w in later_w]
    slab_in = [pl.BlockSpec((None, r, w.shape[2]), lambda s: (l, slab(s), 0)) for r, w in zip(slab_rows, later_w)]
    slab_out = [pl.BlockSpec((r, w.shape[2]), lambda s: (slab(s), 0)) for r, w in zip(slab_rows, later_w)]
    b = n // seq
    n_tiles = n // TILE
    tiles_per_seq = seq // TILE
    done = lambda s: jnp.maximum(s - 1, 0)
    seq_of = lambda s: done(s) // tiles_per_seq
    bias_shape = (None, 2 * WINDOW, N_HEADS * WINDOW)
    per_seq = lambda rows, width: pl.BlockSpec((None, rows, width), lambda s: (seq_of(s), 0, 0))
    return pl.pallas_call(
        functools.partial(_front_kernel, l, tiles_per_seq),
        grid=(n_tiles + 1,),
        in_specs=[
            pl.BlockSpec((TILE, D_MODEL), lambda s: (jnp.minimum(s, n_tiles - 1), 0)),
            _whole_spec(xs.shape),
            _layer_spec(w_in.shape, l),
            pl.BlockSpec(bias_shape, lambda s: (jnp.minimum(done(s) % tiles_per_seq, 1), 0, 0)),
            pl.BlockSpec(bias_shape, lambda s: (1, 0, 0), pipeline_mode=pl.Buffered(1)),
            _whole_spec(sink_rows.shape),
            _layer_spec(pw["conv_w"].shape, l), _whole_spec(pw["conv_b"].shape),
            _layer_spec(pw["w_gate"].shape, l), _whole_spec(pw["gate_a_b"].shape),
            _whole_spec(pw["gate_x_b"].shape), _whole_spec(pw["lam"].shape),
            _layer_spec(pw["w_pool"].shape, l), _whole_spec(pw["pool_scale"].shape),
        ] + slab_in,
        out_specs=[
            pl.BlockSpec((TILE, ATTN_WIDTH), lambda s: (done(s), 0)),
            pl.BlockSpec((TILE, LRU_WIDTH + POOL_WIDTH), lambda s: (done(s), 0)),
            per_seq(1, LRU_WIDTH),
            per_seq(KV_WIDTH, WINDOW), per_seq(KV_WIDTH, WINDOW),
            per_seq(X_TAIL, LRU_WIDTH), per_seq(Z_TAIL, POOL_WIDTH),
            pl.BlockSpec((n_s, ATTN_WIDTH), lambda s: (0, 0)),
            pl.BlockSpec((n_s, REST_WIDTH), lambda s: (0, 0)),
        ] + slab_out,
        out_shape=[
            jax.ShapeDtypeStruct((n, ATTN_WIDTH), BF16),
            jax.ShapeDtypeStruct((n, LRU_WIDTH + POOL_WIDTH), BF16),
            jax.ShapeDtypeStruct((b, 1, LRU_WIDTH), F32),
            jax.ShapeDtypeStruct((b, KV_WIDTH, WINDOW), F32),
            jax.ShapeDtypeStruct((b, KV_WIDTH, WINDOW), F32),
            jax.ShapeDtypeStruct((b, X_TAIL, LRU_WIDTH), F32),
            jax.ShapeDtypeStruct((b, Z_TAIL, POOL_WIDTH), F32),
            jax.ShapeDtypeStruct((n_s, ATTN_WIDTH), BF16),
            jax.ShapeDtypeStruct((n_s, REST_WIDTH), F32),
        ] + [jax.ShapeDtypeStruct(w.shape[1:], BF16) for w in later_w],
        scratch_shapes=[
            pltpu.VMEM((D_MODEL, IN_WIDTH), BF16),
            pltpu.VMEM((2, TILE, ATTN_WIDTH), BF16),
            pltpu.VMEM((2, TILE, REST_WIDTH), F32),
            pltpu.VMEM((WINDOW, KV_WIDTH), BF16),
            pltpu.VMEM((WINDOW, KV_WIDTH), BF16),
            pltpu.VMEM((X_TAIL, LRU_WIDTH), F32),
            pltpu.VMEM((Z_TAIL, POOL_WIDTH), F32),
            pltpu.VMEM((1, LRU_WIDTH), F32),
        ],
        compiler_params=_cparams(1),
        name="front_prompt",
    )(x2d, xs, w_in, bias2, bias2, sink_rows, pw["conv_w"], pw["conv_b"], pw["w_gate"], pw["gate_a_b"],
      pw["gate_x_b"], pw["lam"], pw["w_pool"], pw["pool_scale"], *later_w)


def _mixer_sample_kernel(l, q_ref, kn_ref, vn_ref, xr_ref, gr_ref, zp_ref, ck_ref, cv_ref,
                         h0_ref, sc_ref, sp_ref, bias_ref, sink_ref,
                         cw_ref, cb_ref, wg_ref, ba_ref, bx_ref, lam_ref, wp_ref, ps_ref,
                         attn_ref, rp_ref, h_ref):
    bt = SAMPLE_BT
    lo3 = lax.broadcasted_iota(jnp.int32, (bt, GQA_GROUP, LANES), 2) < HEAD_DIM
    q4 = q_ref[...].astype(F32)
    qm = jnp.concatenate([jnp.where(lo3, q4, 0.0), jnp.where(lo3, 0.0, q4)], axis=1)
    s = jnp.einsum("bqc,bck->bqk", qm.astype(BF16), ck_ref[...].astype(BF16),
                   preferred_element_type=F32) + bias_ref[...]
    s_new = jnp.sum(qm * kn_ref[...][:, None, :], axis=2, keepdims=True)
    sink = sink_ref[:, l:l + 1]
    m = jnp.maximum(jnp.maximum(jnp.max(s, axis=2, keepdims=True), s_new), sink)
    p = jnp.exp(s - m)
    p_new = jnp.exp(s_new - m)
    denom = jnp.sum(p, axis=2, keepdims=True) + p_new + jnp.exp(sink - m)
    o = jnp.einsum("bqk,bck->bqc", p.astype(BF16), cv_ref[...].astype(BF16),
                   preferred_element_type=F32)
    o = (o + p_new * vn_ref[...][:, None, :]) / denom
    attn_ref[...] = jnp.where(lo3, o[:, :GQA_GROUP, :], o[:, GQA_GROUP:, :]).astype(attn_ref.dtype)

    xr = xr_ref[...]
    cw = cw_ref[...]
    xc = cb_ref[l:l + 1, :] + xr * cw[CONV_W - 1:CONV_W, :]
    for tap in range(CONV_W - 1):
        xc = xc + sc_ref[tap] * cw[tap:tap + 1, :]
    a, b = _gates(l, _gate_logits(xc, wg_ref), xc, ba_ref, bx_ref, lam_ref)
    h = a * h0_ref[...] + b
    h_ref[...] = h
    rec = h * _gelu_tanh(gr_ref[...])

    z = zp_ref[...]
    lo = lax.broadcasted_iota(jnp.int32, (bt, LANES), 1) < POOL_GROUP_W
    means = []
    for c in range(POOL_WIDTH // LANES):
        w_small, w_big = POOL_WINDOWS[2 * c], POOL_WINDOWS[2 * c + 1]
        cols = slice(c * LANES, (c + 1) * LANES)
        zc = z[:, cols]
        acc = zc
        small = None
        for d in range(1, w_big):
            acc = acc + sp_ref[POOL_CTX - d][:, cols]
            if d + 1 == w_small:
                small = acc
        win = jnp.where(lo, small, acc)
        count = jnp.where(lo, float(w_small), float(w_big))
        means.append(win / count - zc)
    diff = jnp.concatenate(means, axis=1).astype(BF16)
    pool = jnp.dot(diff, wp_ref[...], preferred_element_type=F32) * ps_ref[l:l + 1, :]
    rp_ref[...] = jnp.concatenate([rec, pool], axis=1).astype(rp_ref.dtype)


def mixer_sample(l, q4, rs, ck_t, cv_t, state_h, state_conv_t, state_pool_t, bias_s, sinks_t, pw):
    n = rs.shape[0]
    bt = SAMPLE_BT
    cache_spec = pl.BlockSpec((None, bt, KV_WIDTH, WINDOW), lambda i: (l, i, 0, 0))
    rcol = lambda width, c: pl.BlockSpec((bt, width), lambda i: (i, c))
    return pl.pallas_call(
        functools.partial(_mixer_sample_kernel, l),
        grid=(n // bt,),
        in_specs=[
            pl.BlockSpec((bt, GQA_GROUP, LANES), lambda i: (i, 0, 0)),
            rcol(KV_WIDTH, K_BLK), rcol(KV_WIDTH, V_BLK),
            rcol(LRU_WIDTH, XR_BLK), rcol(LRU_WIDTH, GR_BLK), rcol(POOL_WIDTH, ZP_BLK),
            cache_spec, cache_spec,
            pl.BlockSpec((None, bt, LRU_WIDTH), lambda i: (l, i, 0)),
            pl.BlockSpec((None, CONV_W - 1, bt, LRU_WIDTH), lambda i: (l, 0, i, 0)),
            pl.BlockSpec((None, POOL_CTX, bt, POOL_WIDTH), lambda i: (l, 0, i, 0)),
            _whole_spec(bias_s.shape),
            _whole_spec(sinks_t.shape),
            _layer_spec(pw["conv_w"].shape, l), _whole_spec(pw["conv_b"].shape),
            _layer_spec(pw["w_gate"].shape, l), _whole_spec(pw["gate_a_b"].shape),
            _whole_spec(pw["gate_x_b"].shape), _whole_spec(pw["lam"].shape),
            _layer_spec(pw["w_pool"].shape, l), _whole_spec(pw["pool_scale"].shape),
        ],
        out_specs=[
            pl.BlockSpec((bt, GQA_GROUP, LANES), lambda i: (i, 0, 0)),
            pl.BlockSpec((bt, LRU_WIDTH + POOL_WIDTH), lambda i: (i, 0)),
            pl.BlockSpec((bt, LRU_WIDTH), lambda i: (i, 0)),
        ],
        out_shape=[
            jax.ShapeDtypeStruct((n, GQA_GROUP, LANES), BF16),
            jax.ShapeDtypeStruct((n, LRU_WIDTH + POOL_WIDTH), BF16),
            jax.ShapeDtypeStruct((n, LRU_WIDTH), F32),
        ],
        compiler_params=_cparams(1),
        name="mixer_sample",
    )(q4, rs, rs, rs, rs, rs, ck_t, cv_t, state_h, state_conv_t, state_pool_t, bias_s, sinks_t,
      pw["conv_w"], pw["conv_b"], pw["w_gate"], pw["gate_a_b"], pw["gate_x_b"], pw["lam"],
      pw["w_pool"], pw["pool_scale"])


def _cache_update_kernel(ck_ref, cv_ref, kn_ref, vn_ref, ok_ref, ov_ref):
    bt = CACHE_BT
    last = lax.broadcasted_iota(jnp.int32, (KV_WIDTH, WINDOW), 1) == WINDOW - 1
    pad = jnp.zeros((LANES - bt, KV_WIDTH), F32)
    for src, new, dst in ((ck_ref, kn_ref, ok_ref), (cv_ref, vn_ref, ov_ref)):
        new_t = jnp.concatenate([new[...], pad], axis=0).T
        for s in range(bt):
            shifted = pltpu.roll(src[s], WINDOW - 1, 1)
            col = pltpu.roll(new_t, WINDOW - 1 - s, 1)
            dst[s] = jnp.where(last, col, shifted)


def cache_update(ck_t, cv_t, k_new, v_new):
    depth, n = k_new.shape[:2]
    bt = CACHE_BT
    cache_spec = pl.BlockSpec((None, bt, KV_WIDTH, WINDOW), lambda l, i: (l, i, 0, 0))
    new_spec = pl.BlockSpec((None, bt, KV_WIDTH), lambda l, i: (l, i, 0))
    shape = jax.ShapeDtypeStruct(ck_t.shape, F32)
    return pl.pallas_call(
        _cache_update_kernel,
        grid=(depth, n // bt),
        in_specs=[cache_spec, cache_spec, new_spec, new_spec],
        out_specs=[cache_spec, cache_spec],
        out_shape=[shape, shape],
        compiler_params=_cparams(2),
        name="cache_update",
    )(ck_t, cv_t, k_new, v_new)


def _out_ffn_kernel(l, x_ref, at_ref, rp_ref, xs_ref, ats_ref, rps_ref, wo_ref, g1_ref, b1_ref,
                    w1_ref, w2_ref, g2_ref, b2_ref, y_ref, ys_ref, acc_ref):
    @pl.when(pl.program_id(0) == pl.num_programs(0) - 1)
    def _():
        mix = jnp.dot(ats_ref[...], wo_ref[:ATTN_WIDTH, :], preferred_element_type=F32)
        mix = mix + jnp.dot(rps_ref[...], wo_ref[ATTN_WIDTH:, :], preferred_element_type=F32)
        x1 = _layer_norm(ALPHA * xs_ref[...] + mix, g1_ref[l:l + 1, :], b1_ref[l:l + 1, :])
        hid = jnp.dot(x1.astype(BF16), w1_ref[...], preferred_element_type=F32)
        hid = jnp.square(jnp.maximum(hid, 0.0)).astype(BF16)
        ffn = jnp.dot(hid, w2_ref[...], preferred_element_type=F32)
        ys_ref[...] = _layer_norm(ALPHA * x1 + ffn, g2_ref[l:l + 1, :], b2_ref[l:l + 1, :])

    tm = x_ref.shape[0]
    sub = FFN_SUB
    groups = [slice(g * sub, (g + 1) * sub) for g in range(tm // sub)]
    mixes = []
    for rows in groups:
        mix = jnp.dot(at_ref[rows, :], wo_ref[:ATTN_WIDTH, :], preferred_element_type=F32)
        mixes.append(mix + jnp.dot(rp_ref[rows, :], wo_ref[ATTN_WIDTH:, :], preferred_element_type=F32))
    x1s = [_layer_norm(ALPHA * x_ref[rows, :] + mix, g1_ref[l:l + 1, :], b1_ref[l:l + 1, :])
           for rows, mix in zip(groups, mixes)]
    x1bs = [x1.astype(BF16) for x1 in x1s]
    for c in range(D_FF // FFN_FC):
        cols = slice(c * FFN_FC, (c + 1) * FFN_FC)
        for rows, x1b in zip(groups, x1bs):
            hid = jnp.dot(x1b, w1_ref[:, cols], preferred_element_type=F32)
            hid = jnp.square(jnp.maximum(hid, 0.0)).astype(BF16)
            part = jnp.dot(hid, w2_ref[cols, :], preferred_element_type=F32)
            if c == 0:
                acc_ref[rows, :] = part
            else:
                acc_ref[rows, :] += part
    for rows, x1 in zip(groups, x1s):
        y_ref[rows, :] = _layer_norm(ALPHA * x1 + acc_ref[rows, :], g2_ref[l:l + 1, :], b2_ref[l:l + 1, :])


def out_ffn(l, x2d, attn2d, rp2d, xs, attn_s, rp_s, w_out_b, w_ff1_b, w_ff2_b, fw):
    n = x2d.shape[0]
    tm = FFN_TM
    row = lambda width: pl.BlockSpec((tm, width), lambda i: (i, 0))
    vec = _whole_spec((DEPTH, D_MODEL))
    return pl.pallas_call(
        functools.partial(_out_ffn_kernel, l),
        grid=(n // tm,),
        in_specs=[
            row(D_MODEL), row(ATTN_WIDTH), row(LRU_WIDTH + POOL_WIDTH),
            _whole_spec(xs.shape), _whole_spec(attn_s.shape), _whole_spec(rp_s.shape),
            _whole_spec(w_out_b.shape), vec, vec,
            _whole_spec(w_ff1_b.shape), _whole_spec(w_ff2_b.shape), vec, vec,
        ],
        out_specs=[row(D_MODEL), pl.BlockSpec(xs.shape, lambda i: (0, 0))],
        out_shape=[jax.ShapeDtypeStruct((n, D_MODEL), F32), jax.ShapeDtypeStruct(xs.shape, F32)],
        scratch_shapes=[pltpu.VMEM((tm, D_MODEL), F32)],
        compiler_params=_cparams(1),
        name="out_ffn",
    )(x2d, attn2d, rp2d, xs, attn_s, rp_s, w_out_b, fw["ln1_g"], fw["ln1_b"],
      w_ff1_b, w_ff2_b, fw["ln2_g"], fw["ln2_b"])


def _block_diag(w):
    depth, g, c, d = w.shape
    eye = jnp.eye(g, dtype=bool)[None, :, None, :, None]
    return jnp.where(eye, w[:, :, :, None, :], 0.0).reshape(depth, g * c, g * d)


def _alibi_slopes():
    return np.exp2(-8.0 * (np.arange(N_HEADS, dtype=np.float32) + 1.0) / N_HEADS).astype(np.float32)


def _prompt_bias_tables():
    slopes = _alibi_slopes()
    jk = np.arange(2 * WINDOW)[:, None]
    tq = np.arange(WINDOW)[None, :]
    delta = tq + WINDOW - jk
    visible = (delta >= 0) & (delta <= WINDOW)
    bias = -slopes[None, :, None] * delta.astype(np.float32)[:, None, :]
    full = np.where(visible[:, None, :], bias, np.float32(NEG_INF))
    first = np.where((visible & (jk >= WINDOW))[:, None, :], bias, np.float32(NEG_INF))
    return np.stack([first, full]).reshape(2, 2 * WINDOW, N_HEADS * WINDOW).astype(np.float32)


def kernel(x_prompt, x_sample, cache_k, cache_v, state_h, state_conv, state_pool, w_in, attn_sinks, conv_w, conv_b, gate_a_w, gate_a_b, gate_x_w, gate_x_b, lru_lambda, pool_w, pool_scale, w_out, ln1_g, ln1_b, w_ff1, w_ff2, ln2_g, ln2_b):
    batch, seq, _ = x_prompt.shape
    dec = x_sample.shape[0]
    bias_prompt = jnp.asarray(_prompt_bias_tables())
    bias_sample = jnp.asarray(-_alibi_slopes()[:, None] * (WINDOW - np.arange(WINDOW, dtype=np.float32))[None, :])
    sink_rows = jnp.repeat(attn_sinks, WINDOW, axis=1)
    sinks_t = attn_sinks.T

    pw = {
        "conv_w": conv_w, "conv_b": conv_b,
        "w_gate": jnp.concatenate([_block_diag(gate_a_w), _block_diag(gate_x_w)], axis=2).astype(BF16),
        "gate_a_b": gate_a_b, "gate_x_b": gate_x_b, "lam": lru_lambda,
        "w_pool": _block_diag(pool_w).astype(BF16), "pool_scale": pool_scale,
    }
    fw = {"ln1_g": ln1_g, "ln1_b": ln1_b, "ln2_g": ln2_g, "ln2_b": ln2_b}

    def kv_view(c):
        return jnp.transpose(c, (0, 1, 3, 4, 2)).reshape(c.shape[0], c.shape[1], KV_WIDTH, WINDOW)

    def kv_unview(c_t):
        c5 = c_t.reshape(c_t.shape[0], c_t.shape[1], N_KV_HEADS, HEAD_DIM, WINDOW)
        return jnp.transpose(c5, (0, 1, 4, 2, 3))

    ck_t, cv_t = kv_view(cache_k), kv_view(cache_v)
    state_conv_t = jnp.swapaxes(state_conv, 1, 2)
    state_pool_t = jnp.swapaxes(state_pool, 1, 2)

    yp = x_prompt.reshape(batch * seq, D_MODEL)
    ys = x_sample.reshape(dec, D_MODEL)
    xr_cols = slice(XR_BLK * LRU_WIDTH, (XR_BLK + 1) * LRU_WIDTH)
    zp_cols = slice(ZP_BLK * LRU_WIDTH, (ZP_BLK + 1) * LRU_WIDTH)
    outs = {k: [] for k in ("pk", "pv", "ph", "pc", "pp", "kn", "vn", "sh", "sc", "sp")}
    for l in range(DEPTH):
        attn, rp, h_last, k_last_t, v_last_t, x_tail, z_tail, qs, rs, w_out_b, w_ff1_b, w_ff2_b = front_prompt(
            l, yp, seq, ys, w_in, bias_prompt, sink_rows, pw, (w_out, w_ff1, w_ff2))
        attn_s, rp_s, h_s = mixer_sample(
            l, qs.reshape(dec, GQA_GROUP, LANES), rs, ck_t, cv_t, state_h, state_conv_t, state_pool_t,
            bias_sample, sinks_t, pw)
        attn_s = attn_s.reshape(dec, N_HEADS, HEAD_DIM)[:, HEAD_ORDER_INV, :].reshape(dec, ATTN_WIDTH)
        yp, ys = out_ffn(l, yp, attn, rp, ys, attn_s, rp_s, w_out_b, w_ff1_b, w_ff2_b, fw)
        outs["pk"].append(k_last_t)
        outs["pv"].append(v_last_t)
        outs["ph"].append(h_last.reshape(batch, LRU_WIDTH))
        outs["pc"].append(x_tail[:, X_TAIL - (CONV_W - 1):, :])
        outs["pp"].append(z_tail[:, Z_TAIL - POOL_CTX:, :])
        outs["kn"].append(rs[:, K_BLK * KV_WIDTH:(K_BLK + 1) * KV_WIDTH])
        outs["vn"].append(rs[:, V_BLK * KV_WIDTH:(V_BLK + 1) * KV_WIDTH])
        outs["sh"].append(h_s)
        outs["sc"].append(jnp.concatenate([state_conv[l][:, 1:], rs[:, None, xr_cols]], axis=1))
        outs["sp"].append(jnp.concatenate([state_pool[l][:, 1:], rs[:, None, zp_cols]], axis=1))

    st = {k: jnp.stack(v) for k, v in outs.items()}
    sk_t, sv_t = cache_update(ck_t, cv_t, st["kn"], st["vn"])
    return (yp.reshape(batch, seq, D_MODEL), ys.reshape(dec, 1, D_MODEL),
            kv_unview(st["pk"]), kv_unview(st["pv"]), st["ph"], st["pc"], st["pp"],
            kv_unview(sk_t), kv_unview(sv_t), st["sh"], st["sc"], st["sp"])
```

```python
import functools

import jax
import jax.numpy as jnp
import numpy as np
from jax import lax
from jax.experimental import pallas as pl
from jax.experimental.pallas import tpu as pltpu

D_MODEL = 1024
DEPTH = 2
HEAD_DIM = 64
ATTN_WIDTH = 512
N_HEADS = 8
N_KV_HEADS = 2
GQA_GROUP = 4
KV_WIDTH = 128
WINDOW = 128
LRU_WIDTH = 256
LRU_C = 8.0
CONV_W = 4
POOL_WINDOWS = (2, 4, 8, 16)
POOL_WIDTH = 256
POOL_GROUP_W = 64
POOL_CTX = 15
MIX_WIDTH = ATTN_WIDTH + LRU_WIDTH + POOL_WIDTH
IN_WIDTH = 1536
REST_WIDTH = IN_WIDTH - ATTN_WIDTH
D_FF = 4096
LN_EPS = 1e-5
NEG_INF = -1e30
ALPHA = (2.0 * DEPTH) ** 0.25
Q_SCALE = HEAD_DIM ** -0.5

K_BLK, V_BLK = 0, 1
XR_BLK, GR_BLK, ZP_BLK = 1, 2, 3

LANES = 128
SUBLANES = 8
VMEM_LIMIT_BYTES = 56 * 1024 * 1024

TILE = 512
FFN_TM = 1024
FFN_FC = 1024
FFN_SUB = 256
X_TAIL = SUBLANES
Z_TAIL = 2 * SUBLANES
SAMPLE_BT = 16

BF16 = jnp.bfloat16
F32 = jnp.float32

HEAD_ORDER = (0, 4, 1, 5, 2, 6, 3, 7)
HEAD_ORDER_INV = tuple(int(i) for i in np.argsort(HEAD_ORDER))

PROJ_COLS = ((0, ATTN_WIDTH), (ATTN_WIDTH, ATTN_WIDTH + 2 * KV_WIDTH),
             (ATTN_WIDTH + 2 * KV_WIDTH, ATTN_WIDTH + 2 * KV_WIDTH + 2 * LRU_WIDTH),
             (ATTN_WIDTH + 2 * KV_WIDTH + 2 * LRU_WIDTH, IN_WIDTH))


def _cparams(n_grid):
    return pltpu.CompilerParams(
        dimension_semantics=("arbitrary",) * n_grid,
        vmem_limit_bytes=VMEM_LIMIT_BYTES,
    )


def _whole_spec(shape):
    nd = len(shape)
    return pl.BlockSpec(shape, lambda *_: (0,) * nd, pipeline_mode=pl.Buffered(1))


def _layer_spec(shape, l):
    nd = len(shape) - 1
    return pl.BlockSpec((None,) + tuple(shape[1:]), lambda *_: (l,) + (0,) * nd, pipeline_mode=pl.Buffered(1))


def _layer_norm(x, g, b):
    mu = jnp.mean(x, axis=-1, keepdims=True)
    xc = x - mu
    var = jnp.mean(xc * xc, axis=-1, keepdims=True)
    return xc * lax.rsqrt(var + LN_EPS) * g + b


def _gelu_tanh(x):
    return 0.5 * x * (1.0 + jnp.tanh(np.sqrt(2.0 / np.pi) * (x + 0.044715 * (x * x * x))))


def _sigmoid(x):
    return 1.0 / (1.0 + jnp.exp(-x))


def _softplus(x):
    return jnp.maximum(x, 0.0) + jnp.log(1.0 + jnp.exp(-jnp.abs(x)))


def _interleave(*gens):
    active = list(gens)
    while active:
        for g in list(active):
            try:
                next(g)
            except StopIteration:
                active.remove(g)


def _attn_scores(q, k2, bias_t, lo):
    zero = jnp.zeros((), q.dtype)
    tiles = [q[:, c * LANES:(c + 1) * LANES] for c in range(GQA_GROUP)]
    qs = jnp.concatenate([jnp.where(lo, t, zero) for t in tiles]
                         + [jnp.where(lo, zero, t) for t in tiles], axis=0)
    return lax.dot_general(k2, qs, (((1,), (1,)), ((), ())), preferred_element_type=F32) + bias_t


def _attn_probs(s, sink):
    m = jnp.maximum(jnp.max(s, axis=0, keepdims=True), sink)
    p = jnp.exp(s - m)
    denom = jnp.sum(p, axis=0, keepdims=True) + jnp.exp(sink - m)
    return p.astype(BF16), 1.0 / denom


def _attn_values(p, v2):
    return lax.dot_general(v2, p, (((0,), (0,)), ((), ())), preferred_element_type=F32)


def _attn_output(o, inv_denom):
    o = o * inv_denom
    cols = []
    for c in range(ATTN_WIDTH // LANES):
        kv = (2 * c) // GQA_GROUP
        rows = slice(kv * HEAD_DIM, (kv + 1) * HEAD_DIM)
        blk = jnp.concatenate([o[rows, (2 * c) * WINDOW:(2 * c + 1) * WINDOW],
                               o[rows, (2 * c + 1) * WINDOW:(2 * c + 2) * WINDOW]], axis=0)
        cols.append(blk.T)
    return jnp.concatenate(cols, axis=1)


def _conv(l, xr, x_tail, cw_ref, cb_ref):
    xe = jnp.concatenate([x_tail, xr], axis=0)
    cw = cw_ref[...]
    xc = cb_ref[l:l + 1, :] + xr * cw[CONV_W - 1:CONV_W, :]
    for tap in range(CONV_W - 1):
        d = CONV_W - 1 - tap
        xc = xc + _shift_rows(xe, d)[X_TAIL:] * cw[tap:tap + 1, :]
    return xc


def _gate_logits(xc, wg_ref):
    return jnp.dot(xc.astype(BF16), wg_ref[...], preferred_element_type=F32)


def _gates(l, g, xc, ba_ref, bx_ref, lam_ref):
    r = _sigmoid(g[:, :LRU_WIDTH] + ba_ref[l:l + 1, :])
    i = _sigmoid(g[:, LRU_WIDTH:] + bx_ref[l:l + 1, :])
    log_a = (-LRU_C * r) * _softplus(-lam_ref[l:l + 1, :])
    a = jnp.exp(log_a)
    b = jnp.sqrt(1.0 - a * a) * (i * xc)
    return a, b


def _lru_scan(a, b, h_prev):
    t, w = a.shape
    groups = t // SUBLANES
    a3 = a.reshape(groups, SUBLANES, w)
    b3 = b.reshape(groups, SUBLANES, w)
    row = lax.broadcasted_iota(jnp.int32, (groups, SUBLANES, w), 1)
    s = 1
    while s < SUBLANES:
        keep = row >= s
        a_sh = jnp.where(keep, pltpu.roll(a3, s, 1), 1.0)
        b_sh = jnp.where(keep, pltpu.roll(b3, s, 1), 0.0)
        b3 = a3 * b_sh + b3
        a3 = a3 * a_sh
        s *= 2
    hs = []
    h = h_prev
    for g in range(groups):
        hg = a3[g] * h + b3[g]
        hs.append(hg)
        h = hg[SUBLANES - 1:SUBLANES, :]
    return jnp.concatenate(hs, axis=0), h


def _shift_rows(x, d):
    return pltpu.roll(x, d, 0)


def _pool_means(zp, z_tail, pos1):
    ze = jnp.concatenate([z_tail, zp], axis=0)
    t = zp.shape[0]
    lo = lax.broadcasted_iota(jnp.int32, (t, LANES), 1) < POOL_GROUP_W
    s2 = ze + _shift_rows(ze, 1)
    s4 = s2 + _shift_rows(s2, 2)
    hi4 = s4[:, LANES:]
    s8 = hi4 + _shift_rows(hi4, 4)
    s16 = s8 + _shift_rows(s8, 8)
    wins = (jnp.where(lo, s2[Z_TAIL:, :LANES], s4[Z_TAIL:, :LANES]),
            jnp.where(lo, s8[Z_TAIL:], s16[Z_TAIL:]))
    means = []
    for c, win in enumerate(wins):
        w_small, w_big = POOL_WINDOWS[2 * c], POOL_WINDOWS[2 * c + 1]
        count = jnp.minimum(pos1, jnp.where(lo, float(w_small), float(w_big)))
        means.append(win / count - zp[:, c * LANES:(c + 1) * LANES])
    return jnp.concatenate(means, axis=1)


def _prep_w_in(w_ref, wb_ref):
    lo = lax.broadcasted_iota(jnp.int32, (D_MODEL, LANES), 1) < HEAD_DIM
    src = [w_ref[:, s * LANES:(s + 1) * LANES] * Q_SCALE for s in range(ATTN_WIDTH // LANES)]
    swapped = [pltpu.roll(t, HEAD_DIM, 1) for t in src]
    for c in range(GQA_GROUP):
        s0, s1 = c // 2, GQA_GROUP // 2 + c // 2
        if c % 2 == 0:
            tile = jnp.where(lo, src[s0], swapped[s1])
        else:
            tile = jnp.where(lo, swapped[s0], src[s1])
        wb_ref[:, c * LANES:(c + 1) * LANES] = tile.astype(BF16)
    wb_ref[:, ATTN_WIDTH:] = w_ref[:, ATTN_WIDTH:].astype(BF16)


def _front_kernel(l, tiles_per_seq, xp_ref, xs_ref, w_ref, bias0_ref, bias_ref, sink_ref,
                  cw_ref, cb_ref, wg_ref, ba_ref, bx_ref, lam_ref, wp_ref, ps_ref,
                  wo_f_ref, w1_f_ref, w2_f_ref,
                  attn_ref, rp_ref, h_ref, kt_ref, vt_ref, xt_ref, zt_ref, qs_ref, rs_ref,
                  wo_b_ref, w1_b_ref, w2_b_ref,
                  wb_ref, q_scr, r_scr, kprev, vprev, xtail, ztail, hcar):
    wo_b_ref[...] = wo_f_ref[...].astype(BF16)
    w1_b_ref[...] = w1_f_ref[...].astype(BF16)
    w2_b_ref[...] = w2_f_ref[...].astype(BF16)

    s = pl.program_id(0)
    slot_p = lax.rem(s, 2)
    slot_m = 1 - slot_p
    j = lax.rem(s + tiles_per_seq - 1, tiles_per_seq)

    @pl.when(s == 0)
    def _():
        _prep_w_in(w_ref, wb_ref)
        us = jnp.dot(xs_ref[...].astype(BF16), wb_ref[...], preferred_element_type=F32)
        qs_ref[...] = us[:, :ATTN_WIDTH].astype(BF16)
        rs_ref[...] = us[:, ATTN_WIDTH:]
        q_scr[1] = jnp.zeros(q_scr.shape[1:], q_scr.dtype)
        r_scr[1] = jnp.zeros(r_scr.shape[1:], r_scr.dtype)
        kprev[...] = jnp.zeros_like(kprev)
        vprev[...] = jnp.zeros_like(vprev)

    @pl.when((j == 0) | (s == 0))
    def _():
        xtail[...] = jnp.zeros_like(xtail)
        ztail[...] = jnp.zeros_like(ztail)
        hcar[...] = jnp.zeros_like(hcar)

    carry = {}
    sub_blocks = [slice(c * WINDOW, (c + 1) * WINDOW) for c in range(TILE // WINDOW)]

    def attention():
        lo = lax.broadcasted_iota(jnp.int32, (WINDOW, LANES), 1) < HEAD_DIM
        sink = sink_ref[l:l + 1, :]
        xb16 = xp_ref[...].astype(BF16)
        k_prev, v_prev = kprev[...], vprev[...]
        k = v = None
        for c, rows in enumerate(sub_blocks):
            k = r_scr[slot_m, rows, K_BLK * KV_WIDTH:(K_BLK + 1) * KV_WIDTH]
            v = r_scr[slot_m, rows, V_BLK * KV_WIDTH:(V_BLK + 1) * KV_WIDTH]
            kb, vb = k.astype(BF16), v.astype(BF16)
            bias_t = bias0_ref[...] if c == 0 else bias_ref[...]
            sc = _attn_scores(q_scr[slot_m, rows, :], jnp.concatenate([k_prev, kb], axis=0), bias_t, lo)
            yield
            c0, c1 = PROJ_COLS[c]
            u = jnp.dot(xb16, wb_ref[:, c0:c1], preferred_element_type=F32)
            if c == 0:
                q_scr[slot_p] = u.astype(BF16)
            else:
                r_scr[slot_p, :, c0 - ATTN_WIDTH:c1 - ATTN_WIDTH] = u
            yield
            p, inv_denom = _attn_probs(sc, sink)
            o = _attn_values(p, jnp.concatenate([v_prev, vb], axis=0))
            yield
            attn_ref[rows, :] = _attn_output(o, inv_denom).astype(attn_ref.dtype)
            k_prev, v_prev = kb, vb
        carry.update(k_prev=k_prev, v_prev=v_prev, k=k, v=v)

    def mixers():
        x_tail, z_tail, h = xtail[...], ztail[...], hcar[...]
        for c, rows in enumerate(sub_blocks):
            xr = r_scr[slot_m, rows, XR_BLK * LRU_WIDTH:(XR_BLK + 1) * LRU_WIDTH]
            gr = r_scr[slot_m, rows, GR_BLK * LRU_WIDTH:(GR_BLK + 1) * LRU_WIDTH]
            zp = r_scr[slot_m, rows, ZP_BLK * LRU_WIDTH:(ZP_BLK + 1) * LRU_WIDTH]
            xc = _conv(l, xr, x_tail, cw_ref, cb_ref)
            g = _gate_logits(xc, wg_ref)
            yield
            a, b = _gates(l, g, xc, ba_ref, bx_ref, lam_ref)
            hs, h = _lru_scan(a, b, h)
            rec = hs * _gelu_tanh(gr)
            pos1 = (j * TILE + c * WINDOW + lax.broadcasted_iota(jnp.int32, (WINDOW, LANES), 0) + 1).astype(F32)
            diff = _pool_means(zp, z_tail, pos1).astype(BF16)
            pool = jnp.dot(diff, wp_ref[...], preferred_element_type=F32) * ps_ref[l:l + 1, :]
            yield
            rp_ref[rows, :] = jnp.concatenate([rec, pool], axis=1).astype(rp_ref.dtype)
            x_tail, z_tail = xr[WINDOW - X_TAIL:, :], zp[WINDOW - Z_TAIL:, :]
        carry.update(x_tail=x_tail, z_tail=z_tail, h=h)

    _interleave(attention(), mixers())

    kprev[...] = carry["k_prev"]
    vprev[...] = carry["v_prev"]
    xtail[...] = carry["x_tail"]
    ztail[...] = carry["z_tail"]
    hcar[...] = carry["h"]

    @pl.when(j == tiles_per_seq - 1)
    def _():
        kt_ref[...] = carry["k"].T
        vt_ref[...] = carry["v"].T
        h_ref[...] = carry["h"]
        xt_ref[...] = carry["x_tail"]
        zt_ref[...] = carry["z_tail"]


def front_prompt(l, x2d, seq, xs, w_in, bias2, sink_rows, pw, later_w):
    n = x2d.shape[0]
    n_s = xs.shape[0]
    slab = lambda s: jnp.minimum(s, n // TILE - 1)
    slab_rows = [w.shape[1] // (n // TILE) for w in later_w]
    slab_in = [pl.BlockSpec((None, r, w.shape[2]), lambda s: (l, slab(s), 0)) for r, w in zip(slab_rows, later_w)]
    slab_out = [pl.BlockSpec((r, w.shape[2]), lambda s: (slab(s), 0)) for r, w in zip(slab_rows, later_w)]
    b = n // seq
    n_tiles = n // TILE
    tiles_per_seq = seq // TILE
    done = lambda s: jnp.maximum(s - 1, 0)
    seq_of = lambda s: done(s) // tiles_per_seq
    bias_shape = (None, 2 * WINDOW, N_HEADS * WINDOW)
    per_seq = lambda rows, width: pl.BlockSpec((None, rows, width), lambda s: (seq_of(s), 0, 0))
    return pl.pallas_call(
        functools.partial(_front_kernel, l, tiles_per_seq),
        grid=(n_tiles + 1,),
        in_specs=[
            pl.BlockSpec((TILE, D_MODEL), lambda s: (jnp.minimum(s, n_tiles - 1), 0)),
            _whole_spec(xs.shape),
            _layer_spec(w_in.shape, l),
            pl.BlockSpec(bias_shape, lambda s: (jnp.minimum(done(s) % tiles_per_seq, 1), 0, 0)),
            pl.BlockSpec(bias_shape, lambda s: (1, 0, 0), pipeline_mode=pl.Buffered(1)),
            _whole_spec(sink_rows.shape),
            _layer_spec(pw["conv_w"].shape, l), _whole_spec(pw["conv_b"].shape),
            _layer_spec(pw["w_gate"].shape, l), _whole_spec(pw["gate_a_b"].shape),
            _whole_spec(pw["gate_x_b"].shape), _whole_spec(pw["lam"].shape),
            _layer_spec(pw["w_pool"].shape, l), _whole_spec(pw["pool_scale"].shape),
        ] + slab_in,
        out_specs=[
            pl.BlockSpec((TILE, ATTN_WIDTH), lambda s: (done(s), 0)),
            pl.BlockSpec((TILE, LRU_WIDTH + POOL_WIDTH), lambda s: (done(s), 0)),
            per_seq(1, LRU_WIDTH),
            per_seq(KV_WIDTH, WINDOW), per_seq(KV_WIDTH, WINDOW),
            per_seq(X_TAIL, LRU_WIDTH), per_seq(Z_TAIL, POOL_WIDTH),
            pl.BlockSpec((n_s, ATTN_WIDTH), lambda s: (0, 0)),
            pl.BlockSpec((n_s, REST_WIDTH), lambda s: (0, 0)),
        ] + slab_out,
        out_shape=[
            jax.ShapeDtypeStruct((n, ATTN_WIDTH), BF16),
            jax.ShapeDtypeStruct((n, LRU_WIDTH + POOL_WIDTH), BF16),
            jax.ShapeDtypeStruct((b, 1, LRU_WIDTH), F32),
            jax.ShapeDtypeStruct((b, KV_WIDTH, WINDOW), F32),
            jax.ShapeDtypeStruct((b, KV_WIDTH, WINDOW), F32),
            jax.ShapeDtypeStruct((b, X_TAIL, LRU_WIDTH), F32),
            jax.ShapeDtypeStruct((b, Z_TAIL, POOL_WIDTH), F32),
            jax.ShapeDtypeStruct((n_s, ATTN_WIDTH), BF16),
            jax.ShapeDtypeStruct((n_s, REST_WIDTH), F32),
        ] + [jax.ShapeDtypeStruct(w.shape[1:], BF16) for w in later_w],
        scratch_shapes=[
            pltpu.VMEM((D_MODEL, IN_WIDTH), BF16),
            pltpu.VMEM((2, TILE, ATTN_WIDTH), BF16),
            pltpu.VMEM((2, TILE, REST_WIDTH), F32),
            pltpu.VMEM((WINDOW, KV_WIDTH), BF16),
            pltpu.VMEM((WINDOW, KV_WIDTH), BF16),
            pltpu.VMEM((X_TAIL, LRU_WIDTH), F32),
            pltpu.VMEM((Z_TAIL, POOL_WIDTH), F32),
            pltpu.VMEM((1, LRU_WIDTH), F32),
        ],
        compiler_params=_cparams(1),
        name="front_prompt",
    )(x2d, xs, w_in, bias2, bias2, sink_rows, pw["conv_w"], pw["conv_b"], pw["w_gate"], pw["gate_a_b"],
      pw["gate_x_b"], pw["lam"], pw["w_pool"], pw["pool_scale"], *later_w)


def _mixer_sample_kernel(l, n_carried, q_ref, kn_ref, vn_ref, xr_ref, gr_ref, zp_ref, ck_ref, cv_ref,
                         h0_ref, sc_ref, sp_ref, bias_ref, sink_ref,
                         cw_ref, cb_ref, wg_ref, ba_ref, bx_ref, lam_ref, wp_ref, ps_ref, *rest):
    attn_ref, rp_ref, h_ref, ok_ref, ov_ref = rest[n_carried:]
    bt = SAMPLE_BT
    lo3 = lax.broadcasted_iota(jnp.int32, (bt, GQA_GROUP, LANES), 2) < HEAD_DIM
    q4 = q_ref[...].astype(F32)
    qm = jnp.concatenate([jnp.where(lo3, q4, 0.0), jnp.where(lo3, 0.0, q4)], axis=1)
    s = jnp.einsum("bqc,bck->bqk", qm.astype(BF16), ck_ref[...].astype(BF16),
                   preferred_element_type=F32) + bias_ref[...]
    s_new = jnp.sum(qm * kn_ref[...][:, None, :], axis=2, keepdims=True)
    sink = sink_ref[:, l:l + 1]
    m = jnp.maximum(jnp.maximum(jnp.max(s, axis=2, keepdims=True), s_new), sink)
    p = jnp.exp(s - m)
    p_new = jnp.exp(s_new - m)
    denom = jnp.sum(p, axis=2, keepdims=True) + p_new + jnp.exp(sink - m)
    o = jnp.einsum("bqk,bck->bqc", p.astype(BF16), cv_ref[...].astype(BF16),
                   preferred_element_type=F32)
    o = (o + p_new * vn_ref[...][:, None, :]) / denom
    attn_ref[...] = jnp.where(lo3, o[:, :GQA_GROUP, :], o[:, GQA_GROUP:, :]).astype(attn_ref.dtype)

    xr = xr_ref[...]
    cw = cw_ref[...]
    xc = cb_ref[l:l + 1, :] + xr * cw[CONV_W - 1:CONV_W, :]
    for tap in range(CONV_W - 1):
        xc = xc + sc_ref[tap] * cw[tap:tap + 1, :]
    a, b = _gates(l, _gate_logits(xc, wg_ref), xc, ba_ref, bx_ref, lam_ref)
    h = a * h0_ref[...] + b
    h_ref[...] = h
    rec = h * _gelu_tanh(gr_ref[...])

    z = zp_ref[...]
    lo = lax.broadcasted_iota(jnp.int32, (bt, LANES), 1) < POOL_GROUP_W
    means = []
    for c in range(POOL_WIDTH // LANES):
        w_small, w_big = POOL_WINDOWS[2 * c], POOL_WINDOWS[2 * c + 1]
        cols = slice(c * LANES, (c + 1) * LANES)
        zc = z[:, cols]
        acc = zc
        small = None
        for d in range(1, w_big):
            acc = acc + sp_ref[POOL_CTX - d][:, cols]
            if d + 1 == w_small:
                small = acc
        win = jnp.where(lo, small, acc)
        count = jnp.where(lo, float(w_small), float(w_big))
        means.append(win / count - zc)
    diff = jnp.concatenate(means, axis=1).astype(BF16)
    pool = jnp.dot(diff, wp_ref[...], preferred_element_type=F32) * ps_ref[l:l + 1, :]
    rp_ref[...] = jnp.concatenate([rec, pool], axis=1).astype(rp_ref.dtype)

    last = lax.broadcasted_iota(jnp.int32, (KV_WIDTH, WINDOW), 1) == WINDOW - 1
    pad = jnp.zeros((LANES - bt, KV_WIDTH), F32)
    for src, new, dst in ((ck_ref, kn_ref, ok_ref), (cv_ref, vn_ref, ov_ref)):
        new_t = jnp.concatenate([new[...], pad], axis=0).T
        for s in range(bt):
            shifted = pltpu.roll(src[s], WINDOW - 1, 1)
            col = pltpu.roll(new_t, WINDOW - 1 - s, 1)
            dst[s] = jnp.where(last, col, shifted)


def mixer_sample(l, q4, rs, ck_t, cv_t, state_h, state_conv_t, state_pool_t, bias_s, sinks_t, pw, carried):
    n = rs.shape[0]
    bt = SAMPLE_BT
    cache_spec = pl.BlockSpec((None, bt, KV_WIDTH, WINDOW), lambda i: (l, i, 0, 0))
    rcol = lambda width, c: pl.BlockSpec((bt, width), lambda i: (i, c))
    n_in = 21
    return pl.pallas_call(
        functools.partial(_mixer_sample_kernel, l, len(carried)),
        grid=(n // bt,),
        in_specs=[
            pl.BlockSpec((bt, GQA_GROUP, LANES), lambda i: (i, 0, 0)),
            rcol(KV_WIDTH, K_BLK), rcol(KV_WIDTH, V_BLK),
            rcol(LRU_WIDTH, XR_BLK), rcol(LRU_WIDTH, GR_BLK), rcol(POOL_WIDTH, ZP_BLK),
            cache_spec, cache_spec,
            pl.BlockSpec((None, bt, LRU_WIDTH), lambda i: (l, i, 0)),
            pl.BlockSpec((None, CONV_W - 1, bt, LRU_WIDTH), lambda i: (l, 0, i, 0)),
            pl.BlockSpec((None, POOL_CTX, bt, POOL_WIDTH), lambda i: (l, 0, i, 0)),
            _whole_spec(bias_s.shape),
            _whole_spec(sinks_t.shape),
            _layer_spec(pw["conv_w"].shape, l), _whole_spec(pw["conv_b"].shape),
            _layer_spec(pw["w_gate"].shape, l), _whole_spec(pw["gate_a_b"].shape),
            _whole_spec(pw["gate_x_b"].shape), _whole_spec(pw["lam"].shape),
            _layer_spec(pw["w_pool"].shape, l), _whole_spec(pw["pool_scale"].shape),
        ] + [pl.BlockSpec(memory_space=pl.ANY)] * len(carried),
        out_specs=[
            pl.BlockSpec((bt, GQA_GROUP, LANES), lambda i: (i, 0, 0)),
            pl.BlockSpec((bt, LRU_WIDTH + POOL_WIDTH), lambda i: (i, 0)),
            pl.BlockSpec((bt, LRU_WIDTH), lambda i: (i, 0)),
            cache_spec, cache_spec,
        ],
        out_shape=[
            jax.ShapeDtypeStruct((n, GQA_GROUP, LANES), BF16),
            jax.ShapeDtypeStruct((n, LRU_WIDTH + POOL_WIDTH), BF16),
            jax.ShapeDtypeStruct((n, LRU_WIDTH), F32),
            jax.ShapeDtypeStruct(ck_t.shape, F32),
            jax.ShapeDtypeStruct(cv_t.shape, F32),
        ],
        input_output_aliases={n_in + c: 3 + c for c in range(len(carried))},
        compiler_params=_cparams(1),
        name="mixer_sample",
    )(q4, rs, rs, rs, rs, rs, ck_t, cv_t, state_h, state_conv_t, state_pool_t, bias_s, sinks_t,
      pw["conv_w"], pw["conv_b"], pw["w_gate"], pw["gate_a_b"], pw["gate_x_b"], pw["lam"],
      pw["w_pool"], pw["pool_scale"], *carried)


def _out_ffn_kernel(l, x_ref, at_ref, rp_ref, xs_ref, ats_ref, rps_ref, wo_ref, g1_ref, b1_ref,
                    w1_ref, w2_ref, g2_ref, b2_ref, y_ref, ys_ref, acc_ref):
    @pl.when(pl.program_id(0) == pl.num_programs(0) - 1)
    def _():
        mix = jnp.dot(ats_ref[...], wo_ref[:ATTN_WIDTH, :], preferred_element_type=F32)
        mix = mix + jnp.dot(rps_ref[...], wo_ref[ATTN_WIDTH:, :], preferred_element_type=F32)
        x1 = _layer_norm(ALPHA * xs_ref[...] + mix, g1_ref[l:l + 1, :], b1_ref[l:l + 1, :])
        hid = jnp.dot(x1.astype(BF16), w1_ref[...], preferred_element_type=F32)
        hid = jnp.square(jnp.maximum(hid, 0.0)).astype(BF16)
        ffn = jnp.dot(hid, w2_ref[...], preferred_element_type=F32)
        ys_ref[...] = _layer_norm(ALPHA * x1 + ffn, g2_ref[l:l + 1, :], b2_ref[l:l + 1, :])

    tm = x_ref.shape[0]
    sub = FFN_SUB
    groups = [slice(g * sub, (g + 1) * sub) for g in range(tm // sub)]
    mixes = []
    for rows in groups:
        mix = jnp.dot(at_ref[rows, :], wo_ref[:ATTN_WIDTH, :], preferred_element_type=F32)
        mixes.append(mix + jnp.dot(rp_ref[rows, :], wo_ref[ATTN_WIDTH:, :], preferred_element_type=F32))
    x1s = [_layer_norm(ALPHA * x_ref[rows, :] + mix, g1_ref[l:l + 1, :], b1_ref[l:l + 1, :])
           for rows, mix in zip(groups, mixes)]
    x1bs = [x1.astype(BF16) for x1 in x1s]
    for c in range(D_FF // FFN_FC):
        cols = slice(c * FFN_FC, (c + 1) * FFN_FC)
        for rows, x1b in zip(groups, x1bs):
            hid = jnp.dot(x1b, w1_ref[:, cols], preferred_element_type=F32)
            hid = jnp.square(jnp.maximum(hid, 0.0)).astype(BF16)
            part = jnp.dot(hid, w2_ref[cols, :], preferred_element_type=F32)
            if c == 0:
                acc_ref[rows, :] = part
            else:
                acc_ref[rows, :] += part
    for rows, x1 in zip(groups, x1s):
        y_ref[rows, :] = _layer_norm(ALPHA * x1 + acc_ref[rows, :], g2_ref[l:l + 1, :], b2_ref[l:l + 1, :])


def out_ffn(l, x2d, attn2d, rp2d, xs, attn_s, rp_s, w_out_b, w_ff1_b, w_ff2_b, fw):
    n = x2d.shape[0]
    tm = FFN_TM
    row = lambda width: pl.BlockSpec((tm, width), lambda i: (i, 0))
    vec = _whole_spec((DEPTH, D_MODEL))
    return pl.pallas_call(
        functools.partial(_out_ffn_kernel, l),
        grid=(n // tm,),
        in_specs=[
            row(D_MODEL), row(ATTN_WIDTH), row(LRU_WIDTH + POOL_WIDTH),
            _whole_spec(xs.shape), _whole_spec(attn_s.shape), _whole_spec(rp_s.shape),
            _whole_spec(w_out_b.shape), vec, vec,
            _whole_spec(w_ff1_b.shape), _whole_spec(w_ff2_b.shape), vec, vec,
        ],
        out_specs=[row(D_MODEL), pl.BlockSpec(xs.shape, lambda i: (0, 0))],
        out_shape=[jax.ShapeDtypeStruct((n, D_MODEL), F32), jax.ShapeDtypeStruct(xs.shape, F32)],
        scratch_shapes=[pltpu.VMEM((tm, D_MODEL), F32)],
        compiler_params=_cparams(1),
        name="out_ffn",
    )(x2d, attn2d, rp2d, xs, attn_s, rp_s, w_out_b, fw["ln1_g"], fw["ln1_b"],
      w_ff1_b, w_ff2_b, fw["ln2_g"], fw["ln2_b"])


def _block_diag(w):
    depth, g, c, d = w.shape
    eye = jnp.eye(g, dtype=bool)[None, :, None, :, None]
    return jnp.where(eye, w[:, :, :, None, :], 0.0).reshape(depth, g * c, g * d)


def _alibi_slopes():
    return np.exp2(-8.0 * (np.arange(N_HEADS, dtype=np.float32) + 1.0) / N_HEADS).astype(np.float32)


def _prompt_bias_tables():
    slopes = _alibi_slopes()
    jk = np.arange(2 * WINDOW)[:, None]
    tq = np.arange(WINDOW)[None, :]
    delta = tq + WINDOW - jk
    visible = (delta >= 0) & (delta <= WINDOW)
    bias = -slopes[None, :, None] * delta.astype(np.float32)[:, None, :]
    full = np.where(visible[:, None, :], bias, np.float32(NEG_INF))
    first = np.where((visible & (jk >= WINDOW))[:, None, :], bias, np.float32(NEG_INF))
    return np.stack([first, full]).reshape(2, 2 * WINDOW, N_HEADS * WINDOW).astype(np.float32)


def kernel(x_prompt, x_sample, cache_k, cache_v, state_h, state_conv, state_pool, w_in, attn_sinks, conv_w, conv_b, gate_a_w, gate_a_b, gate_x_w, gate_x_b, lru_lambda, pool_w, pool_scale, w_out, ln1_g, ln1_b, w_ff1, w_ff2, ln2_g, ln2_b):
    batch, seq, _ = x_prompt.shape
    dec = x_sample.shape[0]
    bias_prompt = jnp.asarray(_prompt_bias_tables())
    bias_sample = jnp.asarray(-_alibi_slopes()[:, None] * (WINDOW - np.arange(WINDOW, dtype=np.float32))[None, :])
    sink_rows = jnp.repeat(attn_sinks, WINDOW, axis=1)
    sinks_t = attn_sinks.T

    pw = {
        "conv_w": conv_w, "conv_b": conv_b,
        "w_gate": jnp.concatenate([_block_diag(gate_a_w), _block_diag(gate_x_w)], axis=2).astype(BF16),
        "gate_a_b": gate_a_b, "gate_x_b": gate_x_b, "lam": lru_lambda,
        "w_pool": _block_diag(pool_w).astype(BF16), "pool_scale": pool_scale,
    }
    fw = {"ln1_g": ln1_g, "ln1_b": ln1_b, "ln2_g": ln2_g, "ln2_b": ln2_b}

    def kv_view(c):
        return jnp.transpose(c, (0, 1, 3, 4, 2)).reshape(c.shape[0], c.shape[1], KV_WIDTH, WINDOW)

    def kv_unview(c_t):
        c5 = c_t.reshape(c_t.shape[0], c_t.shape[1], N_KV_HEADS, HEAD_DIM, WINDOW)
        return jnp.transpose(c5, (0, 1, 4, 2, 3))

    ck_t, cv_t = kv_view(cache_k), kv_view(cache_v)
    state_conv_t = jnp.swapaxes(state_conv, 1, 2)
    state_pool_t = jnp.swapaxes(state_pool, 1, 2)

    yp = x_prompt.reshape(batch * seq, D_MODEL)
    ys = x_sample.reshape(dec, D_MODEL)
    xr_cols = slice(XR_BLK * LRU_WIDTH, (XR_BLK + 1) * LRU_WIDTH)
    zp_cols = slice(ZP_BLK * LRU_WIDTH, (ZP_BLK + 1) * LRU_WIDTH)
    outs = {k: [] for k in ("pk", "pv", "ph", "pc", "pp", "sh", "sc", "sp")}
    new_caches = ()
    for l in range(DEPTH):
        attn, rp, h_last, k_last_t, v_last_t, x_tail, z_tail, qs, rs, w_out_b, w_ff1_b, w_ff2_b = front_prompt(
            l, yp, seq, ys, w_in, bias_prompt, sink_rows, pw, (w_out, w_ff1, w_ff2))
        attn_s, rp_s, h_s, *new_caches = mixer_sample(
            l, qs.reshape(dec, GQA_GROUP, LANES), rs, ck_t, cv_t, state_h, state_conv_t, state_pool_t,
            bias_sample, sinks_t, pw, tuple(new_caches))
        attn_s = attn_s.reshape(dec, N_HEADS, HEAD_DIM)[:, HEAD_ORDER_INV, :].reshape(dec, ATTN_WIDTH)
        yp, ys = out_ffn(l, yp, attn, rp, ys, attn_s, rp_s, w_out_b, w_ff1_b, w_ff2_b, fw)
        outs["pk"].append(k_last_t)
        outs["pv"].append(v_last_t)
        outs["ph"].append(h_last.reshape(batch, LRU_WIDTH))
        outs["pc"].append(x_tail[:, X_TAIL - (CONV_W - 1):, :])
        outs["pp"].append(z_tail[:, Z_TAIL - POOL_CTX:, :])
        outs["sh"].append(h_s)
        outs["sc"].append(jnp.concatenate([state_conv[l][:, 1:], rs[:, None, xr_cols]], axis=1))
        outs["sp"].append(jnp.concatenate([state_pool[l][:, 1:], rs[:, None, zp_cols]], axis=1))

    st = {k: jnp.stack(v) for k, v in outs.items()}
    sk_t, sv_t = new_caches
    return (yp.reshape(batch, seq, D_MODEL), ys.reshape(dec, 1, D_MODEL),
            kv_unview(st["pk"]), kv_unview(st["pv"]), st["ph"], st["pc"], st["pp"],
            kv_unview(sk_t), kv_unview(sv_t), st["sh"], st["sc"], st["sp"])
```

```python
import functools

import jax
import jax.numpy as jnp
import numpy as np
from jax import lax
from jax.experimental import pallas as pl
from jax.experimental.pallas import tpu as pltpu

D_MODEL = 1024
DEPTH = 2
HEAD_DIM = 64
ATTN_WIDTH = 512
N_HEADS = 8
N_KV_HEADS = 2
GQA_GROUP = 4
KV_WIDTH = 128
WINDOW = 128
LRU_WIDTH = 256
LRU_C = 8.0
CONV_W = 4
POOL_WINDOWS = (2, 4, 8, 16)
POOL_WIDTH = 256
POOL_GROUP_W = 64
POOL_CTX = 15
MIX_WIDTH = ATTN_WIDTH + LRU_WIDTH + POOL_WIDTH
IN_WIDTH = 1536
REST_WIDTH = IN_WIDTH - ATTN_WIDTH
D_FF = 4096
LN_EPS = 1e-5
NEG_INF = -1e30
ALPHA = (2.0 * DEPTH) ** 0.25
Q_SCALE = HEAD_DIM ** -0.5

K_BLK, V_BLK = 0, 1
XR_BLK, GR_BLK, ZP_BLK = 1, 2, 3

LANES = 128
SUBLANES = 8
VMEM_LIMIT_BYTES = 56 * 1024 * 1024

TILE = 512
FFN_TM = 1024
FFN_FC = 1024
FFN_SUB = 256
X_TAIL = SUBLANES
Z_TAIL = 2 * SUBLANES
SAMPLE_BT = 16

BF16 = jnp.bfloat16
F32 = jnp.float32

HEAD_ORDER = (0, 4, 1, 5, 2, 6, 3, 7)
HEAD_ORDER_INV = tuple(int(i) for i in np.argsort(HEAD_ORDER))

PROJ_COLS = ((0, ATTN_WIDTH), (ATTN_WIDTH, ATTN_WIDTH + 2 * KV_WIDTH),
             (ATTN_WIDTH + 2 * KV_WIDTH, ATTN_WIDTH + 2 * KV_WIDTH + 2 * LRU_WIDTH),
             (ATTN_WIDTH + 2 * KV_WIDTH + 2 * LRU_WIDTH, IN_WIDTH))


def _cparams(n_grid):
    return pltpu.CompilerParams(
        dimension_semantics=("arbitrary",) * n_grid,
        vmem_limit_bytes=VMEM_LIMIT_BYTES,
    )


def _whole_spec(shape):
    nd = len(shape)
    return pl.BlockSpec(shape, lambda *_: (0,) * nd, pipeline_mode=pl.Buffered(1))


def _layer_spec(shape, l):
    nd = len(shape) - 1
    return pl.BlockSpec((None,) + tuple(shape[1:]), lambda *_: (l,) + (0,) * nd, pipeline_mode=pl.Buffered(1))


def _layer_norm(x, g, b):
    mu = jnp.mean(x, axis=-1, keepdims=True)
    xc = x - mu
    var = jnp.mean(xc * xc, axis=-1, keepdims=True)
    return xc * lax.rsqrt(var + LN_EPS) * g + b


def _gelu_tanh(x):
    c = np.sqrt(2.0 / np.pi)
    half = 0.5 * x
    return half + half * jnp.tanh(x * (c + (c * 0.044715) * (x * x)))


def _sigmoid_of_neg(z):
    return 1.0 / (1.0 + jnp.exp(z))


def _softplus(x):
    return jnp.maximum(x, 0.0) + jnp.log(1.0 + jnp.exp(-jnp.abs(x)))


def _interleave(*gens):
    active = list(gens)
    while active:
        for g in list(active):
            try:
                next(g)
            except StopIteration:
                active.remove(g)


def _attn_scores(q, k2, bias_t, lo):
    zero = jnp.zeros((), q.dtype)
    tiles = [q[:, c * LANES:(c + 1) * LANES] for c in range(GQA_GROUP)]
    qs = jnp.concatenate([jnp.where(lo, t, zero) for t in tiles]
                         + [jnp.where(lo, zero, t) for t in tiles], axis=0)
    return lax.dot_general(k2, qs, (((1,), (1,)), ((), ())), preferred_element_type=F32) + bias_t


def _attn_probs(s, sink):
    m = jnp.maximum(jnp.max(s, axis=0, keepdims=True), sink)
    p = jnp.exp(s - m)
    denom = jnp.sum(p, axis=0, keepdims=True) + jnp.exp(sink - m)
    return p.astype(BF16), 1.0 / denom


def _attn_values(p, v2):
    return lax.dot_general(v2, p, (((0,), (0,)), ((), ())), preferred_element_type=F32)


def _attn_output(o, inv_denom):
    cols = []
    for c in range(ATTN_WIDTH // LANES):
        kv = (2 * c) // GQA_GROUP
        rows = slice(kv * HEAD_DIM, (kv + 1) * HEAD_DIM)
        heads = [slice(n * WINDOW, (n + 1) * WINDOW) for n in (2 * c, 2 * c + 1)]
        blk = jnp.concatenate([o[rows, h] * inv_denom[:, h] for h in heads], axis=0)
        cols.append(blk.T)
    return jnp.concatenate(cols, axis=1)


def _conv(l, xr, x_tail, cw_ref, cb_ref):
    xe = jnp.concatenate([x_tail, xr], axis=0)
    cw = cw_ref[...]
    xc = cb_ref[l:l + 1, :] + xr * cw[CONV_W - 1:CONV_W, :]
    for tap in range(CONV_W - 1):
        d = CONV_W - 1 - tap
        xc = xc + _shift_rows(xe, d)[X_TAIL:] * cw[tap:tap + 1, :]
    return xc


def _gate_logits(xc, wg_ref):
    return jnp.dot(xc.astype(BF16), wg_ref[...], preferred_element_type=F32)


def _gates(l, g_neg, xc, ba_ref, bx_ref, lam_ref):
    r = _sigmoid_of_neg(g_neg[:, :LRU_WIDTH] - ba_ref[l:l + 1, :])
    i = _sigmoid_of_neg(g_neg[:, LRU_WIDTH:] - bx_ref[l:l + 1, :])
    log_a = r * (-LRU_C * _softplus(-lam_ref[l:l + 1, :]))
    a = jnp.exp(log_a)
    b = jnp.sqrt(1.0 - a * a) * (i * xc)
    return a, b


def _scan_in_groups(a, b):
    t, w = a.shape
    groups = t // SUBLANES
    a3 = a.reshape(groups, SUBLANES, w)
    b3 = b.reshape(groups, SUBLANES, w)
    row = lax.broadcasted_iota(jnp.int32, (groups, SUBLANES, w), 1)
    s = 1
    while s < SUBLANES:
        keep = row >= s
        a_sh = jnp.where(keep, pltpu.roll(a3, s, 1), 1.0)
        b_sh = jnp.where(keep, pltpu.roll(b3, s, 1), 0.0)
        b3 = a3 * b_sh + b3
        a3 = a3 * a_sh
        s *= 2
    return a3, b3


def _scan_chain(a3, b3, h_prev):
    hs = []
    h = h_prev
    for g in range(a3.shape[0]):
        hg = a3[g] * h + b3[g]
        hs.append(hg)
        h = hg[SUBLANES - 1:SUBLANES, :]
    return jnp.concatenate(hs, axis=0), h


def _lru_scan(a, b, h_prev):
    return _scan_chain(*_scan_in_groups(a, b), h_prev)


def _lru_scan_by_phase(a, b, h_prev, a_scr, b_scr, h_scr):
    t, w = a.shape
    groups = t // SUBLANES
    tiles = w // LANES
    for c in range(tiles):
        a_scr[c] = a[:, c * LANES:(c + 1) * LANES]
        b_scr[c] = b[:, c * LANES:(c + 1) * LANES]

    def phase(ref, r):
        return jnp.concatenate([ref[c, pl.ds(r, groups, stride=SUBLANES), :] for c in range(tiles)], axis=1)

    a_cum, b_cum = [phase(a_scr, 0)], [phase(b_scr, 0)]
    for r in range(1, SUBLANES):
        ar, br = phase(a_scr, r), phase(b_scr, r)
        b_cum.append(ar * b_cum[-1] + br)
        a_cum.append(ar * a_cum[-1])
    ga, gb = a_cum[-1], b_cum[-1]
    gidx = lax.broadcasted_iota(jnp.int32, (groups, w), 0)
    s = 1
    while s < groups:
        keep = gidx >= s
        a_sh = jnp.where(keep, pltpu.roll(ga, s, 0), 1.0)
        b_sh = jnp.where(keep, pltpu.roll(gb, s, 0), 0.0)
        gb = ga * b_sh + gb
        ga = ga * a_sh
        s *= 2
    h_end = ga * h_prev + gb
    h_in = jnp.where(gidx >= 1, pltpu.roll(h_end, 1, 0), h_prev)
    for r in range(SUBLANES):
        hr = a_cum[r] * h_in + b_cum[r]
        for c in range(tiles):
            h_scr[c, pl.ds(r, groups, stride=SUBLANES), :] = hr[:, c * LANES:(c + 1) * LANES]
    hs = jnp.concatenate([h_scr[c] for c in range(tiles)], axis=1)
    return hs, h_end[groups - 1:groups, :]


def _shift_rows(x, d):
    return pltpu.roll(x, d, 0)


def _pool_means(zp, z_tail, pos1):
    ze = jnp.concatenate([z_tail, zp], axis=0)
    t = zp.shape[0]
    lo = lax.broadcasted_iota(jnp.int32, (t, LANES), 1) < POOL_GROUP_W
    s2 = ze + _shift_rows(ze, 1)
    s4 = s2 + _shift_rows(s2, 2)
    hi4 = s4[:, LANES:]
    s8 = hi4 + _shift_rows(hi4, 4)
    s16 = s8 + _shift_rows(s8, 8)
    wins = (jnp.where(lo, s2[Z_TAIL:, :LANES], s4[Z_TAIL:, :LANES]),
            jnp.where(lo, s8[Z_TAIL:], s16[Z_TAIL:]))
    means = []
    for c, win in enumerate(wins):
        w_small, w_big = POOL_WINDOWS[2 * c], POOL_WINDOWS[2 * c + 1]
        if pos1 is None:
            mean = win * jnp.where(lo, 1.0 / w_small, 1.0 / w_big)
        else:
            mean = win / jnp.minimum(pos1, jnp.where(lo, float(w_small), float(w_big)))
        means.append(mean - zp[:, c * LANES:(c + 1) * LANES])
    return jnp.concatenate(means, axis=1)


def _prep_w_in(w_ref, wb_ref):
    lo = lax.broadcasted_iota(jnp.int32, (D_MODEL, LANES), 1) < HEAD_DIM
    src = [w_ref[:, s * LANES:(s + 1) * LANES] * Q_SCALE for s in range(ATTN_WIDTH // LANES)]
    swapped = [pltpu.roll(t, HEAD_DIM, 1) for t in src]
    for c in range(GQA_GROUP):
        s0, s1 = c // 2, GQA_GROUP // 2 + c // 2
        if c % 2 == 0:
            tile = jnp.where(lo, src[s0], swapped[s1])
        else:
            tile = jnp.where(lo, swapped[s0], src[s1])
        wb_ref[:, c * LANES:(c + 1) * LANES] = tile.astype(BF16)
    wb_ref[:, ATTN_WIDTH:] = w_ref[:, ATTN_WIDTH:].astype(BF16)


def _front_kernel(l, tiles_per_seq, xp_ref, xs_ref, w_ref, bias0_ref, bias_ref, sink_ref,
                  cw_ref, cb_ref, wg_ref, ba_ref, bx_ref, lam_ref, wp_ref, ps_ref,
                  wo_f_ref, w1_f_ref, w2_f_ref,
                  attn_ref, rp_ref, h_ref, kt_ref, vt_ref, xt_ref, zt_ref, qs_ref, rs_ref,
                  wo_b_ref, w1_b_ref, w2_b_ref,
                  wb_ref, q_scr, r_scr, kprev, vprev, xtail, ztail, hcar, a_scr, b_scr, h_scr):
    wo_b_ref[...] = wo_f_ref[...].astype(BF16)
    w1_b_ref[...] = w1_f_ref[...].astype(BF16)
    w2_b_ref[...] = w2_f_ref[...].astype(BF16)

    s = pl.program_id(0)
    slot_p = lax.rem(s, 2)
    slot_m = 1 - slot_p
    j = lax.rem(s + tiles_per_seq - 1, tiles_per_seq)

    @pl.when(s == 0)
    def _():
        _prep_w_in(w_ref, wb_ref)
        us = jnp.dot(xs_ref[...].astype(BF16), wb_ref[...], preferred_element_type=F32)
        qs_ref[...] = us[:, :ATTN_WIDTH].astype(BF16)
        rs_ref[...] = us[:, ATTN_WIDTH:]
        q_scr[1] = jnp.zeros(q_scr.shape[1:], q_scr.dtype)
        r_scr[1] = jnp.zeros(r_scr.shape[1:], r_scr.dtype)
        kprev[...] = jnp.zeros_like(kprev)
        vprev[...] = jnp.zeros_like(vprev)

    @pl.when((j == 0) | (s == 0))
    def _():
        xtail[...] = jnp.zeros_like(xtail)
        ztail[...] = jnp.zeros_like(ztail)
        hcar[...] = jnp.zeros_like(hcar)

    carry = {}
    sub_blocks = [slice(c * WINDOW, (c + 1) * WINDOW) for c in range(TILE // WINDOW)]

    def attention():
        lo = lax.broadcasted_iota(jnp.int32, (WINDOW, LANES), 1) < HEAD_DIM
        sink = sink_ref[l:l + 1, :]
        xb16 = xp_ref[...].astype(BF16)
        k_prev, v_prev = kprev[...], vprev[...]
        k = v = None
        for c, rows in enumerate(sub_blocks):
            k = r_scr[slot_m, rows, K_BLK * KV_WIDTH:(K_BLK + 1) * KV_WIDTH]
            v = r_scr[slot_m, rows, V_BLK * KV_WIDTH:(V_BLK + 1) * KV_WIDTH]
            kb, vb = k.astype(BF16), v.astype(BF16)
            bias_t = bias0_ref[...] if c == 0 else bias_ref[...]
            sc = _attn_scores(q_scr[slot_m, rows, :], jnp.concatenate([k_prev, kb], axis=0), bias_t, lo)
            yield
            c0, c1 = PROJ_COLS[c]
            u = jnp.dot(xb16, wb_ref[:, c0:c1], preferred_element_type=F32)
            if c == 0:
                q_scr[slot_p] = u.astype(BF16)
            else:
                r_scr[slot_p, :, c0 - ATTN_WIDTH:c1 - ATTN_WIDTH] = u
            yield
            p, inv_denom = _attn_probs(sc, sink)
            o = _attn_values(p, jnp.concatenate([v_prev, vb], axis=0))
            yield
            attn_ref[rows, :] = _attn_output(o, inv_denom).astype(attn_ref.dtype)
            k_prev, v_prev = kb, vb
        carry.update(k_prev=k_prev, v_prev=v_prev, k=k, v=v)

    def mixers():
        x_tail, z_tail, h = xtail[...], ztail[...], hcar[...]
        for c, rows in enumerate(sub_blocks):
            xr = r_scr[slot_m, rows, XR_BLK * LRU_WIDTH:(XR_BLK + 1) * LRU_WIDTH]
            gr = r_scr[slot_m, rows, GR_BLK * LRU_WIDTH:(GR_BLK + 1) * LRU_WIDTH]
            zp = r_scr[slot_m, rows, ZP_BLK * LRU_WIDTH:(ZP_BLK + 1) * LRU_WIDTH]
            xc = _conv(l, xr, x_tail, cw_ref, cb_ref)
            g = _gate_logits(xc, wg_ref)
            yield
            a, b = _gates(l, g, xc, ba_ref, bx_ref, lam_ref)
            hs, h = _lru_scan_by_phase(a, b, h, a_scr, b_scr, h_scr)
            rec = hs * _gelu_tanh(gr)
            pos1 = None
            if c == 0:
                pos1 = (j * TILE + lax.broadcasted_iota(jnp.int32, (WINDOW, LANES), 0) + 1).astype(F32)
            diff = _pool_means(zp, z_tail, pos1).astype(BF16)
            pool = jnp.dot(diff, wp_ref[...], preferred_element_type=F32) * ps_ref[l:l + 1, :]
            yield
            rp_ref[rows, :] = jnp.concatenate([rec, pool], axis=1).astype(rp_ref.dtype)
            x_tail, z_tail = xr[WINDOW - X_TAIL:, :], zp[WINDOW - Z_TAIL:, :]
        carry.update(x_tail=x_tail, z_tail=z_tail, h=h)

    _interleave(attention(), mixers())

    kprev[...] = carry["k_prev"]
    vprev[...] = carry["v_prev"]
    xtail[...] = carry["x_tail"]
    ztail[...] = carry["z_tail"]
    hcar[...] = carry["h"]

    @pl.when(j == tiles_per_seq - 1)
    def _():
        kt_ref[...] = carry["k"].T
        vt_ref[...] = carry["v"].T
        h_ref[...] = carry["h"]
        xt_ref[...] = carry["x_tail"]
        zt_ref[...] = carry["z_tail"]


def front_prompt(l, x2d, seq, xs, w_in, bias2, sink_rows, pw, later_w):
    n = x2d.shape[0]
    n_s = xs.shape[0]
    slab = lambda s: jnp.minimum(s, n // TILE - 1)
    slab_rows = [w.shape[1] // (n // TILE) for w in later_w]
    slab_in = [pl.BlockSpec((None, r, w.shape[2]), lambda s: (l, slab(s), 0)) for r, w in zip(slab_rows, later_w)]
    slab_out = [pl.BlockSpec((r, w.shape[2]), lambda s: (slab(s), 0)) for r, w in zip(slab_rows, later_w)]
    b = n // seq
    n_tiles = n // TILE
    tiles_per_seq = seq // TILE
    done = lambda s: jnp.maximum(s - 1, 0)
    seq_of = lambda s: done(s) // tiles_per_seq
    bias_shape = (None, 2 * WINDOW, N_HEADS * WINDOW)
    per_seq = lambda rows, width: pl.BlockSpec((None, rows, width), lambda s: (seq_of(s), 0, 0))
    return pl.pallas_call(
        functools.partial(_front_kernel, l, tiles_per_seq),
        grid=(n_tiles + 1,),
        in_specs=[
            pl.BlockSpec((TILE, D_MODEL), lambda s: (jnp.minimum(s, n_tiles - 1), 0)),
            _whole_spec(xs.shape),
            _layer_spec(w_in.shape, l),
            pl.BlockSpec(bias_shape, lambda s: (jnp.minimum(done(s) % tiles_per_seq, 1), 0, 0)),
            pl.BlockSpec(bias_shape, lambda s: (1, 0, 0), pipeline_mode=pl.Buffered(1)),
            _whole_spec(sink_rows.shape),
            _layer_spec(pw["conv_w"].shape, l), _whole_spec(pw["conv_b"].shape),
            _layer_spec(pw["w_gate"].shape, l), _whole_spec(pw["gate_a_b"].shape),
            _whole_spec(pw["gate_x_b"].shape), _whole_spec(pw["lam"].shape),
            _layer_spec(pw["w_pool"].shape, l), _whole_spec(pw["pool_scale"].shape),
        ] + slab_in,
        out_specs=[
            pl.BlockSpec((TILE, ATTN_WIDTH), lambda s: (done(s), 0)),
            pl.BlockSpec((TILE, LRU_WIDTH + POOL_WIDTH), lambda s: (done(s), 0)),
            per_seq(1, LRU_WIDTH),
            per_seq(KV_WIDTH, WINDOW), per_seq(KV_WIDTH, WINDOW),
            per_seq(X_TAIL, LRU_WIDTH), per_seq(Z_TAIL, POOL_WIDTH),
            pl.BlockSpec((n_s, ATTN_WIDTH), lambda s: (0, 0)),
            pl.BlockSpec((n_s, REST_WIDTH), lambda s: (0, 0)),
        ] + slab_out,
        out_shape=[
            jax.ShapeDtypeStruct((n, ATTN_WIDTH), BF16),
            jax.ShapeDtypeStruct((n, LRU_WIDTH + POOL_WIDTH), BF16),
            jax.ShapeDtypeStruct((b, 1, LRU_WIDTH), F32),
            jax.ShapeDtypeStruct((b, KV_WIDTH, WINDOW), F32),
            jax.ShapeDtypeStruct((b, KV_WIDTH, WINDOW), F32),
            jax.ShapeDtypeStruct((b, X_TAIL, LRU_WIDTH), F32),
            jax.ShapeDtypeStruct((b, Z_TAIL, POOL_WIDTH), F32),
            jax.ShapeDtypeStruct((n_s, ATTN_WIDTH), BF16),
            jax.ShapeDtypeStruct((n_s, REST_WIDTH), F32),
        ] + [jax.ShapeDtypeStruct(w.shape[1:], BF16) for w in later_w],
        scratch_shapes=[
            pltpu.VMEM((D_MODEL, IN_WIDTH), BF16),
            pltpu.VMEM((2, TILE, ATTN_WIDTH), BF16),
            pltpu.VMEM((2, TILE, REST_WIDTH), F32),
            pltpu.VMEM((WINDOW, KV_WIDTH), BF16),
            pltpu.VMEM((WINDOW, KV_WIDTH), BF16),
            pltpu.VMEM((X_TAIL, LRU_WIDTH), F32),
            pltpu.VMEM((Z_TAIL, POOL_WIDTH), F32),
            pltpu.VMEM((1, LRU_WIDTH), F32),
            pltpu.VMEM((LRU_WIDTH // LANES, WINDOW, LANES), F32),
            pltpu.VMEM((LRU_WIDTH // LANES, WINDOW, LANES), F32),
            pltpu.VMEM((LRU_WIDTH // LANES, WINDOW, LANES), F32),
        ],
        compiler_params=_cparams(1),
        name="front_prompt",
    )(x2d, xs, w_in, bias2, bias2, sink_rows, pw["conv_w"], pw["conv_b"], pw["w_gate"], pw["gate_a_b"],
      pw["gate_x_b"], pw["lam"], pw["w_pool"], pw["pool_scale"], *later_w)


def _mixer_sample_kernel(l, n_carried, q_ref, kn_ref, vn_ref, xr_ref, gr_ref, zp_ref, ck_ref, cv_ref,
                         h0_ref, sc_ref, sp_ref, bias_ref, sink_ref,
                         cw_ref, cb_ref, wg_ref, ba_ref, bx_ref, lam_ref, wp_ref, ps_ref, *rest):
    attn_ref, rp_ref, h_ref, ok_ref, ov_ref = rest[n_carried:]
    bt = SAMPLE_BT
    lo3 = lax.broadcasted_iota(jnp.int32, (bt, GQA_GROUP, LANES), 2) < HEAD_DIM
    q4 = q_ref[...].astype(F32)
    qm = jnp.concatenate([jnp.where(lo3, q4, 0.0), jnp.where(lo3, 0.0, q4)], axis=1)
    s = jnp.einsum("bqc,bck->bqk", qm.astype(BF16), ck_ref[...].astype(BF16),
                   preferred_element_type=F32) + bias_ref[...]
    s_new = jnp.sum(qm * kn_ref[...][:, None, :], axis=2, keepdims=True)
    sink = sink_ref[:, l:l + 1]
    m = jnp.maximum(jnp.maximum(jnp.max(s, axis=2, keepdims=True), s_new), sink)
    p = jnp.exp(s - m)
    p_new = jnp.exp(s_new - m)
    denom = jnp.sum(p, axis=2, keepdims=True) + p_new + jnp.exp(sink - m)
    o = jnp.einsum("bqk,bck->bqc", p.astype(BF16), cv_ref[...].astype(BF16),
                   preferred_element_type=F32)
    o = (o + p_new * vn_ref[...][:, None, :]) / denom
    attn_ref[...] = jnp.where(lo3, o[:, :GQA_GROUP, :], o[:, GQA_GROUP:, :]).astype(attn_ref.dtype)

    xr = xr_ref[...]
    cw = cw_ref[...]
    xc = cb_ref[l:l + 1, :] + xr * cw[CONV_W - 1:CONV_W, :]
    for tap in range(CONV_W - 1):
        xc = xc + sc_ref[tap] * cw[tap:tap + 1, :]
    a, b = _gates(l, _gate_logits(xc, wg_ref), xc, ba_ref, bx_ref, lam_ref)
    h = a * h0_ref[...] + b
    h_ref[...] = h
    rec = h * _gelu_tanh(gr_ref[...])

    z = zp_ref[...]
    lo = lax.broadcasted_iota(jnp.int32, (bt, LANES), 1) < POOL_GROUP_W
    means = []
    for c in range(POOL_WIDTH // LANES):
        w_small, w_big = POOL_WINDOWS[2 * c], POOL_WINDOWS[2 * c + 1]
        cols = slice(c * LANES, (c + 1) * LANES)
        zc = z[:, cols]
        acc = zc
        small = None
        for d in range(1, w_big):
            acc = acc + sp_ref[POOL_CTX - d][:, cols]
            if d + 1 == w_small:
                small = acc
        win = jnp.where(lo, small, acc)
        count = jnp.where(lo, float(w_small), float(w_big))
        means.append(win / count - zc)
    diff = jnp.concatenate(means, axis=1).astype(BF16)
    pool = jnp.dot(diff, wp_ref[...], preferred_element_type=F32) * ps_ref[l:l + 1, :]
    rp_ref[...] = jnp.concatenate([rec, pool], axis=1).astype(rp_ref.dtype)

    last = lax.broadcasted_iota(jnp.int32, (KV_WIDTH, WINDOW), 1) == WINDOW - 1
    pad = jnp.zeros((LANES - bt, KV_WIDTH), F32)
    for src, new, dst in ((ck_ref, kn_ref, ok_ref), (cv_ref, vn_ref, ov_ref)):
        new_t = jnp.concatenate([new[...], pad], axis=0).T
        for s in range(bt):
            shifted = pltpu.roll(src[s], WINDOW - 1, 1)
            col = pltpu.roll(new_t, WINDOW - 1 - s, 1)
            dst[s] = jnp.where(last, col, shifted)


def mixer_sample(l, q4, rs, ck_t, cv_t, state_h, state_conv_t, state_pool_t, bias_s, sinks_t, pw, carried):
    n = rs.shape[0]
    bt = SAMPLE_BT
    cache_spec = pl.BlockSpec((None, bt, KV_WIDTH, WINDOW), lambda i: (l, i, 0, 0))
    rcol = lambda width, c: pl.BlockSpec((bt, width), lambda i: (i, c))
    n_in = 21
    return pl.pallas_call(
        functools.partial(_mixer_sample_kernel, l, len(carried)),
        grid=(n // bt,),
        in_specs=[
            pl.BlockSpec((bt, GQA_GROUP, LANES), lambda i: (i, 0, 0)),
            rcol(KV_WIDTH, K_BLK), rcol(KV_WIDTH, V_BLK),
            rcol(LRU_WIDTH, XR_BLK), rcol(LRU_WIDTH, GR_BLK), rcol(POOL_WIDTH, ZP_BLK),
            cache_spec, cache_spec,
            pl.BlockSpec((None, bt, LRU_WIDTH), lambda i: (l, i, 0)),
            pl.BlockSpec((None, CONV_W - 1, bt, LRU_WIDTH), lambda i: (l, 0, i, 0)),
            pl.BlockSpec((None, POOL_CTX, bt, POOL_WIDTH), lambda i: (l, 0, i, 0)),
            _whole_spec(bias_s.shape),
            _whole_spec(sinks_t.shape),
            _layer_spec(pw["conv_w"].shape, l), _whole_spec(pw["conv_b"].shape),
            _layer_spec(pw["w_gate"].shape, l), _whole_spec(pw["gate_a_b"].shape),
            _whole_spec(pw["gate_x_b"].shape), _whole_spec(pw["lam"].shape),
            _layer_spec(pw["w_pool"].shape, l), _whole_spec(pw["pool_scale"].shape),
        ] + [pl.BlockSpec(memory_space=pl.ANY)] * len(carried),
        out_specs=[
            pl.BlockSpec((bt, GQA_GROUP, LANES), lambda i: (i, 0, 0)),
            pl.BlockSpec((bt, LRU_WIDTH + POOL_WIDTH), lambda i: (i, 0)),
            pl.BlockSpec((bt, LRU_WIDTH), lambda i: (i, 0)),
            cache_spec, cache_spec,
        ],
        out_shape=[
            jax.ShapeDtypeStruct((n, GQA_GROUP, LANES), BF16),
            jax.ShapeDtypeStruct((n, LRU_WIDTH + POOL_WIDTH), BF16),
            jax.ShapeDtypeStruct((n, LRU_WIDTH), F32),
            jax.ShapeDtypeStruct(ck_t.shape, F32),
            jax.ShapeDtypeStruct(cv_t.shape, F32),
        ],
        input_output_aliases={n_in + c: 3 + c for c in range(len(carried))},
        compiler_params=_cparams(1),
        name="mixer_sample",
    )(q4, rs, rs, rs, rs, rs, ck_t, cv_t, state_h, state_conv_t, state_pool_t, bias_s, sinks_t,
      pw["conv_w"], pw["conv_b"], pw["w_gate"], pw["gate_a_b"], pw["gate_x_b"], pw["lam"],
      pw["w_pool"], pw["pool_scale"], *carried)


def _out_ffn_kernel(l, x_ref, at_ref, rp_ref, xs_ref, ats_ref, rps_ref, wo_ref, g1_ref, b1_ref,
                    w1_ref, w2_ref, g2_ref, b2_ref, y_ref, ys_ref, acc_ref):
    @pl.when(pl.program_id(0) == pl.num_programs(0) - 1)
    def _():
        mix = jnp.dot(ats_ref[...], wo_ref[:ATTN_WIDTH, :], preferred_element_type=F32)
        mix = mix + jnp.dot(rps_ref[...], wo_ref[ATTN_WIDTH:, :], preferred_element_type=F32)
        x1 = _layer_norm(ALPHA * xs_ref[...] + mix, g1_ref[l:l + 1, :], b1_ref[l:l + 1, :])
        hid = jnp.dot(x1.astype(BF16), w1_ref[...], preferred_element_type=F32)
        hid = jnp.square(jnp.maximum(hid, 0.0)).astype(BF16)
        ffn = jnp.dot(hid, w2_ref[...], preferred_element_type=F32)
        ys_ref[...] = _layer_norm(ALPHA * x1 + ffn, g2_ref[l:l + 1, :], b2_ref[l:l + 1, :])

    tm = x_ref.shape[0]
    sub = FFN_SUB
    groups = [slice(g * sub, (g + 1) * sub) for g in range(tm // sub)]
    mixes = []
    for rows in groups:
        mix = jnp.dot(at_ref[rows, :], wo_ref[:ATTN_WIDTH, :], preferred_element_type=F32)
        mixes.append(mix + jnp.dot(rp_ref[rows, :], wo_ref[ATTN_WIDTH:, :], preferred_element_type=F32))
    x1s = [_layer_norm(ALPHA * x_ref[rows, :] + mix, g1_ref[l:l + 1, :], b1_ref[l:l + 1, :])
           for rows, mix in zip(groups, mixes)]
    x1bs = [x1.astype(BF16) for x1 in x1s]
    for c in range(D_FF // FFN_FC):
        cols = slice(c * FFN_FC, (c + 1) * FFN_FC)
        for rows, x1b in zip(groups, x1bs):
            hid = jnp.dot(x1b, w1_ref[:, cols], preferred_element_type=F32)
            hid = jnp.square(jnp.maximum(hid, 0.0)).astype(BF16)
            part = jnp.dot(hid, w2_ref[cols, :], preferred_element_type=F32)
            if c == 0:
                acc_ref[rows, :] = part
            else:
                acc_ref[rows, :] += part
    for rows, x1 in zip(groups, x1s):
        y_ref[rows, :] = _layer_norm(ALPHA * x1 + acc_ref[rows, :], g2_ref[l:l + 1, :], b2_ref[l:l + 1, :])


def out_ffn(l, x2d, attn2d, rp2d, xs, attn_s, rp_s, w_out_b, w_ff1_b, w_ff2_b, fw):
    n = x2d.shape[0]
    tm = FFN_TM
    row = lambda width: pl.BlockSpec((tm, width), lambda i: (i, 0))
    vec = _whole_spec((DEPTH, D_MODEL))
    return pl.pallas_call(
        functools.partial(_out_ffn_kernel, l),
        grid=(n // tm,),
        in_specs=[
            row(D_MODEL), row(ATTN_WIDTH), row(LRU_WIDTH + POOL_WIDTH),
            _whole_spec(xs.shape), _whole_spec(attn_s.shape), _whole_spec(rp_s.shape),
            _whole_spec(w_out_b.shape), vec, vec,
            _whole_spec(w_ff1_b.shape), _whole_spec(w_ff2_b.shape), vec, vec,
        ],
        out_specs=[row(D_MODEL), pl.BlockSpec(xs.shape, lambda i: (0, 0))],
        out_shape=[jax.ShapeDtypeStruct((n, D_MODEL), F32), jax.ShapeDtypeStruct(xs.shape, F32)],
        scratch_shapes=[pltpu.VMEM((tm, D_MODEL), F32)],
        compiler_params=_cparams(1),
        name="out_ffn",
    )(x2d, attn2d, rp2d, xs, attn_s, rp_s, w_out_b, fw["ln1_g"], fw["ln1_b"],
      w_ff1_b, w_ff2_b, fw["ln2_g"], fw["ln2_b"])


def _block_diag(w):
    depth, g, c, d = w.shape
    eye = jnp.eye(g, dtype=bool)[None, :, None, :, None]
    return jnp.where(eye, w[:, :, :, None, :], 0.0).reshape(depth, g * c, g * d)


def _alibi_slopes():
    return np.exp2(-8.0 * (np.arange(N_HEADS, dtype=np.float32) + 1.0) / N_HEADS).astype(np.float32)


def _prompt_bias_tables():
    slopes = _alibi_slopes()
    jk = np.arange(2 * WINDOW)[:, None]
    tq = np.arange(WINDOW)[None, :]
    delta = tq + WINDOW - jk
    visible = (delta >= 0) & (delta <= WINDOW)
    bias = -slopes[None, :, None] * delta.astype(np.float32)[:, None, :]
    full = np.where(visible[:, None, :], bias, np.float32(NEG_INF))
    first = np.where((visible & (jk >= WINDOW))[:, None, :], bias, np.float32(NEG_INF))
    return np.stack([first, full]).reshape(2, 2 * WINDOW, N_HEADS * WINDOW).astype(np.float32)


def kernel(x_prompt, x_sample, cache_k, cache_v, state_h, state_conv, state_pool, w_in, attn_sinks, conv_w, conv_b, gate_a_w, gate_a_b, gate_x_w, gate_x_b, lru_lambda, pool_w, pool_scale, w_out, ln1_g, ln1_b, w_ff1, w_ff2, ln2_g, ln2_b):
    batch, seq, _ = x_prompt.shape
    dec = x_sample.shape[0]
    bias_prompt = jnp.asarray(_prompt_bias_tables())
    bias_sample = jnp.asarray(-_alibi_slopes()[:, None] * (WINDOW - np.arange(WINDOW, dtype=np.float32))[None, :])
    sink_rows = jnp.repeat(attn_sinks, WINDOW, axis=1)
    sinks_t = attn_sinks.T

    pw = {
        "conv_w": conv_w, "conv_b": conv_b,
        "w_gate": (-jnp.concatenate([_block_diag(gate_a_w), _block_diag(gate_x_w)], axis=2)).astype(BF16),
        "gate_a_b": gate_a_b, "gate_x_b": gate_x_b, "lam": lru_lambda,
        "w_pool": _block_diag(pool_w).astype(BF16), "pool_scale": pool_scale,
    }
    fw = {"ln1_g": ln1_g, "ln1_b": ln1_b, "ln2_g": ln2_g, "ln2_b": ln2_b}

    def kv_view(c):
        return jnp.transpose(c, (0, 1, 3, 4, 2)).reshape(c.shape[0], c.shape[1], KV_WIDTH, WINDOW)

    def kv_unview(c_t):
        c5 = c_t.reshape(c_t.shape[0], c_t.shape[1], N_KV_HEADS, HEAD_DIM, WINDOW)
        return jnp.transpose(c5, (0, 1, 4, 2, 3))

    ck_t, cv_t = kv_view(cache_k), kv_view(cache_v)
    state_conv_t = jnp.swapaxes(state_conv, 1, 2)
    state_pool_t = jnp.swapaxes(state_pool, 1, 2)

    yp = x_prompt.reshape(batch * seq, D_MODEL)
    ys = x_sample.reshape(dec, D_MODEL)
    xr_cols = slice(XR_BLK * LRU_WIDTH, (XR_BLK + 1) * LRU_WIDTH)
    zp_cols = slice(ZP_BLK * LRU_WIDTH, (ZP_BLK + 1) * LRU_WIDTH)
    outs = {k: [] for k in ("pk", "pv", "ph", "pc", "pp", "sh", "sc", "sp")}
    new_caches = ()
    for l in range(DEPTH):
        attn, rp, h_last, k_last_t, v_last_t, x_tail, z_tail, qs, rs, w_out_b, w_ff1_b, w_ff2_b = front_prompt(
            l, yp, seq, ys, w_in, bias_prompt, sink_rows, pw, (w_out, w_ff1, w_ff2))
        attn_s, rp_s, h_s, *new_caches = mixer_sample(
            l, qs.reshape(dec, GQA_GROUP, LANES), rs, ck_t, cv_t, state_h, state_conv_t, state_pool_t,
            bias_sample, sinks_t, pw, tuple(new_caches))
        attn_s = attn_s.reshape(dec, N_HEADS, HEAD_DIM)[:, HEAD_ORDER_INV, :].reshape(dec, ATTN_WIDTH)
        yp, ys = out_ffn(l, yp, attn, rp, ys, attn_s, rp_s, w_out_b, w_ff1_b, w_ff2_b, fw)
        outs["pk"].append(k_last_t)
        outs["pv"].append(v_last_t)
        outs["ph"].append(h_last.reshape(batch, LRU_WIDTH))
        outs["pc"].append(x_tail[:, X_TAIL - (CONV_W - 1):, :])
        outs["pp"].append(z_tail[:, Z_TAIL - POOL_CTX:, :])
        outs["sh"].append(h_s)
        outs["sc"].append(jnp.concatenate([state_conv[l][:, 1:], rs[:, None, xr_cols]], axis=1))
        outs["sp"].append(jnp.concatenate([state_pool[l][:, 1:], rs[:, None, zp_cols]], axis=1))

    st = {k: jnp.stack(v) for k, v in outs.items()}
    sk_t, sv_t = new_caches
    return (yp.reshape(batch, seq, D_MODEL), ys.reshape(dec, 1, D_MODEL),
            kv_unview(st["pk"]), kv_unview(st["pv"]), st["ph"], st["pc"], st["pp"],
            kv_unview(sk_t), kv_unview(sv_t), st["sh"], st["sc"], st["sp"])
```

```python
import functools

import jax
import jax.numpy as jnp
import numpy as np
from jax import lax
from jax.experimental import pallas as pl
from jax.experimental.pallas import tpu as pltpu

D_MODEL = 1024
DEPTH = 2
HEAD_DIM = 64
ATTN_WIDTH = 512
N_HEADS = 8
N_KV_HEADS = 2
GQA_GROUP = 4
KV_WIDTH = 128
WINDOW = 128
LRU_WIDTH = 256
LRU_C = 8.0
CONV_W = 4
POOL_WINDOWS = (2, 4, 8, 16)
POOL_WIDTH = 256
POOL_GROUP_W = 64
POOL_CTX = 15
MIX_WIDTH = ATTN_WIDTH + LRU_WIDTH + POOL_WIDTH
IN_WIDTH = 1536
REST_WIDTH = IN_WIDTH - ATTN_WIDTH
D_FF = 4096
LN_EPS = 1e-5
NEG_INF = -1e30
ALPHA = (2.0 * DEPTH) ** 0.25
Q_SCALE = HEAD_DIM ** -0.5

K_BLK, V_BLK = 0, 1
XR_BLK, GR_BLK, ZP_BLK = 1, 2, 3

LANES = 128
SUBLANES = 8
VMEM_LIMIT_BYTES = 56 * 1024 * 1024

TILE = 512
FFN_TM = 1024
FFN_FC = 1024
FFN_SUB = 256
X_TAIL = SUBLANES
Z_TAIL = 2 * SUBLANES
SAMPLE_BT = 16

BF16 = jnp.bfloat16
F32 = jnp.float32

HEAD_ORDER = (0, 4, 1, 5, 2, 6, 3, 7)
HEAD_ORDER_INV = tuple(int(i) for i in np.argsort(HEAD_ORDER))

PROJ_COLS = ((0, ATTN_WIDTH), (ATTN_WIDTH, ATTN_WIDTH + 2 * KV_WIDTH),
             (ATTN_WIDTH + 2 * KV_WIDTH, ATTN_WIDTH + 2 * KV_WIDTH + 2 * LRU_WIDTH),
             (ATTN_WIDTH + 2 * KV_WIDTH + 2 * LRU_WIDTH, IN_WIDTH))


def _cparams(n_grid):
    return pltpu.CompilerParams(
        dimension_semantics=("arbitrary",) * n_grid,
        vmem_limit_bytes=VMEM_LIMIT_BYTES,
    )


def _whole_spec(shape):
    nd = len(shape)
    return pl.BlockSpec(shape, lambda *_: (0,) * nd, pipeline_mode=pl.Buffered(1))


def _layer_spec(shape, l):
    nd = len(shape) - 1
    return pl.BlockSpec((None,) + tuple(shape[1:]), lambda *_: (l,) + (0,) * nd, pipeline_mode=pl.Buffered(1))


def _layer_norm(x, g, b):
    mu = jnp.mean(x, axis=-1, keepdims=True)
    xc = x - mu
    var = jnp.mean(xc * xc, axis=-1, keepdims=True)
    return xc * lax.rsqrt(var + LN_EPS) * g + b


def _gelu_tanh(x):
    c = np.sqrt(2.0 / np.pi)
    half = 0.5 * x
    return half + half * jnp.tanh(x * (c + (c * 0.044715) * (x * x)))


def _sigmoid_of_neg(z):
    return 1.0 / (1.0 + jnp.exp(z))


def _softplus(x):
    return jnp.maximum(x, 0.0) + jnp.log(1.0 + jnp.exp(-jnp.abs(x)))


def _interleave(*gens):
    active = list(gens)
    while active:
        for g in list(active):
            try:
                next(g)
            except StopIteration:
                active.remove(g)


def _attn_scores(q, k2, bias_t, lo):
    zero = jnp.zeros((), q.dtype)
    tiles = [q[:, c * LANES:(c + 1) * LANES] for c in range(GQA_GROUP)]
    qs = jnp.concatenate([jnp.where(lo, t, zero) for t in tiles]
                         + [jnp.where(lo, zero, t) for t in tiles], axis=0)
    return lax.dot_general(k2, qs, (((1,), (1,)), ((), ())), preferred_element_type=F32) + bias_t


def _attn_probs(s, sink):
    m = jnp.maximum(jnp.max(s, axis=0, keepdims=True), sink)
    p = jnp.exp(s - m)
    denom = jnp.sum(p, axis=0, keepdims=True) + jnp.exp(sink - m)
    return p.astype(BF16), 1.0 / denom


def _attn_values(p, v2):
    return lax.dot_general(v2, p, (((0,), (0,)), ((), ())), preferred_element_type=F32)


def _attn_output(o, inv_denom):
    cols = []
    for c in range(ATTN_WIDTH // LANES):
        kv = (2 * c) // GQA_GROUP
        rows = slice(kv * HEAD_DIM, (kv + 1) * HEAD_DIM)
        heads = [slice(n * WINDOW, (n + 1) * WINDOW) for n in (2 * c, 2 * c + 1)]
        blk = jnp.concatenate([o[rows, h] * inv_denom[:, h] for h in heads], axis=0)
        cols.append(blk.T)
    return jnp.concatenate(cols, axis=1)


def _conv(l, xr, x_tail, cw_ref, cb_ref):
    xe = jnp.concatenate([x_tail, xr], axis=0)
    cw = cw_ref[...]
    xc = cb_ref[l:l + 1, :] + xr * cw[CONV_W - 1:CONV_W, :]
    for tap in range(CONV_W - 1):
        d = CONV_W - 1 - tap
        xc = xc + _shift_rows(xe, d)[X_TAIL:] * cw[tap:tap + 1, :]
    return xc


def _gate_logits(xc, wg_ref):
    return jnp.dot(xc.astype(BF16), wg_ref[...], preferred_element_type=F32)


def _gates(l, g_neg, xc, ba_ref, bx_ref, lam_ref):
    r = _sigmoid_of_neg(g_neg[:, :LRU_WIDTH] - ba_ref[l:l + 1, :])
    i = _sigmoid_of_neg(g_neg[:, LRU_WIDTH:] - bx_ref[l:l + 1, :])
    log_a = r * (-LRU_C * _softplus(-lam_ref[l:l + 1, :]))
    a = jnp.exp(log_a)
    b = jnp.sqrt(1.0 - a * a) * (i * xc)
    return a, b


def _lru_scan(a, b, h_prev):
    t, w = a.shape
    groups = t // SUBLANES
    a3 = a.reshape(groups, SUBLANES, w)
    b3 = b.reshape(groups, SUBLANES, w)
    row = lax.broadcasted_iota(jnp.int32, (groups, SUBLANES, w), 1)
    s = 1
    while s < SUBLANES:
        keep = row >= s
        a_sh = jnp.where(keep, pltpu.roll(a3, s, 1), 1.0)
        b_sh = jnp.where(keep, pltpu.roll(b3, s, 1), 0.0)
        b3 = a3 * b_sh + b3
        a3 = a3 * a_sh
        s *= 2
    hs = []
    h = h_prev
    for g in range(groups):
        hg = a3[g] * h + b3[g]
        hs.append(hg)
        h = hg[SUBLANES - 1:SUBLANES, :]
    return jnp.concatenate(hs, axis=0), h


def _shift_rows(x, d):
    return pltpu.roll(x, d, 0)


def _pool_means(zp, z_tail, pos1):
    ze = jnp.concatenate([z_tail, zp], axis=0)
    t = zp.shape[0]
    lo = lax.broadcasted_iota(jnp.int32, (t, LANES), 1) < POOL_GROUP_W
    s2 = ze + _shift_rows(ze, 1)
    s4 = s2 + _shift_rows(s2, 2)
    hi4 = s4[:, LANES:]
    s8 = hi4 + _shift_rows(hi4, 4)
    s16 = s8 + _shift_rows(s8, 8)
    wins = (jnp.where(lo, s2[Z_TAIL:, :LANES], s4[Z_TAIL:, :LANES]),
            jnp.where(lo, s8[Z_TAIL:], s16[Z_TAIL:]))
    means = []
    for c, win in enumerate(wins):
        w_small, w_big = POOL_WINDOWS[2 * c], POOL_WINDOWS[2 * c + 1]
        if pos1 is None:
            mean = win * jnp.where(lo, 1.0 / w_small, 1.0 / w_big)
        else:
            mean = win / jnp.minimum(pos1, jnp.where(lo, float(w_small), float(w_big)))
        means.append(mean - zp[:, c * LANES:(c + 1) * LANES])
    return jnp.concatenate(means, axis=1)


def _prep_w_in(w_ref, wb_ref):
    lo = lax.broadcasted_iota(jnp.int32, (D_MODEL, LANES), 1) < HEAD_DIM
    src = [w_ref[:, s * LANES:(s + 1) * LANES] * Q_SCALE for s in range(ATTN_WIDTH // LANES)]
    swapped = [pltpu.roll(t, HEAD_DIM, 1) for t in src]
    for c in range(GQA_GROUP):
        s0, s1 = c // 2, GQA_GROUP // 2 + c // 2
        if c % 2 == 0:
            tile = jnp.where(lo, src[s0], swapped[s1])
        else:
            tile = jnp.where(lo, swapped[s0], src[s1])
        wb_ref[:, c * LANES:(c + 1) * LANES] = tile.astype(BF16)
    wb_ref[:, ATTN_WIDTH:] = w_ref[:, ATTN_WIDTH:].astype(BF16)


def _front_kernel(l, tiles_per_seq, xp_ref, xs_ref, w_ref, bias0_ref, bias_ref, sink_ref,
                  cw_ref, cb_ref, wg_ref, ba_ref, bx_ref, lam_ref, wp_ref, ps_ref,
                  wo_f_ref, w1_f_ref, w2_f_ref,
                  attn_ref, rp_ref, h_ref, kt_ref, vt_ref, xt_ref, zt_ref, qs_ref, rs_ref,
                  wo_b_ref, w1_b_ref, w2_b_ref,
                  wb_ref, q_scr, r_scr, kprev, vprev, xtail, ztail, hcar):
    wo_b_ref[...] = wo_f_ref[...].astype(BF16)
    w1_b_ref[...] = w1_f_ref[...].astype(BF16)
    w2_b_ref[...] = w2_f_ref[...].astype(BF16)

    s = pl.program_id(0)
    slot_p = lax.rem(s, 2)
    slot_m = 1 - slot_p
    j = lax.rem(s + tiles_per_seq - 1, tiles_per_seq)

    @pl.when(s == 0)
    def _():
        _prep_w_in(w_ref, wb_ref)
        us = jnp.dot(xs_ref[...].astype(BF16), wb_ref[...], preferred_element_type=F32)
        qs_ref[...] = us[:, :ATTN_WIDTH].astype(BF16)
        rs_ref[...] = us[:, ATTN_WIDTH:]
        q_scr[1] = jnp.zeros(q_scr.shape[1:], q_scr.dtype)
        r_scr[1] = jnp.zeros(r_scr.shape[1:], r_scr.dtype)
        kprev[...] = jnp.zeros_like(kprev)
        vprev[...] = jnp.zeros_like(vprev)

    @pl.when((j == 0) | (s == 0))
    def _():
        xtail[...] = jnp.zeros_like(xtail)
        ztail[...] = jnp.zeros_like(ztail)
        hcar[...] = jnp.zeros_like(hcar)

    carry = {}
    sub_blocks = [slice(c * WINDOW, (c + 1) * WINDOW) for c in range(TILE // WINDOW)]

    def attention():
        lo = lax.broadcasted_iota(jnp.int32, (WINDOW, LANES), 1) < HEAD_DIM
        sink = sink_ref[l:l + 1, :]
        xb16 = xp_ref[...].astype(BF16)
        k_prev, v_prev = kprev[...], vprev[...]
        k = v = None
        for c, rows in enumerate(sub_blocks):
            k = r_scr[slot_m, rows, K_BLK * KV_WIDTH:(K_BLK + 1) * KV_WIDTH]
            v = r_scr[slot_m, rows, V_BLK * KV_WIDTH:(V_BLK + 1) * KV_WIDTH]
            kb, vb = k.astype(BF16), v.astype(BF16)
            bias_t = bias0_ref[...] if c == 0 else bias_ref[...]
            sc = _attn_scores(q_scr[slot_m, rows, :], jnp.concatenate([k_prev, kb], axis=0), bias_t, lo)
            yield
            c0, c1 = PROJ_COLS[c]
            u = jnp.dot(xb16, wb_ref[:, c0:c1], preferred_element_type=F32)
            if c == 0:
                q_scr[slot_p] = u.astype(BF16)
            else:
                r_scr[slot_p, :, c0 - ATTN_WIDTH:c1 - ATTN_WIDTH] = u
            yield
            p, inv_denom = _attn_probs(sc, sink)
            o = _attn_values(p, jnp.concatenate([v_prev, vb], axis=0))
            yield
            attn_ref[rows, :] = _attn_output(o, inv_denom).astype(attn_ref.dtype)
            k_prev, v_prev = kb, vb
        carry.update(k_prev=k_prev, v_prev=v_prev, k=k, v=v)

    def mixers():
        x_tail, z_tail, h = xtail[...], ztail[...], hcar[...]
        for c, rows in enumerate(sub_blocks):
            xr = r_scr[slot_m, rows, XR_BLK * LRU_WIDTH:(XR_BLK + 1) * LRU_WIDTH]
            gr = r_scr[slot_m, rows, GR_BLK * LRU_WIDTH:(GR_BLK + 1) * LRU_WIDTH]
            zp = r_scr[slot_m, rows, ZP_BLK * LRU_WIDTH:(ZP_BLK + 1) * LRU_WIDTH]
            xc = _conv(l, xr, x_tail, cw_ref, cb_ref)
            g = _gate_logits(xc, wg_ref)
            yield
            a, b = _gates(l, g, xc, ba_ref, bx_ref, lam_ref)
            hs, h = _lru_scan(a, b, h)
            rec = hs * _gelu_tanh(gr)
            pos1 = None
            if c == 0:
                pos1 = (j * TILE + lax.broadcasted_iota(jnp.int32, (WINDOW, LANES), 0) + 1).astype(F32)
            diff = _pool_means(zp, z_tail, pos1).astype(BF16)
            pool = jnp.dot(diff, wp_ref[...], preferred_element_type=F32) * ps_ref[l:l + 1, :]
            yield
            rp_ref[rows, :] = jnp.concatenate([rec, pool], axis=1).astype(rp_ref.dtype)
            x_tail, z_tail = xr[WINDOW - X_TAIL:, :], zp[WINDOW - Z_TAIL:, :]
        carry.update(x_tail=x_tail, z_tail=z_tail, h=h)

    _interleave(attention(), mixers())

    kprev[...] = carry["k_prev"]
    vprev[...] = carry["v_prev"]
    xtail[...] = carry["x_tail"]
    ztail[...] = carry["z_tail"]
    hcar[...] = carry["h"]

    @pl.when(j == tiles_per_seq - 1)
    def _():
        kt_ref[...] = carry["k"].T
        vt_ref[...] = carry["v"].T
        h_ref[...] = carry["h"]
        xt_ref[...] = carry["x_tail"]
        zt_ref[...] = carry["z_tail"]


def front_prompt(l, x2d, seq, xs, w_in, bias2, sink_rows, pw, later_w):
    n = x2d.shape[0]
    n_s = xs.shape[0]
    slab = lambda s: jnp.minimum(s, n // TILE - 1)
    slab_rows = [w.shape[1] // (n // TILE) for w in later_w]
    slab_in = [pl.BlockSpec((None, r, w.shape[2]), lambda s: (l, slab(s), 0)) for r, w in zip(slab_rows, later_w)]
    slab_out = [pl.BlockSpec((r, w.shape[2]), lambda s: (slab(s), 0)) for r, w in zip(slab_rows, later_w)]
    b = n // seq
    n_tiles = n // TILE
    tiles_per_seq = seq // TILE
    done = lambda s: jnp.maximum(s - 1, 0)
    seq_of = lambda s: done(s) // tiles_per_seq
    bias_shape = (None, 2 * WINDOW, N_HEADS * WINDOW)
    per_seq = lambda rows, width: pl.BlockSpec((None, rows, width), lambda s: (seq_of(s), 0, 0))
    return pl.pallas_call(
        functools.partial(_front_kernel, l, tiles_per_seq),
        grid=(n_tiles + 1,),
        in_specs=[
            pl.BlockSpec((TILE, D_MODEL), lambda s: (jnp.minimum(s, n_tiles - 1), 0)),
            _whole_spec(xs.shape),
            _layer_spec(w_in.shape, l),
            pl.BlockSpec(bias_shape, lambda s: (jnp.minimum(done(s) % tiles_per_seq, 1), 0, 0)),
            pl.BlockSpec(bias_shape, lambda s: (1, 0, 0), pipeline_mode=pl.Buffered(1)),
            _whole_spec(sink_rows.shape),
            _layer_spec(pw["conv_w"].shape, l), _whole_spec(pw["conv_b"].shape),
            _layer_spec(pw["w_gate"].shape, l), _whole_spec(pw["gate_a_b"].shape),
            _whole_spec(pw["gate_x_b"].shape), _whole_spec(pw["lam"].shape),
            _layer_spec(pw["w_pool"].shape, l), _whole_spec(pw["pool_scale"].shape),
        ] + slab_in,
        out_specs=[
            pl.BlockSpec((TILE, ATTN_WIDTH), lambda s: (done(s), 0)),
            pl.BlockSpec((TILE, LRU_WIDTH + POOL_WIDTH), lambda s: (done(s), 0)),
            per_seq(1, LRU_WIDTH),
            per_seq(KV_WIDTH, WINDOW), per_seq(KV_WIDTH, WINDOW),
            per_seq(X_TAIL, LRU_WIDTH), per_seq(Z_TAIL, POOL_WIDTH),
            pl.BlockSpec((n_s, ATTN_WIDTH), lambda s: (0, 0)),
            pl.BlockSpec((n_s, REST_WIDTH), lambda s: (0, 0)),
        ] + slab_out,
        out_shape=[
            jax.ShapeDtypeStruct((n, ATTN_WIDTH), BF16),
            jax.ShapeDtypeStruct((n, LRU_WIDTH + POOL_WIDTH), BF16),
            jax.ShapeDtypeStruct((b, 1, LRU_WIDTH), F32),
            jax.ShapeDtypeStruct((b, KV_WIDTH, WINDOW), F32),
            jax.ShapeDtypeStruct((b, KV_WIDTH, WINDOW), F32),
            jax.ShapeDtypeStruct((b, X_TAIL, LRU_WIDTH), F32),
            jax.ShapeDtypeStruct((b, Z_TAIL, POOL_WIDTH), F32),
            jax.ShapeDtypeStruct((n_s, ATTN_WIDTH), BF16),
            jax.ShapeDtypeStruct((n_s, REST_WIDTH), F32),
        ] + [jax.ShapeDtypeStruct(w.shape[1:], BF16) for w in later_w],
        scratch_shapes=[
            pltpu.VMEM((D_MODEL, IN_WIDTH), BF16),
            pltpu.VMEM((2, TILE, ATTN_WIDTH), BF16),
            pltpu.VMEM((2, TILE, REST_WIDTH), F32),
            pltpu.VMEM((WINDOW, KV_WIDTH), BF16),
            pltpu.VMEM((WINDOW, KV_WIDTH), BF16),
            pltpu.VMEM((X_TAIL, LRU_WIDTH), F32),
            pltpu.VMEM((Z_TAIL, POOL_WIDTH), F32),
            pltpu.VMEM((1, LRU_WIDTH), F32),
        ],
        compiler_params=_cparams(1),
        name="front_prompt",
    )(x2d, xs, w_in, bias2, bias2, sink_rows, pw["conv_w"], pw["conv_b"], pw["w_gate"], pw["gate_a_b"],
      pw["gate_x_b"], pw["lam"], pw["w_pool"], pw["pool_scale"], *later_w)


def _mixer_sample_kernel(l, n_earlier, q_ref, kn_ref, vn_ref, xr_ref, gr_ref, zp_ref, ck_ref, cv_ref,
                         h0_ref, sc_ref, sp_ref, bias_ref, sink_ref,
                         cw_ref, cb_ref, wg_ref, ba_ref, bx_ref, lam_ref, wp_ref, ps_ref, *rest):
    earlier_k, earlier_v = rest[:n_earlier], rest[n_earlier:2 * n_earlier]
    attn_ref, rp_ref, h_ref, ok_ref, ov_ref = rest[2 * n_earlier:]
    for e in range(n_earlier):
        ok_ref[e] = earlier_k[e][...]
        ov_ref[e] = earlier_v[e][...]
    if n_earlier:
        ok_ref, ov_ref = ok_ref.at[l], ov_ref.at[l]
    bt = SAMPLE_BT
    lo3 = lax.broadcasted_iota(jnp.int32, (bt, GQA_GROUP, LANES), 2) < HEAD_DIM
    q4 = q_ref[...].astype(F32)
    qm = jnp.concatenate([jnp.where(lo3, q4, 0.0), jnp.where(lo3, 0.0, q4)], axis=1)
    s = jnp.einsum("bqc,bck->bqk", qm.astype(BF16), ck_ref[...].astype(BF16),
                   preferred_element_type=F32) + bias_ref[...]
    s_new = jnp.sum(qm * kn_ref[...][:, None, :], axis=2, keepdims=True)
    sink = sink_ref[:, l:l + 1]
    m = jnp.maximum(jnp.maximum(jnp.max(s, axis=2, keepdims=True), s_new), sink)
    p = jnp.exp(s - m)
    p_new = jnp.exp(s_new - m)
    denom = jnp.sum(p, axis=2, keepdims=True) + p_new + jnp.exp(sink - m)
    o = jnp.einsum("bqk,bck->bqc", p.astype(BF16), cv_ref[...].astype(BF16),
                   preferred_element_type=F32)
    o = (o + p_new * vn_ref[...][:, None, :]) / denom
    attn_ref[...] = jnp.where(lo3, o[:, :GQA_GROUP, :], o[:, GQA_GROUP:, :]).astype(attn_ref.dtype)

    xr = xr_ref[...]
    cw = cw_ref[...]
    xc = cb_ref[l:l + 1, :] + xr * cw[CONV_W - 1:CONV_W, :]
    for tap in range(CONV_W - 1):
        xc = xc + sc_ref[tap] * cw[tap:tap + 1, :]
    a, b = _gates(l, _gate_logits(xc, wg_ref), xc, ba_ref, bx_ref, lam_ref)
    h = a * h0_ref[...] + b
    h_ref[...] = h
    rec = h * _gelu_tanh(gr_ref[...])

    z = zp_ref[...]
    lo = lax.broadcasted_iota(jnp.int32, (bt, LANES), 1) < POOL_GROUP_W
    means = []
    for c in range(POOL_WIDTH // LANES):
        w_small, w_big = POOL_WINDOWS[2 * c], POOL_WINDOWS[2 * c + 1]
        cols = slice(c * LANES, (c + 1) * LANES)
        zc = z[:, cols]
        acc = zc
        small = None
        for d in range(1, w_big):
            acc = acc + sp_ref[POOL_CTX - d][:, cols]
            if d + 1 == w_small:
                small = acc
        win = jnp.where(lo, small, acc)
        count = jnp.where(lo, float(w_small), float(w_big))
        means.append(win / count - zc)
    diff = jnp.concatenate(means, axis=1).astype(BF16)
    pool = jnp.dot(diff, wp_ref[...], preferred_element_type=F32) * ps_ref[l:l + 1, :]
    rp_ref[...] = jnp.concatenate([rec, pool], axis=1).astype(rp_ref.dtype)

    last = lax.broadcasted_iota(jnp.int32, (KV_WIDTH, WINDOW), 1) == WINDOW - 1
    pad = jnp.zeros((LANES - bt, KV_WIDTH), F32)
    for src, new, dst in ((ck_ref, kn_ref, ok_ref), (cv_ref, vn_ref, ov_ref)):
        new_t = jnp.concatenate([new[...], pad], axis=0).T
        for s in range(bt):
            shifted = pltpu.roll(src[s], WINDOW - 1, 1)
            col = pltpu.roll(new_t, WINDOW - 1 - s, 1)
            dst[s] = jnp.where(last, col, shifted)


def mixer_sample(l, q4, rs, ck_t, cv_t, state_h, state_conv_t, state_pool_t, bias_s, sinks_t, pw,
                 earlier_k=(), earlier_v=()):
    n = rs.shape[0]
    bt = SAMPLE_BT
    n_earlier = len(earlier_k)
    cache_spec = pl.BlockSpec((None, bt, KV_WIDTH, WINDOW), lambda i: (l, i, 0, 0))
    layer_cache = pl.BlockSpec((bt, KV_WIDTH, WINDOW), lambda i: (i, 0, 0))
    if n_earlier:
        new_cache_spec = pl.BlockSpec((n_earlier + 1, bt, KV_WIDTH, WINDOW), lambda i: (0, i, 0, 0))
        new_cache_shape = jax.ShapeDtypeStruct((n_earlier + 1, n, KV_WIDTH, WINDOW), F32)
    else:
        new_cache_spec = layer_cache
        new_cache_shape = jax.ShapeDtypeStruct((n, KV_WIDTH, WINDOW), F32)
    rcol = lambda width, c: pl.BlockSpec((bt, width), lambda i: (i, c))
    return pl.pallas_call(
        functools.partial(_mixer_sample_kernel, l, n_earlier),
        grid=(n // bt,),
        in_specs=[
            pl.BlockSpec((bt, GQA_GROUP, LANES), lambda i: (i, 0, 0)),
            rcol(KV_WIDTH, K_BLK), rcol(KV_WIDTH, V_BLK),
            rcol(LRU_WIDTH, XR_BLK), rcol(LRU_WIDTH, GR_BLK), rcol(POOL_WIDTH, ZP_BLK),
            cache_spec, cache_spec,
            pl.BlockSpec((None, bt, LRU_WIDTH), lambda i: (l, i, 0)),
            pl.BlockSpec((None, CONV_W - 1, bt, LRU_WIDTH), lambda i: (l, 0, i, 0)),
            pl.BlockSpec((None, POOL_CTX, bt, POOL_WIDTH), lambda i: (l, 0, i, 0)),
            _whole_spec(bias_s.shape),
            _whole_spec(sinks_t.shape),
            _layer_spec(pw["conv_w"].shape, l), _whole_spec(pw["conv_b"].shape),
            _layer_spec(pw["w_gate"].shape, l), _whole_spec(pw["gate_a_b"].shape),
            _whole_spec(pw["gate_x_b"].shape), _whole_spec(pw["lam"].shape),
            _layer_spec(pw["w_pool"].shape, l), _whole_spec(pw["pool_scale"].shape),
        ] + [layer_cache] * (2 * n_earlier),
        out_specs=[
            pl.BlockSpec((bt, GQA_GROUP, LANES), lambda i: (i, 0, 0)),
            pl.BlockSpec((bt, LRU_WIDTH + POOL_WIDTH), lambda i: (i, 0)),
            pl.BlockSpec((bt, LRU_WIDTH), lambda i: (i, 0)),
            new_cache_spec, new_cache_spec,
        ],
        out_shape=[
            jax.ShapeDtypeStruct((n, GQA_GROUP, LANES), BF16),
            jax.ShapeDtypeStruct((n, LRU_WIDTH + POOL_WIDTH), BF16),
            jax.ShapeDtypeStruct((n, LRU_WIDTH), F32),
            new_cache_shape, new_cache_shape,
        ],
        compiler_params=_cparams(1),
        name="mixer_sample",
    )(q4, rs, rs, rs, rs, rs, ck_t, cv_t, state_h, state_conv_t, state_pool_t, bias_s, sinks_t,
      pw["conv_w"], pw["conv_b"], pw["w_gate"], pw["gate_a_b"], pw["gate_x_b"], pw["lam"],
      pw["w_pool"], pw["pool_scale"], *earlier_k, *earlier_v)


def _out_ffn_kernel(l, x_ref, at_ref, rp_ref, xs_ref, ats_ref, rps_ref, wo_ref, g1_ref, b1_ref,
                    w1_ref, w2_ref, g2_ref, b2_ref, y_ref, ys_ref, acc_ref):
    @pl.when(pl.program_id(0) == pl.num_programs(0) - 1)
    def _():
        mix = jnp.dot(ats_ref[...], wo_ref[:ATTN_WIDTH, :], preferred_element_type=F32)
        mix = mix + jnp.dot(rps_ref[...], wo_ref[ATTN_WIDTH:, :], preferred_element_type=F32)
        x1 = _layer_norm(ALPHA * xs_ref[...] + mix, g1_ref[l:l + 1, :], b1_ref[l:l + 1, :])
        hid = jnp.dot(x1.astype(BF16), w1_ref[...], preferred_element_type=F32)
        hid = jnp.square(jnp.maximum(hid, 0.0)).astype(BF16)
        ffn = jnp.dot(hid, w2_ref[...], preferred_element_type=F32)
        ys_ref[...] = _layer_norm(ALPHA * x1 + ffn, g2_ref[l:l + 1, :], b2_ref[l:l + 1, :])

    tm = x_ref.shape[0]
    sub = FFN_SUB
    groups = [slice(g * sub, (g + 1) * sub) for g in range(tm // sub)]
    mixes = []
    for rows in groups:
        mix = jnp.dot(at_ref[rows, :], wo_ref[:ATTN_WIDTH, :], preferred_element_type=F32)
        mixes.append(mix + jnp.dot(rp_ref[rows, :], wo_ref[ATTN_WIDTH:, :], preferred_element_type=F32))
    x1s = [_layer_norm(ALPHA * x_ref[rows, :] + mix, g1_ref[l:l + 1, :], b1_ref[l:l + 1, :])
           for rows, mix in zip(groups, mixes)]
    x1bs = [x1.astype(BF16) for x1 in x1s]
    for c in range(D_FF // FFN_FC):
        cols = slice(c * FFN_FC, (c + 1) * FFN_FC)
        for rows, x1b in zip(groups, x1bs):
            hid = jnp.dot(x1b, w1_ref[:, cols], preferred_element_type=F32)
            hid = jnp.square(jnp.maximum(hid, 0.0)).astype(BF16)
            part = jnp.dot(hid, w2_ref[cols, :], preferred_element_type=F32)
            if c == 0:
                acc_ref[rows, :] = part
            else:
                acc_ref[rows, :] += part
    for rows, x1 in zip(groups, x1s):
        y_ref[rows, :] = _layer_norm(ALPHA * x1 + acc_ref[rows, :], g2_ref[l:l + 1, :], b2_ref[l:l + 1, :])


def out_ffn(l, x2d, attn2d, rp2d, xs, attn_s, rp_s, w_out_b, w_ff1_b, w_ff2_b, fw):
    n = x2d.shape[0]
    tm = FFN_TM
    row = lambda width: pl.BlockSpec((tm, width), lambda i: (i, 0))
    vec = _whole_spec((DEPTH, D_MODEL))
    return pl.pallas_call(
        functools.partial(_out_ffn_kernel, l),
        grid=(n // tm,),
        in_specs=[
            row(D_MODEL), row(ATTN_WIDTH), row(LRU_WIDTH + POOL_WIDTH),
            _whole_spec(xs.shape), _whole_spec(attn_s.shape), _whole_spec(rp_s.shape),
            _whole_spec(w_out_b.shape), vec, vec,
            _whole_spec(w_ff1_b.shape), _whole_spec(w_ff2_b.shape), vec, vec,
        ],
        out_specs=[row(D_MODEL), pl.BlockSpec(xs.shape, lambda i: (0, 0))],
        out_shape=[jax.ShapeDtypeStruct((n, D_MODEL), F32), jax.ShapeDtypeStruct(xs.shape, F32)],
        scratch_shapes=[pltpu.VMEM((tm, D_MODEL), F32)],
        compiler_params=_cparams(1),
        name="out_ffn",
    )(x2d, attn2d, rp2d, xs, attn_s, rp_s, w_out_b, fw["ln1_g"], fw["ln1_b"],
      w_ff1_b, w_ff2_b, fw["ln2_g"], fw["ln2_b"])


def _block_diag(w):
    depth, g, c, d = w.shape
    eye = jnp.eye(g, dtype=bool)[None, :, None, :, None]
    return jnp.where(eye, w[:, :, :, None, :], 0.0).reshape(depth, g * c, g * d)


def _alibi_slopes():
    return np.exp2(-8.0 * (np.arange(N_HEADS, dtype=np.float32) + 1.0) / N_HEADS).astype(np.float32)


def _prompt_bias_tables():
    slopes = _alibi_slopes()
    jk = np.arange(2 * WINDOW)[:, None]
    tq = np.arange(WINDOW)[None, :]
    delta = tq + WINDOW - jk
    visible = (delta >= 0) & (delta <= WINDOW)
    bias = -slopes[None, :, None] * delta.astype(np.float32)[:, None, :]
    full = np.where(visible[:, None, :], bias, np.float32(NEG_INF))
    first = np.where((visible & (jk >= WINDOW))[:, None, :], bias, np.float32(NEG_INF))
    return np.stack([first, full]).reshape(2, 2 * WINDOW, N_HEADS * WINDOW).astype(np.float32)


def kernel(x_prompt, x_sample, cache_k, cache_v, state_h, state_conv, state_pool, w_in, attn_sinks, conv_w, conv_b, gate_a_w, gate_a_b, gate_x_w, gate_x_b, lru_lambda, pool_w, pool_scale, w_out, ln1_g, ln1_b, w_ff1, w_ff2, ln2_g, ln2_b):
    batch, seq, _ = x_prompt.shape
    dec = x_sample.shape[0]
    bias_prompt = jnp.asarray(_prompt_bias_tables())
    bias_sample = jnp.asarray(-_alibi_slopes()[:, None] * (WINDOW - np.arange(WINDOW, dtype=np.float32))[None, :])
    sink_rows = jnp.repeat(attn_sinks, WINDOW, axis=1)
    sinks_t = attn_sinks.T

    pw = {
        "conv_w": conv_w, "conv_b": conv_b,
        "w_gate": (-jnp.concatenate([_block_diag(gate_a_w), _block_diag(gate_x_w)], axis=2)).astype(BF16),
        "gate_a_b": gate_a_b, "gate_x_b": gate_x_b, "lam": lru_lambda,
        "w_pool": _block_diag(pool_w).astype(BF16), "pool_scale": pool_scale,
    }
    fw = {"ln1_g": ln1_g, "ln1_b": ln1_b, "ln2_g": ln2_g, "ln2_b": ln2_b}

    def kv_view(c):
        return jnp.transpose(c, (0, 1, 3, 4, 2)).reshape(c.shape[0], c.shape[1], KV_WIDTH, WINDOW)

    def kv_unview(c_t):
        c5 = c_t.reshape(c_t.shape[0], c_t.shape[1], N_KV_HEADS, HEAD_DIM, WINDOW)
        return jnp.transpose(c5, (0, 1, 4, 2, 3))

    ck_t, cv_t = kv_view(cache_k), kv_view(cache_v)
    state_conv_t = jnp.swapaxes(state_conv, 1, 2)
    state_pool_t = jnp.swapaxes(state_pool, 1, 2)

    yp = x_prompt.reshape(batch * seq, D_MODEL)
    ys = x_sample.reshape(dec, D_MODEL)
    xr_cols = slice(XR_BLK * LRU_WIDTH, (XR_BLK + 1) * LRU_WIDTH)
    zp_cols = slice(ZP_BLK * LRU_WIDTH, (ZP_BLK + 1) * LRU_WIDTH)
    outs = {k: [] for k in ("pk", "pv", "ph", "pc", "pp", "sh", "sc", "sp")}
    new_k, new_v = [], []
    for l in range(DEPTH):
        attn, rp, h_last, k_last_t, v_last_t, x_tail, z_tail, qs, rs, w_out_b, w_ff1_b, w_ff2_b = front_prompt(
            l, yp, seq, ys, w_in, bias_prompt, sink_rows, pw, (w_out, w_ff1, w_ff2))
        last = l == DEPTH - 1
        attn_s, rp_s, h_s, k_new_t, v_new_t = mixer_sample(
            l, qs.reshape(dec, GQA_GROUP, LANES), rs, ck_t, cv_t, state_h, state_conv_t, state_pool_t,
            bias_sample, sinks_t, pw, *((new_k, new_v) if last else ()))
        new_k.append(k_new_t)
        new_v.append(v_new_t)
        attn_s = attn_s.reshape(dec, N_HEADS, HEAD_DIM)[:, HEAD_ORDER_INV, :].reshape(dec, ATTN_WIDTH)
        yp, ys = out_ffn(l, yp, attn, rp, ys, attn_s, rp_s, w_out_b, w_ff1_b, w_ff2_b, fw)
        outs["pk"].append(k_last_t)
        outs["pv"].append(v_last_t)
        outs["ph"].append(h_last.reshape(batch, LRU_WIDTH))
        outs["pc"].append(x_tail[:, X_TAIL - (CONV_W - 1):, :])
        outs["pp"].append(z_tail[:, Z_TAIL - POOL_CTX:, :])
        outs["sh"].append(h_s)
        outs["sc"].append(jnp.concatenate([state_conv[l][:, 1:], rs[:, None, xr_cols]], axis=1))
        outs["sp"].append(jnp.concatenate([state_pool[l][:, 1:], rs[:, None, zp_cols]], axis=1))

    st = {k: jnp.stack(v) for k, v in outs.items()}
    sk_t, sv_t = new_k[-1], new_v[-1]
    return (yp.reshape(batch, seq, D_MODEL), ys.reshape(dec, 1, D_MODEL),
            kv_unview(st["pk"]), kv_unview(st["pv"]), st["ph"], st["pc"], st["pp"],
            kv_unview(sk_t), kv_unview(sv_t), st["sh"], st["sc"], st["sp"])
```

```python
import functools

import jax
import jax.numpy as jnp
import numpy as np
from jax import lax
from jax.experimental import pallas as pl
from jax.experimental.pallas import tpu as pltpu

D_MODEL = 1024
DEPTH = 2
HEAD_DIM = 64
ATTN_WIDTH = 512
N_HEADS = 8
N_KV_HEADS = 2
GQA_GROUP = 4
KV_WIDTH = 128
WINDOW = 128
LRU_WIDTH = 256
LRU_C = 8.0
CONV_W = 4
POOL_WINDOWS = (2, 4, 8, 16)
POOL_WIDTH = 256
POOL_GROUP_W = 64
POOL_CTX = 15
MIX_WIDTH = ATTN_WIDTH + LRU_WIDTH + POOL_WIDTH
IN_WIDTH = 1536
REST_WIDTH = IN_WIDTH - ATTN_WIDTH
D_FF = 4096
LN_EPS = 1e-5
NEG_INF = -1e30
ALPHA = (2.0 * DEPTH) ** 0.25
Q_SCALE = HEAD_DIM ** -0.5

K_BLK, V_BLK = 0, 1
XR_BLK, GR_BLK, ZP_BLK = 1, 2, 3

LANES = 128
SUBLANES = 8
VMEM_LIMIT_BYTES = 56 * 1024 * 1024

TILE = 1024
FFN_TM = 1024
FFN_FC = 1024
FFN_SUB = 256
X_TAIL = SUBLANES
Z_TAIL = 2 * SUBLANES
SAMPLE_BT = 16

BF16 = jnp.bfloat16
F32 = jnp.float32

HEAD_ORDER = (0, 4, 1, 5, 2, 6, 3, 7)
HEAD_ORDER_INV = tuple(int(i) for i in np.argsort(HEAD_ORDER))

PROJ_COLS = ((0, ATTN_WIDTH), (ATTN_WIDTH, ATTN_WIDTH + 2 * KV_WIDTH),
             (ATTN_WIDTH + 2 * KV_WIDTH, ATTN_WIDTH + 2 * KV_WIDTH + 2 * LRU_WIDTH),
             (ATTN_WIDTH + 2 * KV_WIDTH + 2 * LRU_WIDTH, IN_WIDTH))
PROJ_ROWS = TILE * len(PROJ_COLS) // (TILE // WINDOW)


def _cparams(n_grid):
    return pltpu.CompilerParams(
        dimension_semantics=("arbitrary",) * n_grid,
        vmem_limit_bytes=VMEM_LIMIT_BYTES,
    )


def _whole_spec(shape):
    nd = len(shape)
    return pl.BlockSpec(shape, lambda *_: (0,) * nd, pipeline_mode=pl.Buffered(1))


def _layer_spec(shape, l):
    nd = len(shape) - 1
    return pl.BlockSpec((None,) + tuple(shape[1:]), lambda *_: (l,) + (0,) * nd, pipeline_mode=pl.Buffered(1))


def _layer_norm(x, g, b):
    mu = jnp.mean(x, axis=-1, keepdims=True)
    xc = x - mu
    var = jnp.mean(xc * xc, axis=-1, keepdims=True)
    return xc * lax.rsqrt(var + LN_EPS) * g + b


def _gelu_tanh(x):
    c = np.sqrt(2.0 / np.pi)
    half = 0.5 * x
    return half + half * jnp.tanh(x * (c + (c * 0.044715) * (x * x)))


def _sigmoid_of_neg(z):
    return 1.0 / (1.0 + jnp.exp(z))


def _softplus(x):
    return jnp.maximum(x, 0.0) + jnp.log(1.0 + jnp.exp(-jnp.abs(x)))


def _interleave(*gens):
    active = list(gens)
    while active:
        for g in list(active):
            try:
                next(g)
            except StopIteration:
                active.remove(g)


def _attn_scores(q, k2, bias_t, lo):
    zero = jnp.zeros((), q.dtype)
    tiles = [q[:, c * LANES:(c + 1) * LANES] for c in range(GQA_GROUP)]
    qs = jnp.concatenate([jnp.where(lo, t, zero) for t in tiles]
                         + [jnp.where(lo, zero, t) for t in tiles], axis=0)
    return lax.dot_general(k2, qs, (((1,), (1,)), ((), ())), preferred_element_type=F32) + bias_t


def _attn_probs(s, sink):
    m = jnp.maximum(jnp.max(s, axis=0, keepdims=True), sink)
    p = jnp.exp(s - m)
    denom = jnp.sum(p, axis=0, keepdims=True) + jnp.exp(sink - m)
    return p.astype(BF16), 1.0 / denom


def _attn_values(p, v2):
    return lax.dot_general(v2, p, (((0,), (0,)), ((), ())), preferred_element_type=F32)


def _attn_output(o, inv_denom):
    cols = []
    for c in range(ATTN_WIDTH // LANES):
        kv = (2 * c) // GQA_GROUP
        rows = slice(kv * HEAD_DIM, (kv + 1) * HEAD_DIM)
        heads = [slice(n * WINDOW, (n + 1) * WINDOW) for n in (2 * c, 2 * c + 1)]
        blk = jnp.concatenate([o[rows, h] * inv_denom[:, h] for h in heads], axis=0)
        cols.append(blk.T)
    return jnp.concatenate(cols, axis=1)


def _conv(l, xr, x_tail, cw_ref, cb_ref):
    xe = jnp.concatenate([x_tail, xr], axis=0)
    cw = cw_ref[...]
    xc = cb_ref[l:l + 1, :] + xr * cw[CONV_W - 1:CONV_W, :]
    for tap in range(CONV_W - 1):
        d = CONV_W - 1 - tap
        xc = xc + _shift_rows(xe, d)[X_TAIL:] * cw[tap:tap + 1, :]
    return xc


def _gate_logits(xc, wg_ref):
    return jnp.dot(xc.astype(BF16), wg_ref[...], preferred_element_type=F32)


def _gates(l, g_neg, xc, ba_ref, bx_ref, lam_ref):
    r = _sigmoid_of_neg(g_neg[:, :LRU_WIDTH] - ba_ref[l:l + 1, :])
    i = _sigmoid_of_neg(g_neg[:, LRU_WIDTH:] - bx_ref[l:l + 1, :])
    log_a = r * (-LRU_C * _softplus(-lam_ref[l:l + 1, :]))
    a = jnp.exp(log_a)
    b = jnp.sqrt(1.0 - a * a) * (i * xc)
    return a, b


def _lru_scan(a, b, h_prev):
    t, w = a.shape
    groups = t // SUBLANES
    a3 = a.reshape(groups, SUBLANES, w)
    b3 = b.reshape(groups, SUBLANES, w)
    row = lax.broadcasted_iota(jnp.int32, (groups, SUBLANES, w), 1)
    s = 1
    while s < SUBLANES:
        keep = row >= s
        a_sh = jnp.where(keep, pltpu.roll(a3, s, 1), 1.0)
        b_sh = jnp.where(keep, pltpu.roll(b3, s, 1), 0.0)
        b3 = a3 * b_sh + b3
        a3 = a3 * a_sh
        s *= 2
    hs = []
    h = h_prev
    for g in range(groups):
        hg = a3[g] * h + b3[g]
        hs.append(hg)
        h = hg[SUBLANES - 1:SUBLANES, :]
    return jnp.concatenate(hs, axis=0), h


def _shift_rows(x, d):
    return pltpu.roll(x, d, 0)


def _pool_means(zp, z_tail, pos1):
    ze = jnp.concatenate([z_tail, zp], axis=0)
    t = zp.shape[0]
    lo = lax.broadcasted_iota(jnp.int32, (t, LANES), 1) < POOL_GROUP_W
    s2 = ze + _shift_rows(ze, 1)
    s4 = s2 + _shift_rows(s2, 2)
    hi4 = s4[:, LANES:]
    s8 = hi4 + _shift_rows(hi4, 4)
    s16 = s8 + _shift_rows(s8, 8)
    wins = (jnp.where(lo, s2[Z_TAIL:, :LANES], s4[Z_TAIL:, :LANES]),
            jnp.where(lo, s8[Z_TAIL:], s16[Z_TAIL:]))
    means = []
    for c, win in enumerate(wins):
        w_small, w_big = POOL_WINDOWS[2 * c], POOL_WINDOWS[2 * c + 1]
        if pos1 is None:
            mean = win * jnp.where(lo, 1.0 / w_small, 1.0 / w_big)
        else:
            mean = win / jnp.minimum(pos1, jnp.where(lo, float(w_small), float(w_big)))
        means.append(mean - zp[:, c * LANES:(c + 1) * LANES])
    return jnp.concatenate(means, axis=1)


def _prep_w_in(w_ref, wb_ref):
    lo = lax.broadcasted_iota(jnp.int32, (D_MODEL, LANES), 1) < HEAD_DIM
    src = [w_ref[:, s * LANES:(s + 1) * LANES] * Q_SCALE for s in range(ATTN_WIDTH // LANES)]
    swapped = [pltpu.roll(t, HEAD_DIM, 1) for t in src]
    for c in range(GQA_GROUP):
        s0, s1 = c // 2, GQA_GROUP // 2 + c // 2
        if c % 2 == 0:
            tile = jnp.where(lo, src[s0], swapped[s1])
        else:
            tile = jnp.where(lo, swapped[s0], src[s1])
        wb_ref[:, c * LANES:(c + 1) * LANES] = tile.astype(BF16)
    wb_ref[:, ATTN_WIDTH:] = w_ref[:, ATTN_WIDTH:].astype(BF16)


def _front_kernel(l, tiles_per_seq, xp_ref, xs_ref, w_ref, bias0_ref, bias_ref, sink_ref,
                  cw_ref, cb_ref, wg_ref, ba_ref, bx_ref, lam_ref, wp_ref, ps_ref,
                  wo_f_ref, w1_f_ref, w2_f_ref,
                  attn_ref, rp_ref, h_ref, kt_ref, vt_ref, xt_ref, zt_ref, qs_ref, rs_ref,
                  wo_b_ref, w1_b_ref, w2_b_ref,
                  wb_ref, q_scr, r_scr, kprev, vprev, xtail, ztail, hcar):
    wo_b_ref[...] = wo_f_ref[...].astype(BF16)
    w1_b_ref[...] = w1_f_ref[...].astype(BF16)
    w2_b_ref[...] = w2_f_ref[...].astype(BF16)

    s = pl.program_id(0)
    slot_p = lax.rem(s, 2)
    slot_m = 1 - slot_p
    j = lax.rem(s + tiles_per_seq - 1, tiles_per_seq)

    @pl.when(s == 0)
    def _():
        _prep_w_in(w_ref, wb_ref)
        us = jnp.dot(xs_ref[...].astype(BF16), wb_ref[...], preferred_element_type=F32)
        qs_ref[...] = us[:, :ATTN_WIDTH].astype(BF16)
        rs_ref[...] = us[:, ATTN_WIDTH:]
        q_scr[1] = jnp.zeros(q_scr.shape[1:], q_scr.dtype)
        r_scr[1] = jnp.zeros(r_scr.shape[1:], r_scr.dtype)
        kprev[...] = jnp.zeros_like(kprev)
        vprev[...] = jnp.zeros_like(vprev)

    @pl.when((j == 0) | (s == 0))
    def _():
        xtail[...] = jnp.zeros_like(xtail)
        ztail[...] = jnp.zeros_like(ztail)
        hcar[...] = jnp.zeros_like(hcar)

    carry = {}
    sub_blocks = [slice(c * WINDOW, (c + 1) * WINDOW) for c in range(TILE // WINDOW)]

    def attention():
        lo = lax.broadcasted_iota(jnp.int32, (WINDOW, LANES), 1) < HEAD_DIM
        sink = sink_ref[l:l + 1, :]
        xb16 = xp_ref[...].astype(BF16)
        k_prev, v_prev = kprev[...], vprev[...]
        k = v = None
        for c, rows in enumerate(sub_blocks):
            k = r_scr[slot_m, rows, K_BLK * KV_WIDTH:(K_BLK + 1) * KV_WIDTH]
            v = r_scr[slot_m, rows, V_BLK * KV_WIDTH:(V_BLK + 1) * KV_WIDTH]
            kb, vb = k.astype(BF16), v.astype(BF16)
            bias_t = bias0_ref[...] if c == 0 else bias_ref[...]
            sc = _attn_scores(q_scr[slot_m, rows, :], jnp.concatenate([k_prev, kb], axis=0), bias_t, lo)
            yield
            span, group = divmod(c, len(PROJ_COLS))
            prows = slice(span * PROJ_ROWS, (span + 1) * PROJ_ROWS)
            c0, c1 = PROJ_COLS[group]
            u = jnp.dot(xb16[prows], wb_ref[:, c0:c1], preferred_element_type=F32)
            if group == 0:
                q_scr[slot_p, prows, :] = u.astype(BF16)
            else:
                r_scr[slot_p, prows, c0 - ATTN_WIDTH:c1 - ATTN_WIDTH] = u
            yield
            p, inv_denom = _attn_probs(sc, sink)
            o = _attn_values(p, jnp.concatenate([v_prev, vb], axis=0))
            yield
            attn_ref[rows, :] = _attn_output(o, inv_denom).astype(attn_ref.dtype)
            k_prev, v_prev = kb, vb
        carry.update(k_prev=k_prev, v_prev=v_prev, k=k, v=v)

    def mixers():
        x_tail, z_tail, h = xtail[...], ztail[...], hcar[...]
        for c, rows in enumerate(sub_blocks):
            xr = r_scr[slot_m, rows, XR_BLK * LRU_WIDTH:(XR_BLK + 1) * LRU_WIDTH]
            gr = r_scr[slot_m, rows, GR_BLK * LRU_WIDTH:(GR_BLK + 1) * LRU_WIDTH]
            zp = r_scr[slot_m, rows, ZP_BLK * LRU_WIDTH:(ZP_BLK + 1) * LRU_WIDTH]
            xc = _conv(l, xr, x_tail, cw_ref, cb_ref)
            g = _gate_logits(xc, wg_ref)
            yield
            a, b = _gates(l, g, xc, ba_ref, bx_ref, lam_ref)
            hs, h = _lru_scan(a, b, h)
            rec = hs * _gelu_tanh(gr)
            pos1 = None
            if c == 0:
                pos1 = (j * TILE + lax.broadcasted_iota(jnp.int32, (WINDOW, LANES), 0) + 1).astype(F32)
            diff = _pool_means(zp, z_tail, pos1).astype(BF16)
            pool = jnp.dot(diff, wp_ref[...], preferred_element_type=F32) * ps_ref[l:l + 1, :]
            yield
            rp_ref[rows, :] = jnp.concatenate([rec, pool], axis=1).astype(rp_ref.dtype)
            x_tail, z_tail = xr[WINDOW - X_TAIL:, :], zp[WINDOW - Z_TAIL:, :]
        carry.update(x_tail=x_tail, z_tail=z_tail, h=h)

    _interleave(attention(), mixers())

    kprev[...] = carry["k_prev"]
    vprev[...] = carry["v_prev"]
    xtail[...] = carry["x_tail"]
    ztail[...] = carry["z_tail"]
    hcar[...] = carry["h"]

    @pl.when(j == tiles_per_seq - 1)
    def _():
        kt_ref[...] = carry["k"].T
        vt_ref[...] = carry["v"].T
        h_ref[...] = carry["h"]
        xt_ref[...] = carry["x_tail"]
        zt_ref[...] = carry["z_tail"]


def front_prompt(l, x2d, seq, xs, w_in, bias2, sink_rows, pw, later_w):
    n = x2d.shape[0]
    n_s = xs.shape[0]
    slab = lambda s: jnp.minimum(s, n // TILE - 1)
    slab_rows = [w.shape[1] // (n // TILE) for w in later_w]
    slab_in = [pl.BlockSpec((None, r, w.shape[2]), lambda s: (l, slab(s), 0)) for r, w in zip(slab_rows, later_w)]
    slab_out = [pl.BlockSpec((r, w.shape[2]), lambda s: (slab(s), 0)) for r, w in zip(slab_rows, later_w)]
    b = n // seq
    n_tiles = n // TILE
    tiles_per_seq = seq // TILE
    done = lambda s: jnp.maximum(s - 1, 0)
    seq_of = lambda s: done(s) // tiles_per_seq
    bias_shape = (None, 2 * WINDOW, N_HEADS * WINDOW)
    per_seq = lambda rows, width: pl.BlockSpec((None, rows, width), lambda s: (seq_of(s), 0, 0))
    return pl.pallas_call(
        functools.partial(_front_kernel, l, tiles_per_seq),
        grid=(n_tiles + 1,),
        in_specs=[
            pl.BlockSpec((TILE, D_MODEL), lambda s: (jnp.minimum(s, n_tiles - 1), 0)),
            _whole_spec(xs.shape),
            _layer_spec(w_in.shape, l),
            pl.BlockSpec(bias_shape, lambda s: (jnp.minimum(done(s) % tiles_per_seq, 1), 0, 0)),
            pl.BlockSpec(bias_shape, lambda s: (1, 0, 0), pipeline_mode=pl.Buffered(1)),
            _whole_spec(sink_rows.shape),
            _layer_spec(pw["conv_w"].shape, l), _whole_spec(pw["conv_b"].shape),
            _layer_spec(pw["w_gate"].shape, l), _whole_spec(pw["gate_a_b"].shape),
            _whole_spec(pw["gate_x_b"].shape), _whole_spec(pw["lam"].shape),
            _layer_spec(pw["w_pool"].shape, l), _whole_spec(pw["pool_scale"].shape),
        ] + slab_in,
        out_specs=[
            pl.BlockSpec((TILE, ATTN_WIDTH), lambda s: (done(s), 0)),
            pl.BlockSpec((TILE, LRU_WIDTH + POOL_WIDTH), lambda s: (done(s), 0)),
            per_seq(1, LRU_WIDTH),
            per_seq(KV_WIDTH, WINDOW), per_seq(KV_WIDTH, WINDOW),
            per_seq(X_TAIL, LRU_WIDTH), per_seq(Z_TAIL, POOL_WIDTH),
            pl.BlockSpec((n_s, ATTN_WIDTH), lambda s: (0, 0)),
            pl.BlockSpec((n_s, REST_WIDTH), lambda s: (0, 0)),
        ] + slab_out,
        out_shape=[
            jax.ShapeDtypeStruct((n, ATTN_WIDTH), BF16),
            jax.ShapeDtypeStruct((n, LRU_WIDTH + POOL_WIDTH), BF16),
            jax.ShapeDtypeStruct((b, 1, LRU_WIDTH), F32),
            jax.ShapeDtypeStruct((b, KV_WIDTH, WINDOW), F32),
            jax.ShapeDtypeStruct((b, KV_WIDTH, WINDOW), F32),
            jax.ShapeDtypeStruct((b, X_TAIL, LRU_WIDTH), F32),
            jax.ShapeDtypeStruct((b, Z_TAIL, POOL_WIDTH), F32),
            jax.ShapeDtypeStruct((n_s, ATTN_WIDTH), BF16),
            jax.ShapeDtypeStruct((n_s, REST_WIDTH), F32),
        ] + [jax.ShapeDtypeStruct(w.shape[1:], BF16) for w in later_w],
        scratch_shapes=[
            pltpu.VMEM((D_MODEL, IN_WIDTH), BF16),
            pltpu.VMEM((2, TILE, ATTN_WIDTH), BF16),
            pltpu.VMEM((2, TILE, REST_WIDTH), F32),
            pltpu.VMEM((WINDOW, KV_WIDTH), BF16),
            pltpu.VMEM((WINDOW, KV_WIDTH), BF16),
            pltpu.VMEM((X_TAIL, LRU_WIDTH), F32),
            pltpu.VMEM((Z_TAIL, POOL_WIDTH), F32),
            pltpu.VMEM((1, LRU_WIDTH), F32),
        ],
        compiler_params=_cparams(1),
        name="front_prompt",
    )(x2d, xs, w_in, bias2, bias2, sink_rows, pw["conv_w"], pw["conv_b"], pw["w_gate"], pw["gate_a_b"],
      pw["gate_x_b"], pw["lam"], pw["w_pool"], pw["pool_scale"], *later_w)


def _mixer_sample_kernel(l, n_earlier, q_ref, kn_ref, vn_ref, xr_ref, gr_ref, zp_ref, ck_ref, cv_ref,
                         h0_ref, sc_ref, sp_ref, bias_ref, sink_ref,
                         cw_ref, cb_ref, wg_ref, ba_ref, bx_ref, lam_ref, wp_ref, ps_ref, *rest):
    earlier_k, earlier_v = rest[:n_earlier], rest[n_earlier:2 * n_earlier]
    attn_ref, rp_ref, h_ref, ok_ref, ov_ref = rest[2 * n_earlier:]
    for e in range(n_earlier):
        ok_ref[e] = earlier_k[e][...]
        ov_ref[e] = earlier_v[e][...]
    if n_earlier:
        ok_ref, ov_ref = ok_ref.at[l], ov_ref.at[l]
    bt = SAMPLE_BT
    lo3 = lax.broadcasted_iota(jnp.int32, (bt, GQA_GROUP, LANES), 2) < HEAD_DIM
    q4 = q_ref[...].astype(F32)
    qm = jnp.concatenate([jnp.where(lo3, q4, 0.0), jnp.where(lo3, 0.0, q4)], axis=1)
    s = jnp.einsum("bqc,bck->bqk", qm.astype(BF16), ck_ref[...].astype(BF16),
                   preferred_element_type=F32) + bias_ref[...]
    s_new = jnp.sum(qm * kn_ref[...][:, None, :], axis=2, keepdims=True)
    sink = sink_ref[:, l:l + 1]
    m = jnp.maximum(jnp.maximum(jnp.max(s, axis=2, keepdims=True), s_new), sink)
    p = jnp.exp(s - m)
    p_new = jnp.exp(s_new - m)
    denom = jnp.sum(p, axis=2, keepdims=True) + p_new + jnp.exp(sink - m)
    o = jnp.einsum("bqk,bck->bqc", p.astype(BF16), cv_ref[...].astype(BF16),
                   preferred_element_type=F32)
    o = (o + p_new * vn_ref[...][:, None, :]) / denom
    attn_ref[...] = jnp.where(lo3, o[:, :GQA_GROUP, :], o[:, GQA_GROUP:, :]).astype(attn_ref.dtype)

    xr = xr_ref[...]
    cw = cw_ref[...]
    xc = cb_ref[l:l + 1, :] + xr * cw[CONV_W - 1:CONV_W, :]
    for tap in range(CONV_W - 1):
        xc = xc + sc_ref[tap] * cw[tap:tap + 1, :]
    a, b = _gates(l, _gate_logits(xc, wg_ref), xc, ba_ref, bx_ref, lam_ref)
    h = a * h0_ref[...] + b
    h_ref[...] = h
    rec = h * _gelu_tanh(gr_ref[...])

    z = zp_ref[...]
    lo = lax.broadcasted_iota(jnp.int32, (bt, LANES), 1) < POOL_GROUP_W
    means = []
    for c in range(POOL_WIDTH // LANES):
        w_small, w_big = POOL_WINDOWS[2 * c], POOL_WINDOWS[2 * c + 1]
        cols = slice(c * LANES, (c + 1) * LANES)
        zc = z[:, cols]
        acc = zc
        small = None
        for d in range(1, w_big):
            acc = acc + sp_ref[POOL_CTX - d][:, cols]
            if d + 1 == w_small:
                small = acc
        win = jnp.where(lo, small, acc)
        count = jnp.where(lo, float(w_small), float(w_big))
        means.append(win / count - zc)
    diff = jnp.concatenate(means, axis=1).astype(BF16)
    pool = jnp.dot(diff, wp_ref[...], preferred_element_type=F32) * ps_ref[l:l + 1, :]
    rp_ref[...] = jnp.concatenate([rec, pool], axis=1).astype(rp_ref.dtype)

    last = lax.broadcasted_iota(jnp.int32, (KV_WIDTH, WINDOW), 1) == WINDOW - 1
    pad = jnp.zeros((LANES - bt, KV_WIDTH), F32)
    for src, new, dst in ((ck_ref, kn_ref, ok_ref), (cv_ref, vn_ref, ov_ref)):
        new_t = jnp.concatenate([new[...], pad], axis=0).T
        for s in range(bt):
            shifted = pltpu.roll(src[s], WINDOW - 1, 1)
            col = pltpu.roll(new_t, WINDOW - 1 - s, 1)
            dst[s] = jnp.where(last, col, shifted)


def mixer_sample(l, q4, rs, ck_t, cv_t, state_h, state_conv_t, state_pool_t, bias_s, sinks_t, pw,
                 earlier_k=(), earlier_v=()):
    n = rs.shape[0]
    bt = SAMPLE_BT
    n_earlier = len(earlier_k)
    cache_spec = pl.BlockSpec((None, bt, KV_WIDTH, WINDOW), lambda i: (l, i, 0, 0))
    layer_cache = pl.BlockSpec((bt, KV_WIDTH, WINDOW), lambda i: (i, 0, 0))
    if n_earlier:
        new_cache_spec = pl.BlockSpec((n_earlier + 1, bt, KV_WIDTH, WINDOW), lambda i: (0, i, 0, 0))
        new_cache_shape = jax.ShapeDtypeStruct((n_earlier + 1, n, KV_WIDTH, WINDOW), F32)
    else:
        new_cache_spec = layer_cache
        new_cache_shape = jax.ShapeDtypeStruct((n, KV_WIDTH, WINDOW), F32)
    rcol = lambda width, c: pl.BlockSpec((bt, width), lambda i: (i, c))
    return pl.pallas_call(
        functools.partial(_mixer_sample_kernel, l, n_earlier),
        grid=(n // bt,),
        in_specs=[
            pl.BlockSpec((bt, GQA_GROUP, LANES), lambda i: (i, 0, 0)),
            rcol(KV_WIDTH, K_BLK), rcol(KV_WIDTH, V_BLK),
            rcol(LRU_WIDTH, XR_BLK), rcol(LRU_WIDTH, GR_BLK), rcol(POOL_WIDTH, ZP_BLK),
            cache_spec, cache_spec,
            pl.BlockSpec((None, bt, LRU_WIDTH), lambda i: (l, i, 0)),
            pl.BlockSpec((None, CONV_W - 1, bt, LRU_WIDTH), lambda i: (l, 0, i, 0)),
            pl.BlockSpec((None, POOL_CTX, bt, POOL_WIDTH), lambda i: (l, 0, i, 0)),
            _whole_spec(bias_s.shape),
            _whole_spec(sinks_t.shape),
            _layer_spec(pw["conv_w"].shape, l), _whole_spec(pw["conv_b"].shape),
            _layer_spec(pw["w_gate"].shape, l), _whole_spec(pw["gate_a_b"].shape),
            _whole_spec(pw["gate_x_b"].shape), _whole_spec(pw["lam"].shape),
            _layer_spec(pw["w_pool"].shape, l), _whole_spec(pw["pool_scale"].shape),
        ] + [layer_cache] * (2 * n_earlier),
        out_specs=[
            pl.BlockSpec((bt, GQA_GROUP, LANES), lambda i: (i, 0, 0)),
            pl.BlockSpec((bt, LRU_WIDTH + POOL_WIDTH), lambda i: (i, 0)),
            pl.BlockSpec((bt, LRU_WIDTH), lambda i: (i, 0)),
            new_cache_spec, new_cache_spec,
        ],
        out_shape=[
            jax.ShapeDtypeStruct((n, GQA_GROUP, LANES), BF16),
            jax.ShapeDtypeStruct((n, LRU_WIDTH + POOL_WIDTH), BF16),
            jax.ShapeDtypeStruct((n, LRU_WIDTH), F32),
            new_cache_shape, new_cache_shape,
        ],
        compiler_params=_cparams(1),
        name="mixer_sample",
    )(q4, rs, rs, rs, rs, rs, ck_t, cv_t, state_h, state_conv_t, state_pool_t, bias_s, sinks_t,
      pw["conv_w"], pw["conv_b"], pw["w_gate"], pw["gate_a_b"], pw["gate_x_b"], pw["lam"],
      pw["w_pool"], pw["pool_scale"], *earlier_k, *earlier_v)


def _out_ffn_kernel(l, x_ref, at_ref, rp_ref, xs_ref, ats_ref, rps_ref, wo_ref, g1_ref, b1_ref,
                    w1_ref, w2_ref, g2_ref, b2_ref, y_ref, ys_ref, acc_ref):
    @pl.when(pl.program_id(0) == pl.num_programs(0) - 1)
    def _():
        mix = jnp.dot(ats_ref[...], wo_ref[:ATTN_WIDTH, :], preferred_element_type=F32)
        mix = mix + jnp.dot(rps_ref[...], wo_ref[ATTN_WIDTH:, :], preferred_element_type=F32)
        x1 = _layer_norm(ALPHA * xs_ref[...] + mix, g1_ref[l:l + 1, :], b1_ref[l:l + 1, :])
        hid = jnp.dot(x1.astype(BF16), w1_ref[...], preferred_element_type=F32)
        hid = jnp.square(jnp.maximum(hid, 0.0)).astype(BF16)
        ffn = jnp.dot(hid, w2_ref[...], preferred_element_type=F32)
        ys_ref[...] = _layer_norm(ALPHA * x1 + ffn, g2_ref[l:l + 1, :], b2_ref[l:l + 1, :])

    tm = x_ref.shape[0]
    sub = FFN_SUB
    groups = [slice(g * sub, (g + 1) * sub) for g in range(tm // sub)]
    mixes = []
    for rows in groups:
        mix = jnp.dot(at_ref[rows, :], wo_ref[:ATTN_WIDTH, :], preferred_element_type=F32)
        mixes.append(mix + jnp.dot(rp_ref[rows, :], wo_ref[ATTN_WIDTH:, :], preferred_element_type=F32))
    x1s = [_layer_norm(ALPHA * x_ref[rows, :] + mix, g1_ref[l:l + 1, :], b1_ref[l:l + 1, :])
           for rows, mix in zip(groups, mixes)]
    x1bs = [x1.astype(BF16) for x1 in x1s]
    for c in range(D_FF // FFN_FC):
        cols = slice(c * FFN_FC, (c + 1) * FFN_FC)
        for rows, x1b in zip(groups, x1bs):
            hid = jnp.dot(x1b, w1_ref[:, cols], preferred_element_type=F32)
            hid = jnp.square(jnp.maximum(hid, 0.0)).astype(BF16)
            part = jnp.dot(hid, w2_ref[cols, :], preferred_element_type=F32)
            if c == 0:
                acc_ref[rows, :] = part
            else:
                acc_ref[rows, :] += part
    for rows, x1 in zip(groups, x1s):
        y_ref[rows, :] = _layer_norm(ALPHA * x1 + acc_ref[rows, :], g2_ref[l:l + 1, :], b2_ref[l:l + 1, :])


def out_ffn(l, x2d, attn2d, rp2d, xs, attn_s, rp_s, w_out_b, w_ff1_b, w_ff2_b, fw):
    n = x2d.shape[0]
    tm = FFN_TM
    row = lambda width: pl.BlockSpec((tm, width), lambda i: (i, 0))
    vec = _whole_spec((DEPTH, D_MODEL))
    return pl.pallas_call(
        functools.partial(_out_ffn_kernel, l),
        grid=(n // tm,),
        in_specs=[
            row(D_MODEL), row(ATTN_WIDTH), row(LRU_WIDTH + POOL_WIDTH),
            _whole_spec(xs.shape), _whole_spec(attn_s.shape), _whole_spec(rp_s.shape),
            _whole_spec(w_out_b.shape), vec, vec,
            _whole_spec(w_ff1_b.shape), _whole_spec(w_ff2_b.shape), vec, vec,
        ],
        out_specs=[row(D_MODEL), pl.BlockSpec(xs.shape, lambda i: (0, 0))],
        out_shape=[jax.ShapeDtypeStruct((n, D_MODEL), F32), jax.ShapeDtypeStruct(xs.shape, F32)],
        scratch_shapes=[pltpu.VMEM((tm, D_MODEL), F32)],
        compiler_params=_cparams(1),
        name="out_ffn",
    )(x2d, attn2d, rp2d, xs, attn_s, rp_s, w_out_b, fw["ln1_g"], fw["ln1_b"],
      w_ff1_b, w_ff2_b, fw["ln2_g"], fw["ln2_b"])


def _block_diag(w):
    depth, g, c, d = w.shape
    eye = jnp.eye(g, dtype=bool)[None, :, None, :, None]
    return jnp.where(eye, w[:, :, :, None, :], 0.0).reshape(depth, g * c, g * d)


def _alibi_slopes():
    return np.exp2(-8.0 * (np.arange(N_HEADS, dtype=np.float32) + 1.0) / N_HEADS).astype(np.float32)


def _prompt_bias_tables():
    slopes = _alibi_slopes()
    jk = np.arange(2 * WINDOW)[:, None]
    tq = np.arange(WINDOW)[None, :]
    delta = tq + WINDOW - jk
    visible = (delta >= 0) & (delta <= WINDOW)
    bias = -slopes[None, :, None] * delta.astype(np.float32)[:, None, :]
    full = np.where(visible[:, None, :], bias, np.float32(NEG_INF))
    first = np.where((visible & (jk >= WINDOW))[:, None, :], bias, np.float32(NEG_INF))
    return np.stack([first, full]).reshape(2, 2 * WINDOW, N_HEADS * WINDOW).astype(np.float32)


def kernel(x_prompt, x_sample, cache_k, cache_v, state_h, state_conv, state_pool, w_in, attn_sinks, conv_w, conv_b, gate_a_w, gate_a_b, gate_x_w, gate_x_b, lru_lambda, pool_w, pool_scale, w_out, ln1_g, ln1_b, w_ff1, w_ff2, ln2_g, ln2_b):
    batch, seq, _ = x_prompt.shape
    dec = x_sample.shape[0]
    bias_prompt = jnp.asarray(_prompt_bias_tables())
    bias_sample = jnp.asarray(-_alibi_slopes()[:, None] * (WINDOW - np.arange(WINDOW, dtype=np.float32))[None, :])
    sink_rows = jnp.repeat(attn_sinks, WINDOW, axis=1)
    sinks_t = attn_sinks.T

    pw = {
        "conv_w": conv_w, "conv_b": conv_b,
        "w_gate": (-jnp.concatenate([_block_diag(gate_a_w), _block_diag(gate_x_w)], axis=2)).astype(BF16),
        "gate_a_b": gate_a_b, "gate_x_b": gate_x_b, "lam": lru_lambda,
        "w_pool": _block_diag(pool_w).astype(BF16), "pool_scale": pool_scale,
    }
    fw = {"ln1_g": ln1_g, "ln1_b": ln1_b, "ln2_g": ln2_g, "ln2_b": ln2_b}

    def kv_view(c):
        return jnp.transpose(c, (0, 1, 3, 4, 2)).reshape(c.shape[0], c.shape[1], KV_WIDTH, WINDOW)

    def kv_unview(c_t):
        c5 = c_t.reshape(c_t.shape[0], c_t.shape[1], N_KV_HEADS, HEAD_DIM, WINDOW)
        return jnp.transpose(c5, (0, 1, 4, 2, 3))

    ck_t, cv_t = kv_view(cache_k), kv_view(cache_v)
    state_conv_t = jnp.swapaxes(state_conv, 1, 2)
    state_pool_t = jnp.swapaxes(state_pool, 1, 2)

    yp = x_prompt.reshape(batch * seq, D_MODEL)
    ys = x_sample.reshape(dec, D_MODEL)
    xr_cols = slice(XR_BLK * LRU_WIDTH, (XR_BLK + 1) * LRU_WIDTH)
    zp_cols = slice(ZP_BLK * LRU_WIDTH, (ZP_BLK + 1) * LRU_WIDTH)
    outs = {k: [] for k in ("pk", "pv", "ph", "pc", "pp", "sh", "sc", "sp")}
    new_k, new_v = [], []
    for l in range(DEPTH):
        attn, rp, h_last, k_last_t, v_last_t, x_tail, z_tail, qs, rs, w_out_b, w_ff1_b, w_ff2_b = front_prompt(
            l, yp, seq, ys, w_in, bias_prompt, sink_rows, pw, (w_out, w_ff1, w_ff2))
        last = l == DEPTH - 1
        attn_s, rp_s, h_s, k_new_t, v_new_t = mixer_sample(
            l, qs.reshape(dec, GQA_GROUP, LANES), rs, ck_t, cv_t, state_h, state_conv_t, state_pool_t,
            bias_sample, sinks_t, pw, *((new_k, new_v) if last else ()))
        new_k.append(k_new_t)
        new_v.append(v_new_t)
        attn_s = attn_s.reshape(dec, N_HEADS, HEAD_DIM)[:, HEAD_ORDER_INV, :].reshape(dec, ATTN_WIDTH)
        yp, ys = out_ffn(l, yp, attn, rp, ys, attn_s, rp_s, w_out_b, w_ff1_b, w_ff2_b, fw)
        outs["pk"].append(k_last_t)
        outs["pv"].append(v_last_t)
        outs["ph"].append(h_last.reshape(batch, LRU_WIDTH))
        outs["pc"].append(x_tail[:, X_TAIL - (CONV_W - 1):, :])
        outs["pp"].append(z_tail[:, Z_TAIL - POOL_CTX:, :])
        outs["sh"].append(h_s)
        outs["sc"].append(jnp.concatenate([state_conv[l][:, 1:], rs[:, None, xr_cols]], axis=1))
        outs["sp"].append(jnp.concatenate([state_pool[l][:, 1:], rs[:, None, zp_cols]], axis=1))

    st = {k: jnp.stack(v) for k, v in outs.items()}
    sk_t, sv_t = new_k[-1], new_v[-1]
    return (yp.reshape(batch, seq, D_MODEL), ys.reshape(dec, 1, D_MODEL),
            kv_unview(st["pk"]), kv_unview(st["pv"]), st["ph"], st["pc"], st["pp"],
            kv_unview(sk_t), kv_unview(sv_t), st["sh"], st["sc"], st["sp"])
```
